```python
import math
import jax, jax.numpy as jnp
from jax import lax
import numpy as np

D_MODEL = 4096
BATCH = 4
SEQ = 4096
DEPTH = 2

N_A_LAYERS = DEPTH // 2
N_B_LAYERS = DEPTH - N_A_LAYERS
N_MOD = 6
NORM_EPS = 1e-6
SSM_WIDTH = D_MODEL
SSM_GROUP = 16
SSM_GROUPS = SSM_WIDTH // SSM_GROUP
SSM_STATE = 64
SSM_CHUNK = 128
DT_MIN = 1e-3
DT_MAX = 1e-1
HEAD_DIM = 128
N_Q_HEADS = D_MODEL // HEAD_DIM
N_KV_HEADS = 8
ROT_DIM = HEAD_DIM // 4
ROPE_THETA = 500000.0
DILATION_GROUPS = ((128, 1), (512, 4), (2048, 16))
N_DIL = len(DILATION_GROUPS)
ATTN_BLOCK = 128
N_EXPERTS = 32
D_EXPERT = 512
TOP_K = 4
N_EXPERT_GROUPS = 8
TOPK_GROUPS = 4
D_SHARED = 1024
ROUTED_SCALE = 2.5

kernel_name = "yoco_s5_dilated_attn_moe_trunk"


def rmsnorm(x, g):
    xf = x.astype(jnp.float32)
    var = jnp.mean(xf * xf, axis=-1, keepdims=True)
    return (xf * lax.rsqrt(var + NORM_EPS) * g.astype(jnp.float32)).astype(x.dtype)


def partial_rotary(t, positions):
    half = ROT_DIM // 2
    inv_freq = ROPE_THETA ** (-jnp.arange(half, dtype=jnp.float32) * 2.0 / ROT_DIM)
    ang = positions.astype(jnp.float32)[:, :, None] * inv_freq
    cos = jnp.cos(ang)[:, :, None, :]
    sin = jnp.sin(ang)[:, :, None, :]
    tf = t.astype(jnp.float32)
    x1 = tf[..., :half]
    x2 = tf[..., half:ROT_DIM]
    out = jnp.concatenate([x1 * cos - x2 * sin, x2 * cos + x1 * sin, tf[..., ROT_DIM:]], axis=-1)
    return out.astype(t.dtype)


def s5_mixer(h, w_in, a_re, a_im, log_step, b_re, b_im, c_re, c_im, d_skip, glu_w, glu_b, w_out):
    bsz, seq, _ = h.shape
    f32 = jnp.float32
    u = (h @ w_in).astype(f32)
    ug = u.reshape(bsz, seq, SSM_GROUPS, SSM_GROUP)
    step = jnp.exp(log_step.astype(f32))[:, None]
    lr = a_re.astype(f32)
    li = a_im.astype(f32)
    decay = jnp.exp(lr * step)
    abar_re = decay * jnp.cos(li * step)
    abar_im = decay * jnp.sin(li * step)
    denom = lr * lr + li * li
    nr = abar_re - 1.0
    ni = abar_im
    f_re = (nr * lr + ni * li) / denom
    f_im = (ni * lr - nr * li) / denom
    br = b_re.astype(f32)
    bi = b_im.astype(f32)
    bbar_re = f_re[..., None] * br - f_im[..., None] * bi
    bbar_im = f_re[..., None] * bi + f_im[..., None] * br
    cr = c_re.astype(f32)
    ci = c_im.astype(f32)
    n_chunks = seq // SSM_CHUNK
    u_chunks = ug.reshape(bsz, n_chunks, SSM_CHUNK, SSM_GROUPS, SSM_GROUP).transpose(1, 2, 0, 3, 4)
    elem_shape = (SSM_CHUNK, bsz, SSM_GROUPS, SSM_STATE)
    ab_re = jnp.broadcast_to(abar_re, elem_shape)
    ab_im = jnp.broadcast_to(abar_im, elem_shape)

    def combine(e1, e2):
        a1r, a1i, b1r, b1i = e1
        a2r, a2i, b2r, b2i = e2
        return (a2r * a1r - a2i * a1i, a2r * a1i + a2i * a1r,
                a2r * b1r - a2i * b1i + b2r, a2r * b1i + a2i * b1r + b2i)

    def chunk_step(carry, u_c):
        h_re, h_im = carry
        bu_re = jnp.einsum('lbgi,gpi->lbgp', u_c, bbar_re)
        bu_im = jnp.einsum('lbgi,gpi->lbgp', u_c, bbar_im)
        ar, ai, sr, si = lax.associative_scan(combine, (ab_re, ab_im, bu_re, bu_im), axis=0)
        x_re = sr + ar * h_re - ai * h_im
        x_im = si + ar * h_im + ai * h_re
        y_c = jnp.einsum('lbgp,gip->lbgi', x_re, cr) - jnp.einsum('lbgp,gip->lbgi', x_im, ci)
        return (x_re[-1], x_im[-1]), y_c

    init = (jnp.zeros((bsz, SSM_GROUPS, SSM_STATE), f32), jnp.zeros((bsz, SSM_GROUPS, SSM_STATE), f32))
    _, y = lax.scan(chunk_step, init, u_chunks)
    y = y.transpose(2, 0, 1, 3, 4).reshape(bsz, seq, SSM_WIDTH)
    y = (y + d_skip.astype(f32) * u).astype(h.dtype)
    z = jax.nn.gelu(y)
    z = z * jax.nn.sigmoid(z @ glu_w + glu_b)
    return z @ w_out


def shared_kv(x, kv_norm_g, w_k, w_v, positions):
    bsz, seq, _ = x.shape
    hkv = rmsnorm(x, kv_norm_g)
    k = partial_rotary((hkv @ w_k).reshape(bsz, seq, N_KV_HEADS, HEAD_DIM), positions)
    v = (hkv @ w_v).reshape(bsz, seq, N_KV_HEADS, HEAD_DIM)
    return k, v


def dilated_window_attention(q, k, v, steps, dilation):
    bsz, seq, _, _ = q.shape
    rep = N_Q_HEADS // N_KV_HEADS
    m_len = seq // dilation
    blk = min(ATTN_BLOCK, m_len)
    n_blk = -(-m_len // blk)
    m_pad = n_blk * blk

    def to_blocks(t):
        t = t.reshape(bsz, m_len, dilation, t.shape[2], t.shape[3])
        t = jnp.pad(t, ((0, 0), (0, m_pad - m_len), (0, 0), (0, 0), (0, 0)))
        return t.reshape(bsz, n_blk, blk, dilation, t.shape[3], t.shape[4])

    def with_prev(t):
        prev = jnp.pad(t[:, :-1], ((0, 0), (1, 0), (0, 0), (0, 0), (0, 0), (0, 0)))
        return jnp.concatenate([prev, t], axis=2)

    qb = to_blocks(q).reshape(bsz, n_blk, blk, dilation, N_KV_HEADS, rep, HEAD_DIM)
    kc = with_prev(to_blocks(k))
    vc = with_prev(to_blocks(v))
    qi = jnp.arange(blk)[:, None]
    kj = jnp.arange(2 * blk)[None, :]
    dist = qi + blk - kj
    band = (dist >= 0) & (dist <= steps)

    def block_attn(args):
        n, qn, kn, vn = args
        s = jnp.einsum('bqrhgk,bsrhk->brhgqs', qn, kn, preferred_element_type=jnp.float32)
        valid = band & ((n > 0) | (kj >= blk))
        s = jnp.where(valid, s, -1e30)
        mx = jnp.max(s, axis=-1, keepdims=True)
        p = jnp.exp(s - mx)
        den = jnp.sum(p, axis=-1, keepdims=True)
        o = jnp.einsum('brhgqs,bsrhk->bqrhgk', (p / den).astype(vn.dtype), vn,
                       preferred_element_type=jnp.float32)
        lse = (mx + jnp.log(den))[..., 0]
        return o.astype(qn.dtype), jnp.moveaxis(lse, -1, 1)

    xs = (jnp.arange(n_blk), jnp.moveaxis(qb, 1, 0), jnp.moveaxis(kc, 1, 0), jnp.moveaxis(vc, 1, 0))
    o, lse = lax.map(block_attn, xs)
    o = jnp.moveaxis(o, 0, 1).reshape(bsz, m_pad, dilation, N_Q_HEADS, HEAD_DIM)[:, :m_len]
    o = o.reshape(bsz, seq, N_Q_HEADS, HEAD_DIM)
    lse = jnp.moveaxis(lse, 0, 1).reshape(bsz, m_pad, dilation, N_Q_HEADS)[:, :m_len]
    lse = lse.reshape(bsz, seq, N_Q_HEADS)
    return o, lse


def dilated_attention_mixer(h, positions, w_q, w_o, k, v):
    bsz, seq, _ = h.shape
    q = (h @ w_q).reshape(bsz, seq, N_DIL * N_Q_HEADS, HEAD_DIM)
    q = partial_rotary(q, positions) * (HEAD_DIM ** -0.5)
    q = q.reshape(bsz, seq, N_DIL, N_Q_HEADS, HEAD_DIM)
    outs = []
    lses = []
    for g, (window, dil) in enumerate(DILATION_GROUPS):
        o_g, l_g = dilated_window_attention(q[:, :, g], k, v, window // dil, dil)
        outs.append(o_g)
        lses.append(l_g)
    wts = jax.nn.softmax(jnp.stack(lses, axis=2), axis=2)
    o = outs[0].astype(jnp.float32) * wts[:, :, 0, :, None]
    for g in range(1, N_DIL):
        o = o + outs[g].astype(jnp.float32) * wts[:, :, g, :, None]
    o = o.astype(h.dtype).reshape(bsz, seq, N_Q_HEADS * HEAD_DIM)
    return o @ w_o


def swiglu(t, wg, wu, wd):
    return (jax.nn.silu(t @ wg) * (t @ wu)) @ wd


def moe_ffn(h, router_w, router_bias, e_gate, e_up, e_down, s_gate, s_up, s_down):
    bsz, seq, dm = h.shape
    f32 = jnp.float32
    t = h.reshape(bsz * seq, dm)
    scores = jax.nn.sigmoid(jnp.einsum('td,de->te', t, router_w, preferred_element_type=f32))
    biased = scores + router_bias.astype(f32)
    grouped = biased.reshape(-1, N_EXPERT_GROUPS, N_EXPERTS // N_EXPERT_GROUPS)
    group_score = jnp.sum(lax.top_k(grouped, 2)[0], axis=-1)
    _, top_groups = lax.top_k(group_score, TOPK_GROUPS)
    group_mask = jnp.sum(jax.nn.one_hot(top_groups, N_EXPERT_GROUPS, dtype=f32), axis=1)
    expert_mask = jnp.repeat(group_mask, N_EXPERTS // N_EXPERT_GROUPS, axis=1) > 0
    _, top_idx = lax.top_k(jnp.where(expert_mask, biased, -jnp.inf), TOP_K)
    top_w = jnp.take_along_axis(scores, top_idx, axis=1)
    top_w = top_w / jnp.sum(top_w, axis=-1, keepdims=True) * ROUTED_SCALE
    gates = jnp.sum(jax.nn.one_hot(top_idx, N_EXPERTS, dtype=f32) * top_w[..., None], axis=1).astype(h.dtype)
    out = swiglu(t, s_gate, s_up, s_down)
    for e in range(N_EXPERTS):
        out = out + gates[:, e:e + 1] * swiglu(t, e_gate[e], e_up[e], e_down[e])
    return out.reshape(bsz, seq, dm)


def setup_inputs(seed: int = 0) -> dict:
    key = jax.random.key(seed)
    keys = jax.random.split(key, 40)
    counter = [0]

    def nk():
        counter[0] += 1
        return keys[counter[0] - 1]

    f32 = jnp.float32

    def nrm(shape, scale):
        return jax.random.normal(nk(), shape, f32) * scale

    qdim = N_DIL * N_Q_HEADS * HEAD_DIM
    odim = N_Q_HEADS * HEAD_DIM
    kvdim = N_KV_HEADS * HEAD_DIM
    n_idx = jnp.arange(SSM_STATE, dtype=f32)
    positions = (jax.random.randint(nk(), (BATCH, 1), 0, 1024, dtype=jnp.int32)
                 + jnp.arange(SEQ, dtype=jnp.int32)[None, :])
    return {
        "x": nrm((BATCH, SEQ, D_MODEL), 1.0),
        "c": nrm((BATCH, D_MODEL), 1.0),
        "positions": positions,
        "ada_w": nrm((DEPTH, D_MODEL, N_MOD * D_MODEL), 0.5 * D_MODEL ** -0.5),
        "ada_b": nrm((DEPTH, N_MOD * D_MODEL), 0.02),
        "norm_g": 1.0 + nrm((DEPTH, 4, D_MODEL), 0.02),
        "ssm_w_in": nrm((N_A_LAYERS, D_MODEL, SSM_WIDTH), D_MODEL ** -0.5),
        "ssm_a_re": -0.5 + nrm((N_A_LAYERS, SSM_GROUPS, SSM_STATE), 0.01),
        "ssm_a_im": math.pi * n_idx + nrm((N_A_LAYERS, SSM_GROUPS, SSM_STATE), 0.01),
        "ssm_log_step": jax.random.uniform(nk(), (N_A_LAYERS, SSM_GROUPS), f32,
                                           math.log(DT_MIN), math.log(DT_MAX)),
        "ssm_b_re": nrm((N_A_LAYERS, SSM_GROUPS, SSM_STATE, SSM_GROUP), 2 ** -0.5),
        "ssm_b_im": nrm((N_A_LAYERS, SSM_GROUPS, SSM_STATE, SSM_GROUP), 2 ** -0.5),
        "ssm_c_re": nrm((N_A_LAYERS, SSM_GROUPS, SSM_GROUP, SSM_STATE), (2 * SSM_STATE) ** -0.5),
        "ssm_c_im": nrm((N_A_LAYERS, SSM_GROUPS, SSM_GROUP, SSM_STATE), (2 * SSM_STATE) ** -0.5),
        "ssm_d": nrm((N_A_LAYERS, SSM_WIDTH), 0.5),
        "glu_w": nrm((N_A_LAYERS, SSM_WIDTH, SSM_WIDTH), SSM_WIDTH ** -0.5),
        "glu_b": nrm((N_A_LAYERS, SSM_WIDTH), 0.02),
        "ssm_w_out": nrm((N_A_LAYERS, SSM_WIDTH, D_MODEL), SSM_WIDTH ** -0.5),
        "kv_norm_g": 1.0 + nrm((D_MODEL,), 0.02),
        "w_k": nrm((D_MODEL, kvdim), D_MODEL ** -0.5),
        "w_v": nrm((D_MODEL, kvdim), D_MODEL ** -0.5),
        "attn_w_q": nrm((N_B_LAYERS, D_MODEL, qdim), D_MODEL ** -0.5),
        "attn_w_o": nrm((N_B_LAYERS, odim, D_MODEL), odim ** -0.5),
        "router_w": nrm((DEPTH, D_MODEL, N_EXPERTS), D_MODEL ** -0.5),
        "router_bias": nrm((DEPTH, N_EXPERTS), 0.01),
        "exp_w_gate": nrm((DEPTH, N_EXPERTS, D_MODEL, D_EXPERT), D_MODEL ** -0.5),
        "exp_w_up": nrm((DEPTH, N_EXPERTS, D_MODEL, D_EXPERT), D_MODEL ** -0.5),
        "exp_w_down": nrm((DEPTH, N_EXPERTS, D_EXPERT, D_MODEL), D_EXPERT ** -0.5),
        "sh_w_gate": nrm((DEPTH, D_MODEL, D_SHARED), D_MODEL ** -0.5),
        "sh_w_up": nrm((DEPTH, D_MODEL, D_SHARED), D_MODEL ** -0.5),
        "sh_w_down": nrm((DEPTH, D_SHARED, D_MODEL), D_SHARED ** -0.5),
    }


def reference(x, c, positions, ada_w, ada_b, norm_g, ssm_w_in, ssm_a_re, ssm_a_im, ssm_log_step,
              ssm_b_re, ssm_b_im, ssm_c_re, ssm_c_im, ssm_d, glu_w, glu_b, ssm_w_out,
              kv_norm_g, w_k, w_v, attn_w_q, attn_w_o, router_w, router_bias,
              exp_w_gate, exp_w_up, exp_w_down, sh_w_gate, sh_w_up, sh_w_down):
    cond = jax.nn.silu(c)
    k = None
    v = None
    for layer in range(DEPTH):
        mod = (cond @ ada_w[layer] + ada_b[layer])[:, None, :]
        sh1, sc1, g1, sh2, sc2, g2 = jnp.split(mod, N_MOD, axis=-1)
        h = rmsnorm(x, norm_g[layer, 0]) * (1.0 + sc1) + sh1
        if layer < N_A_LAYERS:
            a = layer
            y = s5_mixer(h, ssm_w_in[a], ssm_a_re[a], ssm_a_im[a], ssm_log_step[a],
                         ssm_b_re[a], ssm_b_im[a], ssm_c_re[a], ssm_c_im[a], ssm_d[a],
                         glu_w[a], glu_b[a], ssm_w_out[a])
        else:
            if layer == N_A_LAYERS:
                k, v = shared_kv(x, kv_norm_g, w_k, w_v, positions)
            bl = layer - N_A_LAYERS
            y = dilated_attention_mixer(h, positions, attn_w_q[bl], attn_w_o[bl], k, v)
        x = x + g1 * rmsnorm(y, norm_g[layer, 1])
        h = rmsnorm(x, norm_g[layer, 2]) * (1.0 + sc2) + sh2
        y = moe_ffn(h, router_w[layer], router_bias[layer], exp_w_gate[layer], exp_w_up[layer],
                    exp_w_down[layer], sh_w_gate[layer], sh_w_up[layer], sh_w_down[layer])
        x = x + g2 * rmsnorm(y, norm_g[layer, 3])
    return x
```

```python
import functools
import math

import jax
import jax.numpy as jnp
from jax import lax
from jax.experimental import pallas as pl
from jax.experimental.pallas import tpu as pltpu

F32 = jnp.float32
BF16 = jnp.bfloat16
I32 = jnp.int32

LANE = 128
SUBLANE = 8
VMEM_LIMIT_BYTES = 56 * 1024 * 1024
MXU_DIM = 256

N_MOD = 6
NORM_EPS = 1e-6
HEAD_DIM = 128
ROT_DIM = HEAD_DIM // 4
ROT_HALF = ROT_DIM // 2
ROPE_THETA = 500000.0
DILATION_GROUPS = ((128, 1), (512, 4), (2048, 16))
ATTN_BLOCK = 128
N_EXPERT_GROUPS = 8
TOPK_GROUPS = 4
TOP_K = 4
ROUTED_SCALE = 2.5
NEG_MASK = -1e30

SSM_PACK_CH = MXU_DIM
SSM_CHUNK_TOKENS = 256
SSM_LANE_CHUNK = 512
EXPERT_TILE = 256
ROW_TILE = 256
ROUTER_TILE = 512


def _params(*sem):
    return pltpu.CompilerParams(dimension_semantics=sem, vmem_limit_bytes=VMEM_LIMIT_BYTES)


def _ada_kernel(c_ref, w_ref, b_ref, o_ref):
    cond = jax.nn.silu(c_ref[...])
    o_ref[...] = jnp.dot(cond, w_ref[...], precision=lax.Precision.HIGHEST,
                         preferred_element_type=F32) + b_ref[...]


def ada_modulation(c, ada_w, ada_b):
    depth, d, n = ada_w.shape
    bsz = c.shape[0]
    rows = -(-bsz // SUBLANE) * SUBLANE
    cp = jnp.zeros((rows, d), F32).at[:bsz].set(c)
    tn = 512
    out = pl.pallas_call(
        _ada_kernel,
        grid=(depth, n // tn),
        in_specs=[pl.BlockSpec((rows, d), lambda l, j: (0, 0)),
                  pl.BlockSpec((None, d, tn), lambda l, j: (l, 0, j)),
                  pl.BlockSpec((None, 1, tn), lambda l, j: (l, 0, j))],
        out_specs=pl.BlockSpec((None, rows, tn), lambda l, j: (l, 0, j)),
        out_shape=jax.ShapeDtypeStruct((depth, rows, n), F32),
        compiler_params=_params("parallel", "parallel"),
        name="ada_modulation",
    )(cp, ada_w, ada_b.reshape(depth, 1, n))
    return out[:, :bsz]


def _normmod_kernel(x_ref, g_ref, sc_ref, sh_ref, *rest, with_kv, with_f32):
    x = x_ref[...]
    xn = x * lax.rsqrt(jnp.mean(x * x, axis=-1, keepdims=True) + NORM_EPS)
    h = xn * g_ref[...] * (1.0 + sc_ref[...]) + sh_ref[...]
    rest = list(rest)
    if with_kv:
        kvg_ref = rest.pop(0)
    h_ref = rest.pop(0)
    h_ref[...] = h.astype(h_ref.dtype)
    if with_kv:
        rest.pop(0)[...] = (xn * kvg_ref[...]).astype(BF16)
    if with_f32:
        rest.pop(0)[...] = h


def norm_modulate(x, g, sc, sh, kv_g=None, with_f32=False):
    bsz, seq, d = x.shape
    ts = ROW_TILE
    nt = seq // ts
    row = lambda b, i: (b * nt + i, 0)
    vec = pl.BlockSpec((1, d), lambda b, i: (0, 0))
    bvec = pl.BlockSpec((None, 1, d), lambda b, i: (b, 0, 0))
    in_specs = [pl.BlockSpec((None, ts, d), lambda b, i: (b, i, 0)), vec, bvec, bvec]
    args = [x, g.reshape(1, d), sc.reshape(bsz, 1, d), sh.reshape(bsz, 1, d)]
    out_specs = [pl.BlockSpec((ts, d), row)]
    out_shape = [jax.ShapeDtypeStruct((bsz * seq, d), BF16)]
    if kv_g is not None:
        in_specs.append(vec)
        args.append(kv_g.reshape(1, d))
        out_specs.append(pl.BlockSpec((ts, d), row))
        out_shape.append(jax.ShapeDtypeStruct((bsz * seq, d), BF16))
    if with_f32:
        out_specs.append(pl.BlockSpec((ts, d), row))
        out_shape.append(jax.ShapeDtypeStruct((bsz * seq, d), F32))
    return pl.pallas_call(
        functools.partial(_normmod_kernel, with_kv=kv_g is not None, with_f32=with_f32),
        grid=(bsz, nt), in_specs=in_specs, out_specs=out_specs, out_shape=out_shape,
        compiler_params=_params("parallel", "parallel"), name="norm_modulate",
    )(*args)


def _resid_norm_kernel(x_ref, y_ref, g_ref, gate_ref, o_ref):
    y = y_ref[...].astype(F32)
    yn = y * lax.rsqrt(jnp.mean(y * y, axis=-1, keepdims=True) + NORM_EPS) * g_ref[...]
    o_ref[...] = x_ref[...] + gate_ref[...] * yn


def residual_norm(x, y, g, gate):
    bsz, seq, d = x.shape
    ts = ROW_TILE
    nt = seq // ts
    return pl.pallas_call(
        _resid_norm_kernel,
        grid=(bsz, nt),
        in_specs=[pl.BlockSpec((None, ts, d), lambda b, i: (b, i, 0)),
                  pl.BlockSpec((ts, d), lambda b, i: (b * nt + i, 0)),
                  pl.BlockSpec((1, d), lambda b, i: (0, 0)),
                  pl.BlockSpec((None, 1, d), lambda b, i: (b, 0, 0))],
        out_specs=pl.BlockSpec((None, ts, d), lambda b, i: (b, i, 0)),
        out_shape=jax.ShapeDtypeStruct(x.shape, F32),
        compiler_params=_params("parallel", "parallel"), name="residual_norm",
    )(x, y, g.reshape(1, d), gate.reshape(bsz, 1, d))


def _mm_kernel(a_ref, w_ref, o_ref):
    o_ref[...] = jnp.dot(a_ref[...], w_ref[...], preferred_element_type=F32).astype(o_ref.dtype)


def _mm_glu_kernel(a_ref, w_ref, z_ref, b_ref, o_ref):
    acc = jnp.dot(a_ref[...], w_ref[...], preferred_element_type=F32) + b_ref[...]
    o_ref[...] = (z_ref[...].astype(F32) * jax.nn.sigmoid(acc)).astype(o_ref.dtype)


def _mm_swiglu_kernel(a_ref, wg_ref, wu_ref, o_ref):
    a = a_ref[...]
    g = jnp.dot(a, wg_ref[...], preferred_element_type=F32)
    u = jnp.dot(a, wu_ref[...], preferred_element_type=F32)
    o_ref[...] = (jax.nn.silu(g) * u).astype(o_ref.dtype)


def _mm_rotary_kernel(a_ref, w_ref, cos_ref, sin_lo_ref, sin_hi_ref, o_ref):
    acc = jnp.dot(a_ref[...], w_ref[...], preferred_element_type=F32)
    cos, sin_lo, sin_hi = cos_ref[...], sin_lo_ref[...], sin_hi_ref[...]
    for hd in range(acc.shape[1] // HEAD_DIM):
        cols = slice(hd * HEAD_DIM, (hd + 1) * HEAD_DIM)
        t = acc[:, cols]
        rot = (t * cos + pltpu.roll(t, HEAD_DIM - ROT_HALF, 1) * sin_lo + pltpu.roll(t, ROT_HALF, 1) * sin_hi)
        o_ref[:, cols] = rot.astype(o_ref.dtype)


def matmul(a, w, out_dtype, *, tm, tn, name):
    m, k = a.shape
    n = w.shape[1]
    tm, tn = min(tm, m), min(tn, n)
    return pl.pallas_call(
        _mm_kernel, grid=(m // tm, n // tn),
        in_specs=[pl.BlockSpec((tm, k), lambda i, j: (i, 0)), pl.BlockSpec((k, tn), lambda i, j: (0, j))],
        out_specs=pl.BlockSpec((tm, tn), lambda i, j: (i, j)),
        out_shape=jax.ShapeDtypeStruct((m, n), out_dtype),
        compiler_params=_params("parallel", "arbitrary"), name=name,
    )(a, w)


def matmul_rotary(a, w, tables, out_dtype, *, tm, tn, name):
    m, k = a.shape
    n = w.shape[1]
    tm, tn = min(tm, m), min(tn, n)
    tab = pl.BlockSpec((tm, HEAD_DIM), lambda i, j: (i, 0))
    return pl.pallas_call(
        _mm_rotary_kernel, grid=(m // tm, n // tn),
        in_specs=[pl.BlockSpec((tm, k), lambda i, j: (i, 0)), pl.BlockSpec((k, tn), lambda i, j: (0, j)),
                  tab, tab, tab],
        out_specs=pl.BlockSpec((tm, tn), lambda i, j: (i, j)),
        out_shape=jax.ShapeDtypeStruct((m, n), out_dtype),
        compiler_params=_params("parallel", "arbitrary"), name=name,
    )(a, w, *tables)


def matmul_swiglu(a, wg, wu, *, tm, tn, name):
    m, k = a.shape
    n = wg.shape[1]
    tm, tn = min(tm, m), min(tn, n)
    wspec = pl.BlockSpec((k, tn), lambda i, j: (0, j))
    return pl.pallas_call(
        _mm_swiglu_kernel, grid=(m // tm, n // tn),
        in_specs=[pl.BlockSpec((tm, k), lambda i, j: (i, 0)), wspec, wspec],
        out_specs=pl.BlockSpec((tm, tn), lambda i, j: (i, j)),
        out_shape=jax.ShapeDtypeStruct((m, n), BF16),
        compiler_params=_params("parallel", "arbitrary"), name=name,
    )(a, wg, wu)


def matmul_glu(z, w, b, *, tm, tn, name):
    m, k = z.shape
    n = w.shape[1]
    tm, tn = min(tm, m), min(tn, n)
    return pl.pallas_call(
        _mm_glu_kernel, grid=(m // tm, n // tn),
        in_specs=[pl.BlockSpec((tm, k), lambda i, j: (i, 0)), pl.BlockSpec((k, tn), lambda i, j: (0, j)),
                  pl.BlockSpec((tm, tn), lambda i, j: (i, j)), pl.BlockSpec((1, tn), lambda i, j: (0, j))],
        out_specs=pl.BlockSpec((tm, tn), lambda i, j: (i, j)),
        out_shape=jax.ShapeDtypeStruct((m, n), BF16),
        compiler_params=_params("parallel", "arbitrary"), name=name,
    )(z, w, z, b.reshape(1, n))


def matmul_to_token_major(a, w, bsz, *, tm, tn, name):
    m, k = a.shape
    n = w.shape[1]
    seq = m // bsz
    tm, tn = min(tm, seq), min(tn, n)
    nt, nj = seq // tm, n // tn
    return pl.pallas_call(
        _mm_kernel, grid=(bsz, nt, nj),
        in_specs=[pl.BlockSpec((tm, k), lambda b, i, j: (b * nt + i, 0)),
                  pl.BlockSpec((k, tn), lambda b, i, j: (0, j))],
        out_specs=pl.BlockSpec((tm, tn), lambda b, i, j: (i, b * nj + j)),
        out_shape=jax.ShapeDtypeStruct((seq, bsz * n), F32),
        compiler_params=_params("parallel", "parallel", "arbitrary"), name=name,
    )(a, w)


def matmul_from_token_major(a, w, bsz, out_dtype, *, tm, tn, name):
    seq = a.shape[0]
    k, n = w.shape
    tm, tn = min(tm, seq), min(tn, n)
    nt, nj = seq // tm, n // tn
    return pl.pallas_call(
        _mm_kernel, grid=(bsz, nt, nj),
        in_specs=[pl.BlockSpec((tm, k), lambda b, i, j: (i, b)),
                  pl.BlockSpec((k, tn), lambda b, i, j: (0, j))],
        out_specs=pl.BlockSpec((tm, tn), lambda b, i, j: (b * nt + i, j)),
        out_shape=jax.ShapeDtypeStruct((bsz * seq, n), out_dtype),
        compiler_params=_params("parallel", "parallel", "arbitrary"), name=name,
    )(a, w)


def _s5_discretize_kernel(are_ref, aim_ref, ls_ref, abr_ref, abi_ref, fr_ref, fi_ref):
    lr, li = are_ref[...], aim_ref[...]
    step = jnp.exp(ls_ref[...])
    decay = jnp.exp(lr * step)
    abar_re = decay * jnp.cos(li * step)
    abar_im = decay * jnp.sin(li * step)
    denom = lr * lr + li * li
    nr = abar_re - 1.0
    ni = abar_im
    abr_ref[...] = abar_re
    abi_ref[...] = abar_im
    fr_ref[...] = (nr * lr + ni * li) / denom
    fi_ref[...] = (ni * lr - nr * li) / denom


def s5_discretize(a_re, a_im, log_step):
    g, p = a_re.shape
    shp = jax.ShapeDtypeStruct((g, p), F32)
    return pl.pallas_call(_s5_discretize_kernel, out_shape=[shp] * 4, name="s5_discretize")(
        a_re, a_im, log_step.reshape(g, 1))


def _s5_scan_kernel(u_ref, bblk_ref, cblk_ref, ab_ref, d_ref, z_ref, x_scr, carry_scr, *, rows, nstate, bsz):
    @pl.when(pl.program_id(1) == 0)
    def _():
        carry_scr[...] = jnp.zeros_like(carry_scr)

    u = u_ref[...]
    x_scr[...] = jnp.dot(u.astype(BF16), bblk_ref[...], preferred_element_type=F32)
    lw = SSM_LANE_CHUNK
    second = lax.broadcasted_iota(I32, (SUBLANE, lw), 0) >= bsz
    for lt in range(nstate // lw):
        re = slice(lt * lw, (lt + 1) * lw)
        im = slice(nstate + lt * lw, nstate + (lt + 1) * lw)
        ar, ai = ab_ref[0:1, re], ab_ref[1:2, re]
        a2r, a2i = ar * ar - ai * ai, 2.0 * ar * ai
        amr, ami = jnp.where(second, ar, 0.0), jnp.where(second, ai, 0.0)
        pr, pi = jnp.where(second, a2r, ar), jnp.where(second, a2i, ai)

        def body(n, carry, re=re, im=im, amr=amr, ami=ami, pr=pr, pi=pi):
            cr, ci = carry
            r0 = pl.multiple_of(n * SUBLANE, SUBLANE)
            bur, bui = x_scr[pl.ds(r0, SUBLANE), re], x_scr[pl.ds(r0, SUBLANE), im]
            sr, si = pltpu.roll(bur, bsz, 0), pltpu.roll(bui, bsz, 0)
            wr = bur + amr * sr - ami * si
            wi = bui + amr * si + ami * sr
            cbr = jnp.where(second, cr, pltpu.roll(cr, bsz, 0))
            cbi = jnp.where(second, ci, pltpu.roll(ci, bsz, 0))
            xr = wr + pr * cbr - pi * cbi
            xi = wi + pr * cbi + pi * cbr
            x_scr[pl.ds(r0, SUBLANE), re] = xr
            x_scr[pl.ds(r0, SUBLANE), im] = xi
            return xr, xi

        cr, ci = lax.fori_loop(0, rows // SUBLANE, body, (carry_scr[:, re], carry_scr[:, im]), unroll=2)
        carry_scr[:, re] = cr
        carry_scr[:, im] = ci
    y = jnp.dot(x_scr[...].astype(BF16), cblk_ref[...], preferred_element_type=F32) + d_ref[...] * u
    z_ref[...] = jax.nn.gelu(y).astype(z_ref.dtype)


def s5_scan(u_tm, bblk, cblk, ab, d_skip, bsz):
    m, width = u_tm.shape
    assert 2 * bsz == SUBLANE, "the scan tiles two tokens of every batch element into one sublane tile"
    packs, pack_ch, two_n = bblk.shape
    nstate = two_n // 2
    rows = SSM_CHUNK_TOKENS * bsz
    return pl.pallas_call(
        functools.partial(_s5_scan_kernel, rows=rows, nstate=nstate, bsz=bsz),
        grid=(packs, m // rows),
        in_specs=[pl.BlockSpec((rows, pack_ch), lambda p, c: (c, p)),
                  pl.BlockSpec((None, pack_ch, two_n), lambda p, c: (p, 0, 0)),
                  pl.BlockSpec((None, two_n, pack_ch), lambda p, c: (p, 0, 0)),
                  pl.BlockSpec((None, 2, nstate), lambda p, c: (p, 0, 0)),
                  pl.BlockSpec((None, 1, pack_ch), lambda p, c: (p, 0, 0))],
        out_specs=pl.BlockSpec((rows, pack_ch), lambda p, c: (c, p)),
        out_shape=jax.ShapeDtypeStruct((m, width), BF16),
        scratch_shapes=[pltpu.VMEM((rows, two_n), F32), pltpu.VMEM((SUBLANE, two_n), F32)],
        compiler_params=_params("parallel", "arbitrary"), name="s5_scan",
    )(u_tm, bblk, cblk, ab, d_skip.reshape(packs, 1, pack_ch))


def s5_operators(a_re, a_im, log_step, b_re, b_im, c_re, c_im):
    g, p = a_re.shape
    gsz = b_re.shape[-1]
    gpp = SSM_PACK_CH // gsz
    packs = g // gpp
    abr, abi, fr, fi = s5_discretize(a_re, a_im, log_step)
    bbar_re = fr[..., None] * b_re - fi[..., None] * b_im
    bbar_im = fr[..., None] * b_im + fi[..., None] * b_re
    eye = jnp.eye(gpp, dtype=F32)

    def b_block(t):
        t = t.reshape(packs, gpp, p, gsz).transpose(0, 1, 3, 2)
        return (t[:, :, :, None, :] * eye[None, :, None, :, None]).reshape(packs, gpp * gsz, gpp * p)

    def c_block(t):
        t = t.reshape(packs, gpp, gsz, p).transpose(0, 1, 3, 2)
        return (t[:, :, :, None, :] * eye[None, :, None, :, None]).reshape(packs, gpp * p, gpp * gsz)

    bblk = jnp.concatenate([b_block(bbar_re), b_block(bbar_im)], axis=2).astype(BF16)
    cblk = jnp.concatenate([c_block(c_re), -c_block(c_im)], axis=1).astype(BF16)
    ab = jnp.stack([abr.reshape(packs, gpp * p), abi.reshape(packs, gpp * p)], axis=1)
    return bblk, cblk, ab


def _rope_table_kernel(pos_ref, invf_ref, cos_ref, sin_lo_ref, sin_hi_ref, *, scale):
    ang = pos_ref[...].astype(F32) * invf_ref[...]
    lane = lax.broadcasted_iota(I32, ang.shape, 1)
    c, s = jnp.cos(ang), jnp.sin(ang)
    cos_ref[...] = jnp.where(lane < ROT_DIM, c, 1.0) * scale
    sin_lo_ref[...] = jnp.where(lane < ROT_HALF, -s, 0.0) * scale
    sin_hi_ref[...] = jnp.where((lane >= ROT_HALF) & (lane < ROT_DIM), s, 0.0) * scale


def rope_tables(positions, scale):
    t = positions.size
    inv_freq = ROPE_THETA ** (-jnp.arange(ROT_HALF, dtype=F32) * 2.0 / ROT_DIM)
    invf = jnp.zeros((1, HEAD_DIM), F32).at[0, :ROT_DIM].set(jnp.concatenate([inv_freq, inv_freq]))
    ts = 1024
    shp = jax.ShapeDtypeStruct((t, HEAD_DIM), F32)
    spec = pl.BlockSpec((ts, HEAD_DIM), lambda i: (i, 0))
    return pl.pallas_call(
        functools.partial(_rope_table_kernel, scale=scale), grid=(t // ts,),
        in_specs=[pl.BlockSpec((ts, 1), lambda i: (i, 0)), pl.BlockSpec((1, HEAD_DIM), lambda i: (0, 0))],
        out_specs=[spec] * 3, out_shape=[shp] * 3,
        compiler_params=_params("parallel"), name="rope_tables",
    )(positions.reshape(t, 1), invf)


def _attn_kernel(q_ref, kp_ref, kc_ref, vp_ref, vc_ref, o_ref, lse_ref, *, n_kv, rep, steps):
    blk = ATTN_BLOCK
    n = pl.program_id(2)
    qi = lax.broadcasted_iota(I32, (blk, 2 * blk), 0)
    kj = lax.broadcasted_iota(I32, (blk, 2 * blk), 1)
    dist = qi + blk - kj
    valid = (dist >= 0) & (dist <= steps) & ((n > 0) | (kj >= blk))
    lane = lax.broadcasted_iota(I32, (blk, LANE), 1)
    lse_tile = jnp.zeros((blk, LANE), F32)
    for h in range(n_kv):
        hs = slice(h * HEAD_DIM, (h + 1) * HEAD_DIM)
        kk = jnp.concatenate([kp_ref[:, hs], kc_ref[:, hs]], axis=0)
        vv = jnp.concatenate([vp_ref[:, hs], vc_ref[:, hs]], axis=0)
        for r in range(rep):
            head = h * rep + r
            cols = slice(head * HEAD_DIM, (head + 1) * HEAD_DIM)
            s = lax.dot_general(q_ref[:, cols], kk, (((1,), (1,)), ((), ())), preferred_element_type=F32)
            s = jnp.where(valid, s, NEG_MASK)
            mx = jnp.max(s, axis=-1, keepdims=True)
            p = jnp.exp(s - mx)
            den = jnp.sum(p, axis=-1, keepdims=True)
            o = jnp.dot((p / den).astype(BF16), vv, preferred_element_type=F32)
            o_ref[:, cols] = o.astype(o_ref.dtype)
            lse_tile = jnp.where(lane == head, mx + jnp.log(den), lse_tile)
    lse_ref[...] = lse_tile


def dilated_attention(q, k, v, group, steps, dilation, n_dil):
    bsz, seq, qd = q.shape
    d = qd // n_dil
    kvd = k.shape[2]
    n_kv = kvd // HEAD_DIM
    rep = (d // HEAD_DIM) // n_kv
    m_len = seq // dilation
    blk = ATTN_BLOCK
    assert m_len % blk == 0
    qv = q.reshape(bsz, m_len, dilation * qd)
    kv_ = k.reshape(bsz, m_len, dilation * kvd)
    vv_ = v.reshape(bsz, m_len, dilation * kvd)
    prev = lambda b, r, n: (b, jnp.maximum(n - 1, 0), r)
    cur = lambda b, r, n: (b, n, r)
    o, lse = pl.pallas_call(
        functools.partial(_attn_kernel, n_kv=n_kv, rep=rep, steps=steps),
        grid=(bsz, dilation, m_len // blk),
        in_specs=[pl.BlockSpec((None, blk, d), lambda b, r, n: (b, n, r * n_dil + group)),
                  pl.BlockSpec((None, blk, kvd), prev), pl.BlockSpec((None, blk, kvd), cur),
                  pl.BlockSpec((None, blk, kvd), prev), pl.BlockSpec((None, blk, kvd), cur)],
        out_specs=[pl.BlockSpec((None, blk, d), cur), pl.BlockSpec((None, blk, LANE), cur)],
        out_shape=[jax.ShapeDtypeStruct((bsz, m_len, dilation * d), BF16),
                   jax.ShapeDtypeStruct((bsz, m_len, dilation * LANE), F32)],
        compiler_params=_params("parallel", "parallel", "arbitrary"), name=f"dilated_attention_{dilation}",
    )(qv, kv_, kv_, vv_, vv_)
    return o.reshape(bsz * seq, d), lse.reshape(bsz * seq, LANE)


def _attn_merge_kernel(*refs, n_dil, n_heads):
    o_refs, l_refs, out_ref = refs[:n_dil], refs[n_dil:2 * n_dil], refs[2 * n_dil]
    ls = [r[...] for r in l_refs]
    mx = functools.reduce(jnp.maximum, ls)
    es = [jnp.exp(l - mx) for l in ls]
    tot = functools.reduce(jnp.add, es)
    ws = [e / tot for e in es]
    for h in range(n_heads):
        cols = slice(h * HEAD_DIM, (h + 1) * HEAD_DIM)
        acc = o_refs[0][:, cols].astype(F32) * ws[0][:, h:h + 1]
        for g in range(1, n_dil):
            acc = acc + o_refs[g][:, cols].astype(F32) * ws[g][:, h:h + 1]
        out_ref[:, cols] = acc.astype(out_ref.dtype)


def attention_merge(outs, lses):
    t, d = outs[0].shape
    n_dil = len(outs)
    ts = ROW_TILE
    return pl.pallas_call(
        functools.partial(_attn_merge_kernel, n_dil=n_dil, n_heads=d // HEAD_DIM), grid=(t // ts,),
        in_specs=[pl.BlockSpec((ts, d), lambda i: (i, 0))] * n_dil + [pl.BlockSpec((ts, LANE), lambda i: (i, 0))] * n_dil,
        out_specs=pl.BlockSpec((ts, d), lambda i: (i, 0)),
        out_shape=jax.ShapeDtypeStruct((t, d), BF16),
        compiler_params=_params("parallel"), name="attention_merge",
    )(*outs, *lses)


def _router_kernel(h_ref, wr_ref, bias_ref, ek_ref, pk_ref, gk_ref, cnt_ref, run_scr, *, n_exp):
    tm = h_ref.shape[0]
    per = n_exp // N_EXPERT_GROUPS
    ng = N_EXPERT_GROUPS

    @pl.when(pl.program_id(0) == 0)
    def _():
        run_scr[...] = jnp.zeros_like(run_scr)

    logits = lax.dot_general(wr_ref[...], h_ref[...], (((1,), (1,)), ((), ())), preferred_element_type=F32)
    scores = jax.nn.sigmoid(logits)
    biased = scores + bias_ref[...]
    sc = [scores[w * ng:(w + 1) * ng] for w in range(per)]
    bi = [biased[w * ng:(w + 1) * ng] for w in range(per)]
    assert per == 4
    hi01, lo01 = jnp.maximum(bi[0], bi[1]), jnp.minimum(bi[0], bi[1])
    hi23, lo23 = jnp.maximum(bi[2], bi[3]), jnp.minimum(bi[2], bi[3])
    group_score = jnp.maximum(hi01, hi23) + jnp.maximum(jnp.minimum(hi01, hi23), jnp.maximum(lo01, lo23))
    gidx = lax.broadcasted_iota(I32, (ng, tm), 0)
    grank = jnp.zeros((ng, tm), I32)
    for g2 in range(ng):
        row = group_score[g2:g2 + 1]
        grank += ((row > group_score) | ((row == group_score) & (g2 < gidx))).astype(I32)
    gsel = grank < TOPK_GROUPS
    cand = [jnp.where(gsel, b, -jnp.inf) for b in bi]
    eid = [gidx * per + w for w in range(per)]
    erank = [jnp.zeros((ng, tm), I32) for _ in range(per)]
    for w2 in range(per):
        for g2 in range(ng):
            row = cand[w2][g2:g2 + 1]
            e2 = g2 * per + w2
            for w in range(per):
                erank[w] += ((row > cand[w]) | ((row == cand[w]) & (e2 < eid[w]))).astype(I32)
    sel = [r < TOP_K for r in erank]
    ssum = functools.reduce(jnp.add, [jnp.sum(jnp.where(m, s, 0.0), axis=0, keepdims=True) for m, s in zip(sel, sc)])
    gate = [jnp.where(m, s / ssum * ROUTED_SCALE, 0.0) for m, s in zip(sel, sc)]
    sel_all = jnp.concatenate([m.astype(F32) for m in sel], axis=0)
    earlier = (lax.broadcasted_iota(I32, (tm, tm), 0) < lax.broadcasted_iota(I32, (tm, tm), 1)).astype(BF16)
    pos = jnp.dot(sel_all.astype(BF16), earlier, preferred_element_type=F32) + run_scr[...]
    run_scr[...] += jnp.sum(sel_all, axis=1, keepdims=True)
    cnt_ref[...] = run_scr[...].astype(I32)
    above = (lax.broadcasted_iota(I32, (n_exp, n_exp), 1) < lax.broadcasted_iota(I32, (n_exp, n_exp), 0)).astype(BF16)
    kidx = jnp.dot(above, sel_all.astype(BF16), preferred_element_type=F32)
    eid_all = jnp.concatenate(eid, axis=0).astype(F32)
    gate_all = jnp.concatenate(gate, axis=0)
    for k in range(TOP_K):
        mk = (sel_all > 0.0) & (kidx == float(k))
        ek_ref[k:k + 1, :] = jnp.sum(jnp.where(mk, eid_all, 0.0), axis=0, keepdims=True).astype(I32)
        pk_ref[k:k + 1, :] = jnp.sum(jnp.where(mk, pos, 0.0), axis=0, keepdims=True).astype(I32)
        gk_ref[k:k + 1, :] = jnp.sum(jnp.where(mk, gate_all, 0.0), axis=0, keepdims=True)


def moe_route(h, router_w, router_bias):
    t, d = h.shape
    n_exp = router_w.shape[1]
    per = n_exp // N_EXPERT_GROUPS
    wr = router_w.T.reshape(N_EXPERT_GROUPS, per, d).transpose(1, 0, 2).reshape(n_exp, d).astype(BF16)
    bias = router_bias.astype(F32).reshape(N_EXPERT_GROUPS, per).T.reshape(n_exp, 1)
    tm = ROUTER_TILE
    kspec = pl.BlockSpec((TOP_K, tm), lambda i: (0, i))
    ek, pk, gk, cnt = pl.pallas_call(
        functools.partial(_router_kernel, n_exp=n_exp), grid=(t // tm,),
        in_specs=[pl.BlockSpec((tm, d), lambda i: (i, 0)), pl.BlockSpec((n_exp, d), lambda i: (0, 0)),
                  pl.BlockSpec((n_exp, 1), lambda i: (0, 0))],
        out_specs=[kspec, kspec, kspec, pl.BlockSpec((n_exp, 1), lambda i: (0, 0))],
        out_shape=[jax.ShapeDtypeStruct((TOP_K, t), I32), jax.ShapeDtypeStruct((TOP_K, t), I32),
                   jax.ShapeDtypeStruct((TOP_K, t), F32), jax.ShapeDtypeStruct((n_exp, 1), I32)],
        scratch_shapes=[pltpu.VMEM((n_exp, 1), F32)],
        compiler_params=_params("arbitrary"), name="moe_router",
    )(h, wr, bias)
    counts = cnt.reshape(per, N_EXPERT_GROUPS).T.reshape(n_exp)
    return ek, pk, gk, counts


def _row_copy(src_ref, src_row, dst_ref, dst_row, sem):
    return pltpu.make_async_copy(src_ref.at[pl.ds(src_row, 1)], dst_ref.at[pl.ds(dst_row, 1)], sem)


def _dispatch_kernel(slot_ref, h_ref, xs_in_ref, xs_ref, sem, *, tm):
    del xs_in_ref
    base = pl.program_id(0) * (TOP_K * tm)
    for k in range(TOP_K):
        def issue(r, _, k=k):
            _row_copy(h_ref, r, xs_ref, slot_ref[base + k * tm + r], sem).start()
            return 0
        lax.fori_loop(0, tm, issue, 0)
    def drain(j, _):
        _row_copy(h_ref, 0, xs_ref, 0, sem).wait()
        return 0
    lax.fori_loop(0, TOP_K * tm, drain, 0)


def moe_dispatch(h32, slots, n_slots):
    t, d = h32.shape
    tm = ROW_TILE
    xs0 = jnp.zeros((n_slots, d), F32)
    return pl.pallas_call(
        functools.partial(_dispatch_kernel, tm=tm),
        grid_spec=pltpu.PrefetchScalarGridSpec(
            num_scalar_prefetch=1, grid=(t // tm,),
            in_specs=[pl.BlockSpec((tm, d), lambda i, s: (i, 0)), pl.BlockSpec(memory_space=pl.ANY)],
            out_specs=pl.BlockSpec(memory_space=pl.ANY),
            scratch_shapes=[pltpu.SemaphoreType.DMA(())]),
        out_shape=jax.ShapeDtypeStruct((n_slots, d), F32),
        input_output_aliases={2: 0},
        compiler_params=_params("arbitrary"), name="moe_dispatch",
    )(slots, h32, xs0)


def _expert_kernel(te_ref, nused_ref, xs_ref, wg_ref, wu_ref, wd_ref, ys_ref):
    i = pl.program_id(0)

    @pl.when(i < nused_ref[0])
    def _():
        x = xs_ref[...].astype(BF16)
        g = jnp.dot(x, wg_ref[...], preferred_element_type=F32)
        u = jnp.dot(x, wu_ref[...], preferred_element_type=F32)
        a = (jax.nn.silu(g) * u).astype(BF16)
        ys_ref[...] = jnp.dot(a, wd_ref[...], preferred_element_type=F32)

    @pl.when(i >= nused_ref[0])
    def _():
        ys_ref[...] = jnp.zeros_like(ys_ref)


def moe_experts(xs, tile_expert, n_used, wg, wu, wd):
    n_slots, d = xs.shape
    de = wg.shape[2]
    te = EXPERT_TILE
    row = lambda i, e, n: (jnp.minimum(i, n[0] - 1), 0)
    return pl.pallas_call(
        _expert_kernel,
        grid_spec=pltpu.PrefetchScalarGridSpec(
            num_scalar_prefetch=2, grid=(n_slots // te,),
            in_specs=[pl.BlockSpec((te, d), row),
                      pl.BlockSpec((None, d, de), lambda i, e, n: (e[i], 0, 0)),
                      pl.BlockSpec((None, d, de), lambda i, e, n: (e[i], 0, 0)),
                      pl.BlockSpec((None, de, d), lambda i, e, n: (e[i], 0, 0))],
            out_specs=pl.BlockSpec((te, d), lambda i, e, n: (i, 0))),
        out_shape=jax.ShapeDtypeStruct((n_slots, d), F32),
        compiler_params=_params("arbitrary"), name="moe_experts",
    )(tile_expert, n_used, xs, wg, wu, wd)


def _combine_kernel(slot_ref, gk_ref, sh_ref, x_ref, gate_ref, ng_ref, ys_ref, o_ref, buf, sem, *, tm):
    base = pl.program_id(0) * (TOP_K * tm)
    for k in range(TOP_K):
        def issue(r, _, k=k):
            _row_copy(ys_ref, slot_ref[base + k * tm + r], buf.at[k], r, sem).start()
            return 0
        lax.fori_loop(0, tm, issue, 0)
    def drain(j, _):
        _row_copy(ys_ref, 0, buf.at[0], 0, sem).wait()
        return 0
    lax.fori_loop(0, TOP_K * tm, drain, 0)
    y = sh_ref[...].astype(F32)
    gk = gk_ref[...]
    for k in range(TOP_K):
        y = y + gk[:, k:k + 1] * buf[k]
    yn = y * lax.rsqrt(jnp.mean(y * y, axis=-1, keepdims=True) + NORM_EPS) * ng_ref[...]
    o_ref[...] = x_ref[...] + gate_ref[...] * yn


def moe_combine(x, ys, slots, gk_t, shared, norm_g, gate):
    bsz, seq, d = x.shape
    tm = ROW_TILE
    nt = seq // tm
    return pl.pallas_call(
        functools.partial(_combine_kernel, tm=tm),
        grid_spec=pltpu.PrefetchScalarGridSpec(
            num_scalar_prefetch=1, grid=(bsz * nt,),
            in_specs=[pl.BlockSpec((tm, TOP_K), lambda i, s: (i, 0)),
                      pl.BlockSpec((tm, d), lambda i, s: (i, 0)),
                      pl.BlockSpec((None, tm, d), lambda i, s: (i // nt, i % nt, 0)),
                      pl.BlockSpec((None, 1, d), lambda i, s: (i // nt, 0, 0)),
                      pl.BlockSpec((1, d), lambda i, s: (0, 0)),
                      pl.BlockSpec(memory_space=pl.ANY)],
            out_specs=pl.BlockSpec((None, tm, d), lambda i, s: (i // nt, i % nt, 0)),
            scratch_shapes=[pltpu.VMEM((TOP_K, tm, d), F32), pltpu.SemaphoreType.DMA(())]),
        out_shape=jax.ShapeDtypeStruct(x.shape, F32),
        compiler_params=_params("arbitrary"), name="moe_combine",
    )(slots, gk_t, shared, x, gate.reshape(bsz, 1, d), norm_g.reshape(1, d), ys)


def moe_layer(x, h, h32, norm_g, gate, router_w, router_bias, e_gate, e_up, e_down, s_gate, s_up, s_down):
    t, d = h.shape
    n_exp = router_w.shape[1]
    te = EXPERT_TILE
    ek, pk, gk, counts = moe_route(h, router_w, router_bias)
    tiles = (counts + te - 1) // te
    tile_end = jnp.cumsum(tiles)
    offsets = (tile_end - tiles) * te
    slot = offsets[ek] + pk
    tm = ROW_TILE
    slots = slot.reshape(TOP_K, t // tm, tm).transpose(1, 0, 2).reshape(-1).astype(I32)
    n_tiles = (t * TOP_K) // te + n_exp
    n_used = tile_end[-1].astype(I32)
    tile_expert = jnp.searchsorted(tile_end, jnp.arange(n_tiles, dtype=tile_end.dtype), side="right")
    last = jnp.searchsorted(tile_end, n_used - 1, side="right")
    tile_expert = jnp.minimum(tile_expert, last).astype(I32)
    xs = moe_dispatch(h32, slots, n_tiles * te)
    ys = moe_experts(xs, tile_expert, n_used.reshape(1), e_gate.astype(BF16), e_up.astype(BF16), e_down.astype(BF16))
    act = matmul_swiglu(h, s_gate.astype(BF16), s_up.astype(BF16), tm=1024, tn=512, name="shared_up")
    shared = matmul(act, s_down.astype(BF16), BF16, tm=1024, tn=1024, name="shared_down")
    return moe_combine(x, ys, slots, gk.T, shared, norm_g, gate)


def kernel(x, c, positions, ada_w, ada_b, norm_g, ssm_w_in, ssm_a_re, ssm_a_im, ssm_log_step, ssm_b_re, ssm_b_im, ssm_c_re, ssm_c_im, ssm_d, glu_w, glu_b, ssm_w_out, kv_norm_g, w_k, w_v, attn_w_q, attn_w_o, router_w, router_bias, exp_w_gate, exp_w_up, exp_w_down, sh_w_gate, sh_w_up, sh_w_down):
    bsz, seq, d = x.shape
    depth = ada_w.shape[0]
    n_a = ssm_w_in.shape[0]
    t = bsz * seq
    mod = ada_modulation(c, ada_w, ada_b)
    k = v = None
    for layer in range(depth):
        sh1, sc1, g1, sh2, sc2, g2 = jnp.split(mod[layer], N_MOD, axis=-1)
        if layer < n_a:
            a = layer
            (h,) = norm_modulate(x, norm_g[layer, 0], sc1, sh1)
            u_tm = matmul_to_token_major(h, ssm_w_in[a].astype(BF16), bsz, tm=1024, tn=512, name="ssm_in")
            bblk, cblk, ab = s5_operators(ssm_a_re[a], ssm_a_im[a], ssm_log_step[a], ssm_b_re[a], ssm_b_im[a],
                                          ssm_c_re[a], ssm_c_im[a])
            z = s5_scan(u_tm.reshape(seq * bsz, -1), bblk, cblk, ab, ssm_d[a], bsz)
            zz = matmul_glu(z, glu_w[a].astype(BF16), glu_b[a], tm=1024, tn=512, name="ssm_glu")
            y = matmul_from_token_major(zz.reshape(seq, -1), ssm_w_out[a].astype(BF16), bsz, BF16,
                                        tm=1024, tn=512, name="ssm_out")
        else:
            bl = layer - n_a
            if layer == n_a:
                h, hkv = norm_modulate(x, norm_g[layer, 0], sc1, sh1, kv_g=kv_norm_g)
                k_tabs = rope_tables(positions, 1.0)
                k = matmul_rotary(hkv, w_k.astype(BF16), k_tabs, BF16, tm=1024, tn=512, name="k_proj")
                v = matmul(hkv, w_v.astype(BF16), BF16, tm=1024, tn=512, name="v_proj")
                k = k.reshape(bsz, seq, -1)
                v = v.reshape(bsz, seq, -1)
            else:
                (h,) = norm_modulate(x, norm_g[layer, 0], sc1, sh1)
            n_dil = attn_w_q.shape[2] // d
            q_tabs = rope_tables(positions, HEAD_DIM ** -0.5)
            q = matmul_rotary(h, attn_w_q[bl].astype(BF16), q_tabs, BF16, tm=1024, tn=512, name="q_proj")
            q = q.reshape(bsz, seq, -1)
            outs, lses = [], []
            for grp, (window, dil) in enumerate(DILATION_GROUPS[:n_dil]):
                o_g, l_g = dilated_attention(q, k, v, grp, window // dil, dil, n_dil)
                outs.append(o_g)
                lses.append(l_g)
            o = attention_merge(outs, lses)
            y = matmul(o, attn_w_o[bl].astype(BF16), BF16, tm=1024, tn=512, name="attn_out")
        x = residual_norm(x, y, norm_g[layer, 1], g1)
        h, h32 = norm_modulate(x, norm_g[layer, 2], sc2, sh2, with_f32=True)
        x = moe_layer(x, h, h32, norm_g[layer, 3], g2, router_w[layer], router_bias[layer],
                      exp_w_gate[layer], exp_w_up[layer], exp_w_down[layer],
                      sh_w_gate[layer], sh_w_up[layer], sh_w_down[layer])
    return x
```

```python
import functools
import math

import jax
import jax.numpy as jnp
from jax import lax
from jax.experimental import pallas as pl
from jax.experimental.pallas import tpu as pltpu

F32 = jnp.float32
BF16 = jnp.bfloat16
I32 = jnp.int32

LANE = 128
SUBLANE = 8
VMEM_LIMIT_BYTES = 56 * 1024 * 1024
MXU_DIM = 256

N_MOD = 6
NORM_EPS = 1e-6
HEAD_DIM = 128
ROT_DIM = HEAD_DIM // 4
ROT_HALF = ROT_DIM // 2
ROPE_THETA = 500000.0
DILATION_GROUPS = ((128, 1), (512, 4), (2048, 16))
ATTN_BLOCK = 128
ATTN_SPAN = ATTN_BLOCK * max(dil for _, dil in DILATION_GROUPS)
N_EXPERT_GROUPS = 8
TOPK_GROUPS = 4
TOP_K = 4
ROUTED_SCALE = 2.5
NEG_MASK = -1e30

SSM_PACK_CH = MXU_DIM
SSM_CHUNK_TOKENS = 256
SSM_LANE_CHUNK = 512
EXPERT_TILE = 256
ROW_TILE = 256
COMBINE_TILE = 128
ROUTER_TILE = 512


def _params(*sem):
    return pltpu.CompilerParams(dimension_semantics=sem, vmem_limit_bytes=VMEM_LIMIT_BYTES)


def _ada_kernel(c_ref, w_ref, b_ref, o_ref):
    cond = jax.nn.silu(c_ref[...])
    o_ref[...] = jnp.dot(cond, w_ref[...], precision=lax.Precision.HIGHEST,
                         preferred_element_type=F32) + b_ref[...]


def ada_modulation(c, ada_w, ada_b):
    depth, d, n = ada_w.shape
    bsz = c.shape[0]
    rows = -(-bsz // SUBLANE) * SUBLANE
    cp = jnp.zeros((rows, d), F32).at[:bsz].set(c)
    tn = 512
    out = pl.pallas_call(
        _ada_kernel,
        grid=(depth, n // tn),
        in_specs=[pl.BlockSpec((rows, d), lambda l, j: (0, 0)),
                  pl.BlockSpec((None, d, tn), lambda l, j: (l, 0, j)),
                  pl.BlockSpec((None, 1, tn), lambda l, j: (l, 0, j))],
        out_specs=pl.BlockSpec((None, rows, tn), lambda l, j: (l, 0, j)),
        out_shape=jax.ShapeDtypeStruct((depth, rows, n), F32),
        compiler_params=_params("parallel", "parallel"),
        name="ada_modulation",
    )(cp, ada_w, ada_b.reshape(depth, 1, n))
    return out[:, :bsz]


def _normmod_kernel(x_ref, g_ref, sc_ref, sh_ref, *rest, with_kv, with_f32):
    x = x_ref[...]
    xn = x * lax.rsqrt(jnp.mean(x * x, axis=-1, keepdims=True) + NORM_EPS)
    h = xn * g_ref[...] * (1.0 + sc_ref[...]) + sh_ref[...]
    rest = list(rest)
    if with_kv:
        kvg_ref = rest.pop(0)
    h_ref = rest.pop(0)
    h_ref[...] = h.astype(h_ref.dtype)
    if with_kv:
        rest.pop(0)[...] = (xn * kvg_ref[...]).astype(BF16)
    if with_f32:
        rest.pop(0)[...] = h


def norm_modulate(x, g, sc, sh, kv_g=None, with_f32=False):
    bsz, seq, d = x.shape
    ts = ROW_TILE
    nt = seq // ts
    row = lambda b, i: (b * nt + i, 0)
    vec = pl.BlockSpec((1, d), lambda b, i: (0, 0))
    bvec = pl.BlockSpec((None, 1, d), lambda b, i: (b, 0, 0))
    in_specs = [pl.BlockSpec((None, ts, d), lambda b, i: (b, i, 0)), vec, bvec, bvec]
    args = [x, g.reshape(1, d), sc.reshape(bsz, 1, d), sh.reshape(bsz, 1, d)]
    out_specs = [pl.BlockSpec((ts, d), row)]
    out_shape = [jax.ShapeDtypeStruct((bsz * seq, d), BF16)]
    if kv_g is not None:
        in_specs.append(vec)
        args.append(kv_g.reshape(1, d))
        out_specs.append(pl.BlockSpec((ts, d), row))
        out_shape.append(jax.ShapeDtypeStruct((bsz * seq, d), BF16))
    if with_f32:
        out_specs.append(pl.BlockSpec((ts, d), row))
        out_shape.append(jax.ShapeDtypeStruct((bsz * seq, d), F32))
    return pl.pallas_call(
        functools.partial(_normmod_kernel, with_kv=kv_g is not None, with_f32=with_f32),
        grid=(bsz, nt), in_specs=in_specs, out_specs=out_specs, out_shape=out_shape,
        compiler_params=_params("parallel", "parallel"), name="norm_modulate",
    )(*args)


def _resid_norm_kernel(x_ref, y_ref, g_ref, gate_ref, o_ref):
    y = y_ref[...].astype(F32)
    yn = y * lax.rsqrt(jnp.mean(y * y, axis=-1, keepdims=True) + NORM_EPS) * g_ref[...]
    o_ref[...] = x_ref[...] + gate_ref[...] * yn


def residual_norm(x, y, g, gate):
    bsz, seq, d = x.shape
    ts = ROW_TILE
    nt = seq // ts
    return pl.pallas_call(
        _resid_norm_kernel,
        grid=(bsz, nt),
        in_specs=[pl.BlockSpec((None, ts, d), lambda b, i: (b, i, 0)),
                  pl.BlockSpec((ts, d), lambda b, i: (b * nt + i, 0)),
                  pl.BlockSpec((1, d), lambda b, i: (0, 0)),
                  pl.BlockSpec((None, 1, d), lambda b, i: (b, 0, 0))],
        out_specs=pl.BlockSpec((None, ts, d), lambda b, i: (b, i, 0)),
        out_shape=jax.ShapeDtypeStruct(x.shape, F32),
        compiler_params=_params("parallel", "parallel"), name="residual_norm",
    )(x, y, g.reshape(1, d), gate.reshape(bsz, 1, d))


def _mm_kernel(a_ref, w_ref, o_ref):
    o_ref[...] = jnp.dot(a_ref[...], w_ref[...], preferred_element_type=F32).astype(o_ref.dtype)


def _mm_glu_kernel(a_ref, w_ref, z_ref, b_ref, o_ref):
    acc = jnp.dot(a_ref[...], w_ref[...], preferred_element_type=F32) + b_ref[...]
    o_ref[...] = (z_ref[...].astype(F32) * jax.nn.sigmoid(acc)).astype(o_ref.dtype)


def _mm_swiglu_kernel(a_ref, wg_ref, wu_ref, o_ref):
    a = a_ref[...]
    g = jnp.dot(a, wg_ref[...], preferred_element_type=F32)
    u = jnp.dot(a, wu_ref[...], preferred_element_type=F32)
    o_ref[...] = (jax.nn.silu(g) * u).astype(o_ref.dtype)


def matmul(a, w, out_dtype, *, tm, tn, name):
    m, k = a.shape
    n = w.shape[1]
    tm, tn = min(tm, m), min(tn, n)
    return pl.pallas_call(
        _mm_kernel, grid=(m // tm, n // tn),
        in_specs=[pl.BlockSpec((tm, k), lambda i, j: (i, 0)), pl.BlockSpec((k, tn), lambda i, j: (0, j))],
        out_specs=pl.BlockSpec((tm, tn), lambda i, j: (i, j)),
        out_shape=jax.ShapeDtypeStruct((m, n), out_dtype),
        compiler_params=_params("parallel", "arbitrary"), name=name,
    )(a, w)


def matmul_swiglu(a, wg, wu, *, tm, tn, name):
    m, k = a.shape
    n = wg.shape[1]
    tm, tn = min(tm, m), min(tn, n)
    wspec = pl.BlockSpec((k, tn), lambda i, j: (0, j))
    return pl.pallas_call(
        _mm_swiglu_kernel, grid=(m // tm, n // tn),
        in_specs=[pl.BlockSpec((tm, k), lambda i, j: (i, 0)), wspec, wspec],
        out_specs=pl.BlockSpec((tm, tn), lambda i, j: (i, j)),
        out_shape=jax.ShapeDtypeStruct((m, n), BF16),
        compiler_params=_params("parallel", "arbitrary"), name=name,
    )(a, wg, wu)


def matmul_glu(z, w, b, *, tm, tn, name):
    m, k = z.shape
    n = w.shape[1]
    tm, tn = min(tm, m), min(tn, n)
    return pl.pallas_call(
        _mm_glu_kernel, grid=(m // tm, n // tn),
        in_specs=[pl.BlockSpec((tm, k), lambda i, j: (i, 0)), pl.BlockSpec((k, tn), lambda i, j: (0, j)),
                  pl.BlockSpec((tm, tn), lambda i, j: (i, j)), pl.BlockSpec((1, tn), lambda i, j: (0, j))],
        out_specs=pl.BlockSpec((tm, tn), lambda i, j: (i, j)),
        out_shape=jax.ShapeDtypeStruct((m, n), BF16),
        compiler_params=_params("parallel", "arbitrary"), name=name,
    )(z, w, z, b.reshape(1, n))


def matmul_to_token_major(a, w, bsz, *, tm, tn, name):
    m, k = a.shape
    n = w.shape[1]
    seq = m // bsz
    tm, tn = min(tm, seq), min(tn, n)
    nt, nj = seq // tm, n // tn
    return pl.pallas_call(
        _mm_kernel, grid=(bsz, nt, nj),
        in_specs=[pl.BlockSpec((tm, k), lambda b, i, j: (b * nt + i, 0)),
                  pl.BlockSpec((k, tn), lambda b, i, j: (0, j))],
        out_specs=pl.BlockSpec((tm, tn), lambda b, i, j: (i, b * nj + j)),
        out_shape=jax.ShapeDtypeStruct((seq, bsz * n), F32),
        compiler_params=_params("parallel", "parallel", "arbitrary"), name=name,
    )(a, w)


def matmul_from_token_major(a, w, bsz, out_dtype, *, tm, tn, name):
    seq = a.shape[0]
    k, n = w.shape
    tm, tn = min(tm, seq), min(tn, n)
    nt, nj = seq // tm, n // tn
    return pl.pallas_call(
        _mm_kernel, grid=(bsz, nt, nj),
        in_specs=[pl.BlockSpec((tm, k), lambda b, i, j: (i, b)),
                  pl.BlockSpec((k, tn), lambda b, i, j: (0, j))],
        out_specs=pl.BlockSpec((tm, tn), lambda b, i, j: (b * nt + i, j)),
        out_shape=jax.ShapeDtypeStruct((bsz * seq, n), out_dtype),
        compiler_params=_params("parallel", "parallel", "arbitrary"), name=name,
    )(a, w)


def _s5_discretize_kernel(are_ref, aim_ref, ls_ref, abr_ref, abi_ref, fr_ref, fi_ref):
    lr, li = are_ref[...], aim_ref[...]
    step = jnp.exp(ls_ref[...])
    decay = jnp.exp(lr * step)
    abar_re = decay * jnp.cos(li * step)
    abar_im = decay * jnp.sin(li * step)
    denom = lr * lr + li * li
    nr = abar_re - 1.0
    ni = abar_im
    abr_ref[...] = abar_re
    abi_ref[...] = abar_im
    fr_ref[...] = (nr * lr + ni * li) / denom
    fi_ref[...] = (ni * lr - nr * li) / denom


def s5_discretize(a_re, a_im, log_step):
    g, p = a_re.shape
    shp = jax.ShapeDtypeStruct((g, p), F32)
    return pl.pallas_call(_s5_discretize_kernel, out_shape=[shp] * 4, name="s5_discretize")(
        a_re, a_im, log_step.reshape(g, 1))


def _s5_scan_kernel(u_ref, bblk_ref, cblk_ref, ab_ref, d_ref, z_ref, x_scr, carry_scr, *, rows, nstate, bsz):
    @pl.when(pl.program_id(1) == 0)
    def _():
        carry_scr[...] = jnp.zeros_like(carry_scr)

    u = u_ref[...]
    x_scr[...] = jnp.dot(u.astype(BF16), bblk_ref[...], preferred_element_type=F32)
    lw = SSM_LANE_CHUNK
    second = lax.broadcasted_iota(I32, (SUBLANE, lw), 0) >= bsz
    for lt in range(nstate // lw):
        re = slice(lt * lw, (lt + 1) * lw)
        im = slice(nstate + lt * lw, nstate + (lt + 1) * lw)
        ar, ai = ab_ref[0:1, re], ab_ref[1:2, re]
        a2r, a2i = ar * ar - ai * ai, 2.0 * ar * ai
        amr, ami = jnp.where(second, ar, 0.0), jnp.where(second, ai, 0.0)
        pr, pi = jnp.where(second, a2r, ar), jnp.where(second, a2i, ai)

        def body(n, carry, re=re, im=im, amr=amr, ami=ami, pr=pr, pi=pi):
            cr, ci = carry
            r0 = pl.multiple_of(n * SUBLANE, SUBLANE)
            bur, bui = x_scr[pl.ds(r0, SUBLANE), re], x_scr[pl.ds(r0, SUBLANE), im]
            sr, si = pltpu.roll(bur, bsz, 0), pltpu.roll(bui, bsz, 0)
            wr = bur + amr * sr - ami * si
            wi = bui + amr * si + ami * sr
            cbr = jnp.where(second, cr, pltpu.roll(cr, bsz, 0))
            cbi = jnp.where(second, ci, pltpu.roll(ci, bsz, 0))
            xr = wr + pr * cbr - pi * cbi
            xi = wi + pr * cbi + pi * cbr
            x_scr[pl.ds(r0, SUBLANE), re] = xr
            x_scr[pl.ds(r0, SUBLANE), im] = xi
            return xr, xi

        cr, ci = lax.fori_loop(0, rows // SUBLANE, body, (carry_scr[:, re], carry_scr[:, im]), unroll=2)
        carry_scr[:, re] = cr
        carry_scr[:, im] = ci
    y = jnp.dot(x_scr[...].astype(BF16), cblk_ref[...], preferred_element_type=F32) + d_ref[...] * u
    z_ref[...] = jax.nn.gelu(y).astype(z_ref.dtype)


def s5_scan(u_tm, bblk, cblk, ab, d_skip, bsz):
    m, width = u_tm.shape
    assert 2 * bsz == SUBLANE, "the scan tiles two tokens of every batch element into one sublane tile"
    packs, pack_ch, two_n = bblk.shape
    nstate = two_n // 2
    rows = SSM_CHUNK_TOKENS * bsz
    return pl.pallas_call(
        functools.partial(_s5_scan_kernel, rows=rows, nstate=nstate, bsz=bsz),
        grid=(packs, m // rows),
        in_specs=[pl.BlockSpec((rows, pack_ch), lambda p, c: (c, p)),
                  pl.BlockSpec((None, pack_ch, two_n), lambda p, c: (p, 0, 0)),
                  pl.BlockSpec((None, two_n, pack_ch), lambda p, c: (p, 0, 0)),
                  pl.BlockSpec((None, 2, nstate), lambda p, c: (p, 0, 0)),
                  pl.BlockSpec((None, 1, pack_ch), lambda p, c: (p, 0, 0))],
        out_specs=pl.BlockSpec((rows, pack_ch), lambda p, c: (c, p)),
        out_shape=jax.ShapeDtypeStruct((m, width), BF16),
        scratch_shapes=[pltpu.VMEM((rows, two_n), F32), pltpu.VMEM((SUBLANE, two_n), F32)],
        compiler_params=_params("parallel", "arbitrary"), name="s5_scan",
    )(u_tm, bblk, cblk, ab, d_skip.reshape(packs, 1, pack_ch))


def s5_operators(a_re, a_im, log_step, b_re, b_im, c_re, c_im):
    g, p = a_re.shape
    gsz = b_re.shape[-1]
    gpp = SSM_PACK_CH // gsz
    packs = g // gpp
    abr, abi, fr, fi = s5_discretize(a_re, a_im, log_step)
    bbar_re = fr[..., None] * b_re - fi[..., None] * b_im
    bbar_im = fr[..., None] * b_im + fi[..., None] * b_re
    eye = jnp.eye(gpp, dtype=F32)

    def b_block(t):
        t = t.reshape(packs, gpp, p, gsz).transpose(0, 1, 3, 2)
        return (t[:, :, :, None, :] * eye[None, :, None, :, None]).reshape(packs, gpp * gsz, gpp * p)

    def c_block(t):
        t = t.reshape(packs, gpp, gsz, p).transpose(0, 1, 3, 2)
        return (t[:, :, :, None, :] * eye[None, :, None, :, None]).reshape(packs, gpp * p, gpp * gsz)

    bblk = jnp.concatenate([b_block(bbar_re), b_block(bbar_im)], axis=2).astype(BF16)
    cblk = jnp.concatenate([c_block(c_re), -c_block(c_im)], axis=1).astype(BF16)
    ab = jnp.stack([abr.reshape(packs, gpp * p), abi.reshape(packs, gpp * p)], axis=1)
    return bblk, cblk, ab


def _rope_table_kernel(pos_ref, invf_ref, cos_ref, sin_lo_ref, sin_hi_ref, *, scale):
    ang = pos_ref[...].astype(F32) * invf_ref[...]
    lane = lax.broadcasted_iota(I32, ang.shape, 1)
    c, s = jnp.cos(ang), jnp.sin(ang)
    cos_ref[...] = jnp.where(lane < ROT_DIM, c, 1.0) * scale
    sin_lo_ref[...] = jnp.where(lane < ROT_HALF, -s, 0.0) * scale
    sin_hi_ref[...] = jnp.where((lane >= ROT_HALF) & (lane < ROT_DIM), s, 0.0) * scale


def rope_tables(positions, scale):
    t = positions.size
    inv_freq = ROPE_THETA ** (-jnp.arange(ROT_HALF, dtype=F32) * 2.0 / ROT_DIM)
    invf = jnp.zeros((1, HEAD_DIM), F32).at[0, :ROT_DIM].set(jnp.concatenate([inv_freq, inv_freq]))
    ts = 1024
    shp = jax.ShapeDtypeStruct((t, HEAD_DIM), F32)
    spec = pl.BlockSpec((ts, HEAD_DIM), lambda i: (i, 0))
    return pl.pallas_call(
        functools.partial(_rope_table_kernel, scale=scale), grid=(t // ts,),
        in_specs=[pl.BlockSpec((ts, 1), lambda i: (i, 0)), pl.BlockSpec((1, HEAD_DIM), lambda i: (0, 0))],
        out_specs=[spec] * 3, out_shape=[shp] * 3,
        compiler_params=_params("parallel"), name="rope_tables",
    )(positions.reshape(t, 1), invf)


def _stream_rows(chunk, residue, dil):
    start = chunk * (ATTN_BLOCK * dil) + residue
    return pl.ds(start, ATTN_BLOCK) if dil == 1 else pl.ds(start, ATTN_BLOCK, stride=dil)


def _stream_major_store(o_ref, src_ref, dil):
    span = ATTN_BLOCK * dil
    for hd in range(src_ref.shape[0]):
        cols = slice(hd * HEAD_DIM, (hd + 1) * HEAD_DIM)
        for chunk in range(src_ref.shape[1] // span):
            for residue in range(dil):
                dst = pl.ds(chunk * span + residue * ATTN_BLOCK, ATTN_BLOCK)
                o_ref[dst, cols] = src_ref[hd, _stream_rows(chunk, residue, dil), :].astype(o_ref.dtype)


def _mm_streams_kernel(a_ref, w_ref, *rest, dils, rotary):
    rest = list(rest)
    if rotary:
        cos, sin_lo, sin_hi = [rest.pop(0)[...] for _ in range(3)]
    o_refs, scr = rest[:len(dils)], rest[len(dils)]
    acc = jnp.dot(a_ref[...], w_ref[...], preferred_element_type=F32)
    for hd in range(acc.shape[1] // HEAD_DIM):
        t = acc[:, hd * HEAD_DIM:(hd + 1) * HEAD_DIM]
        if rotary:
            t = t * cos + pltpu.roll(t, HEAD_DIM - ROT_HALF, 1) * sin_lo + pltpu.roll(t, ROT_HALF, 1) * sin_hi
        scr[hd] = t
    for o_ref, dil in zip(o_refs, dils):
        _stream_major_store(o_ref, scr, dil)


def matmul_streams(a, w, col_block0, n, dils, tables, *, tn, name):
    m, k = a.shape
    tm = ATTN_SPAN
    rotary = tables is not None
    in_specs = [pl.BlockSpec((tm, k), lambda i, j: (i, 0), pipeline_mode=pl.Buffered(1)),
                pl.BlockSpec((k, tn), lambda i, j: (0, col_block0 + j))]
    args = [a, w]
    if rotary:
        in_specs += [pl.BlockSpec((tm, HEAD_DIM), lambda i, j: (i, 0))] * 3
        args += list(tables)
    return pl.pallas_call(
        functools.partial(_mm_streams_kernel, dils=dils, rotary=rotary), grid=(m // tm, n // tn),
        in_specs=in_specs,
        out_specs=[pl.BlockSpec((tm, tn), lambda i, j: (i, j))] * len(dils),
        out_shape=[jax.ShapeDtypeStruct((m, n), BF16)] * len(dils),
        scratch_shapes=[pltpu.VMEM((tn // HEAD_DIM, tm, HEAD_DIM), F32)],
        compiler_params=_params("parallel", "arbitrary"), name=name,
    )(*args)


def _attn_unit(q, kp, kc, vp, vc, first, prev_ok, cur_ok):
    nt = (((1,), (1,)), ((), ()))
    sp = lax.dot_general(q, kp, nt, preferred_element_type=F32)
    sc = lax.dot_general(q, kc, nt, preferred_element_type=F32)
    sp = jnp.where(prev_ok & jnp.logical_not(first), sp, NEG_MASK)
    sc = jnp.where(cur_ok, sc, NEG_MASK)
    mx = jnp.maximum(jnp.max(sp, axis=-1, keepdims=True), jnp.max(sc, axis=-1, keepdims=True))
    pp, pc = jnp.exp(sp - mx), jnp.exp(sc - mx)
    den = jnp.sum(pp, axis=-1, keepdims=True) + jnp.sum(pc, axis=-1, keepdims=True)
    o = (jnp.dot((pp / den).astype(BF16), vp, preferred_element_type=F32)
         + jnp.dot((pc / den).astype(BF16), vc, preferred_element_type=F32))
    return o, mx + jnp.log(den)


def _attn_kernel(*refs, dils):
    n = len(dils)
    q_refs = refs[:n]
    kv_refs = refs[n:5 * n]
    o_ref, o_scr, l_scr = refs[5 * n], refs[5 * n + 1], refs[5 * n + 2]
    blk = ATTN_BLOCK
    seq_start = pl.program_id(1) == 0
    qi = lax.broadcasted_iota(I32, (blk, blk), 0)
    kj = lax.broadcasted_iota(I32, (blk, blk), 1)
    prev_ok, cur_ok = kj >= qi, kj <= qi
    for g, dil in enumerate(dils):
        q_ref = q_refs[g]
        kp_ref, kc_ref, vp_ref, vc_ref = kv_refs[4 * g:4 * g + 4]
        span = blk * dil
        for chunk in range(ATTN_SPAN // span):
            for residue in range(dil):
                rows = pl.ds(chunk * span + residue * blk, blk)
                if chunk == 0:
                    prev = pl.ds(residue * blk, blk)
                    kp, vp, first = kp_ref[prev, :], vp_ref[prev, :], seq_start
                else:
                    prev = pl.ds((chunk - 1) * span + residue * blk, blk)
                    kp, vp, first = kc_ref[prev, :], vc_ref[prev, :], False
                o, lse = _attn_unit(q_ref[rows, :], kp, kc_ref[rows, :], vp, vc_ref[rows, :], first, prev_ok, cur_ok)
                dst = _stream_rows(chunk, residue, dil)
                o_scr[g, dst, :] = o
                l_scr[g, dst, :] = jnp.broadcast_to(lse, (blk, HEAD_DIM))
    ls = [l_scr[g] for g in range(n)]
    mx = functools.reduce(jnp.maximum, ls)
    es = [jnp.exp(l - mx) for l in ls]
    tot = functools.reduce(jnp.add, es)
    acc = o_scr[0] * (es[0] / tot)
    for g in range(1, n):
        acc = acc + o_scr[g] * (es[g] / tot)
    o_ref[...] = acc.astype(o_ref.dtype)


def dilated_attention(qs, ks, vs, bsz, dils):
    t, d = qs[0].shape
    kvd = ks[0].shape[1]
    rep = d // kvd
    seq = t // bsz
    assert seq % ATTN_SPAN == 0
    ns = seq // ATTN_SPAN
    cur = lambda b, c, h: (b * ns + c, h // rep)
    in_specs = [pl.BlockSpec((ATTN_SPAN, HEAD_DIM), lambda b, c, h: (b * ns + c, h))] * len(dils)
    args = list(qs)
    for g, dil in enumerate(dils):
        span = ATTN_BLOCK * dil
        per = ATTN_SPAN // span
        prev = lambda b, c, h, per=per: (jnp.maximum((b * ns + c) * per - 1, 0), h // rep)
        in_specs += [pl.BlockSpec((span, HEAD_DIM), prev), pl.BlockSpec((ATTN_SPAN, HEAD_DIM), cur)] * 2
        args += [ks[g], ks[g], vs[g], vs[g]]
    return pl.pallas_call(
        functools.partial(_attn_kernel, dils=dils),
        grid=(bsz, ns, d // HEAD_DIM),
        in_specs=in_specs,
        out_specs=pl.BlockSpec((ATTN_SPAN, HEAD_DIM), lambda b, c, h: (b * ns + c, h)),
        out_shape=jax.ShapeDtypeStruct((t, d), BF16),
        scratch_shapes=[pltpu.VMEM((len(dils), ATTN_SPAN, HEAD_DIM), F32)] * 2,
        compiler_params=_params("parallel", "arbitrary", "arbitrary"), name="dilated_attention",
    )(*args)


def _router_kernel(h_ref, wr_ref, bias_ref, ek_ref, pk_ref, gk_ref, cnt_ref, run_scr, *, n_exp):
    tm = h_ref.shape[0]
    per = n_exp // N_EXPERT_GROUPS
    ng = N_EXPERT_GROUPS

    @pl.when(pl.program_id(0) == 0)
    def _():
        run_scr[...] = jnp.zeros_like(run_scr)

    logits = lax.dot_general(wr_ref[...], h_ref[...], (((1,), (1,)), ((), ())), preferred_element_type=F32)
    scores = jax.nn.sigmoid(logits)
    biased = scores + bias_ref[...]
    sc = [scores[w * ng:(w + 1) * ng] for w in range(per)]
    bi = [biased[w * ng:(w + 1) * ng] for w in range(per)]
    assert per == 4
    hi01, lo01 = jnp.maximum(bi[0], bi[1]), jnp.minimum(bi[0], bi[1])
    hi23, lo23 = jnp.maximum(bi[2], bi[3]), jnp.minimum(bi[2], bi[3])
    group_score = jnp.maximum(hi01, hi23) + jnp.maximum(jnp.minimum(hi01, hi23), jnp.maximum(lo01, lo23))
    gidx = lax.broadcasted_iota(I32, (ng, tm), 0)
    grank = jnp.zeros((ng, tm), I32)
    for g2 in range(ng):
        row = group_score[g2:g2 + 1]
        grank += ((row > group_score) | ((row == group_score) & (g2 < gidx))).astype(I32)
    gsel = grank < TOPK_GROUPS
    cand = [jnp.where(gsel, b, -jnp.inf) for b in bi]
    eid = [gidx * per + w for w in range(per)]
    erank = [jnp.zeros((ng, tm), I32) for _ in range(per)]
    for w2 in range(per):
        for g2 in range(ng):
            row = cand[w2][g2:g2 + 1]
            e2 = g2 * per + w2
            for w in range(per):
                erank[w] += ((row > cand[w]) | ((row == cand[w]) & (e2 < eid[w]))).astype(I32)
    sel = [r < TOP_K for r in erank]
    ssum = functools.reduce(jnp.add, [jnp.sum(jnp.where(m, s, 0.0), axis=0, keepdims=True) for m, s in zip(sel, sc)])
    gate = [jnp.where(m, s / ssum * ROUTED_SCALE, 0.0) for m, s in zip(sel, sc)]
    sel_all = jnp.concatenate([m.astype(F32) for m in sel], axis=0)
    earlier = (lax.broadcasted_iota(I32, (tm, tm), 0) < lax.broadcasted_iota(I32, (tm, tm), 1)).astype(BF16)
    pos = jnp.dot(sel_all.astype(BF16), earlier, preferred_element_type=F32) + run_scr[...]
    run_scr[...] += jnp.sum(sel_all, axis=1, keepdims=True)
    cnt_ref[...] = run_scr[...].astype(I32)
    above = (lax.broadcasted_iota(I32, (n_exp, n_exp), 1) < lax.broadcasted_iota(I32, (n_exp, n_exp), 0)).astype(BF16)
    kidx = jnp.dot(above, sel_all.astype(BF16), preferred_element_type=F32)
    eid_all = jnp.concatenate(eid, axis=0).astype(F32)
    gate_all = jnp.concatenate(gate, axis=0)
    for k in range(TOP_K):
        mk = (sel_all > 0.0) & (kidx == float(k))
        ek_ref[k:k + 1, :] = jnp.sum(jnp.where(mk, eid_all, 0.0), axis=0, keepdims=True).astype(I32)
        pk_ref[k:k + 1, :] = jnp.sum(jnp.where(mk, pos, 0.0), axis=0, keepdims=True).astype(I32)
        gk_ref[k:k + 1, :] = jnp.sum(jnp.where(mk, gate_all, 0.0), axis=0, keepdims=True)


def moe_route(h, router_w, router_bias):
    t, d = h.shape
    n_exp = router_w.shape[1]
    per = n_exp // N_EXPERT_GROUPS
    wr = router_w.T.reshape(N_EXPERT_GROUPS, per, d).transpose(1, 0, 2).reshape(n_exp, d).astype(BF16)
    bias = router_bias.astype(F32).reshape(N_EXPERT_GROUPS, per).T.reshape(n_exp, 1)
    tm = ROUTER_TILE
    kspec = pl.BlockSpec((TOP_K, tm), lambda i: (0, i))
    ek, pk, gk, cnt = pl.pallas_call(
        functools.partial(_router_kernel, n_exp=n_exp), grid=(t // tm,),
        in_specs=[pl.BlockSpec((tm, d), lambda i: (i, 0)), pl.BlockSpec((n_exp, d), lambda i: (0, 0)),
                  pl.BlockSpec((n_exp, 1), lambda i: (0, 0))],
        out_specs=[kspec, kspec, kspec, pl.BlockSpec((n_exp, 1), lambda i: (0, 0))],
        out_shape=[jax.ShapeDtypeStruct((TOP_K, t), I32), jax.ShapeDtypeStruct((TOP_K, t), I32),
                   jax.ShapeDtypeStruct((TOP_K, t), F32), jax.ShapeDtypeStruct((n_exp, 1), I32)],
        scratch_shapes=[pltpu.VMEM((n_exp, 1), F32)],
        compiler_params=_params("arbitrary"), name="moe_router",
    )(h, wr, bias)
    counts = cnt.reshape(per, N_EXPERT_GROUPS).T.reshape(n_exp)
    return ek, pk, gk, counts


def _row_copy(src_ref, src_row, dst_ref, dst_row, sem):
    return pltpu.make_async_copy(src_ref.at[pl.ds(src_row, 1)], dst_ref.at[pl.ds(dst_row, 1)], sem)


def _rows_done(src_ref, dst_ref, sem):
    pltpu.make_async_copy(src_ref.at[pl.ds(0, dst_ref.shape[0])], dst_ref, sem).wait()


def _slot_token_kernel(slot_ref, tok_ref, *, n_tok, steps):
    phase, j = pl.program_id(0), pl.program_id(1)

    @pl.when(phase == 0)
    def _():
        per = tok_ref.shape[0] // steps
        def clear(r, _):
            tok_ref[j * per + r] = 0
            return 0
        lax.fori_loop(0, per, clear, 0)

    @pl.when(phase == 1)
    def _():
        per = n_tok // steps
        for k in range(TOP_K):
            def put(r, _, k=k):
                tok = j * per + r
                tok_ref[slot_ref[k * n_tok + tok]] = tok
                return 0
            lax.fori_loop(0, per, put, 0)


def moe_slot_tokens(slots, n_slots):
    n_tok = slots.shape[0] // TOP_K
    steps = 64
    assert n_slots % steps == 0 and n_tok % steps == 0
    return pl.pallas_call(
        functools.partial(_slot_token_kernel, n_tok=n_tok, steps=steps),
        grid_spec=pltpu.PrefetchScalarGridSpec(
            num_scalar_prefetch=1, grid=(2, steps), in_specs=[],
            out_specs=pl.BlockSpec(memory_space=pltpu.SMEM)),
        out_shape=jax.ShapeDtypeStruct((n_slots,), I32),
        compiler_params=_params("arbitrary", "arbitrary"), name="moe_slot_tokens",
    )(slots)


def _expert_kernel(te_ref, nused_ref, tok_ref, h_ref, wg_ref, wu_ref, wd_ref, ys_ref, xbuf, sem, *, te):
    i = pl.program_id(0)
    n_used = nused_ref[0]

    def gather(tile, buf):
        def issue(r, _):
            _row_copy(h_ref, tok_ref[tile * te + r], xbuf.at[buf], r, sem.at[buf]).start()
            return 0
        lax.fori_loop(0, te, issue, 0, unroll=8)

    @pl.when(i == 0)
    def _():
        gather(0, 0)

    @pl.when(i + 1 < n_used)
    def _():
        gather(i + 1, (i + 1) % 2)

    @pl.when(i < n_used)
    def _():
        buf = i % 2
        _rows_done(h_ref, xbuf.at[buf], sem.at[buf])
        x = xbuf[buf].astype(BF16)
        g = jnp.dot(x, wg_ref[...], preferred_element_type=F32)
        u = jnp.dot(x, wu_ref[...], preferred_element_type=F32)
        a = (jax.nn.silu(g) * u).astype(BF16)
        ys_ref[...] = jnp.dot(a, wd_ref[...], preferred_element_type=F32)

    @pl.when(i >= n_used)
    def _():
        ys_ref[...] = jnp.zeros_like(ys_ref)


def moe_experts(h32, slot_tok, tile_expert, n_used, wg, wu, wd):
    t, d = h32.shape
    n_slots = slot_tok.shape[0]
    de = wg.shape[2]
    te = EXPERT_TILE
    return pl.pallas_call(
        functools.partial(_expert_kernel, te=te),
        grid_spec=pltpu.PrefetchScalarGridSpec(
            num_scalar_prefetch=3, grid=(n_slots // te,),
            in_specs=[pl.BlockSpec(memory_space=pl.ANY),
                      pl.BlockSpec((None, d, de), lambda i, e, n, tk: (e[i], 0, 0)),
                      pl.BlockSpec((None, d, de), lambda i, e, n, tk: (e[i], 0, 0)),
                      pl.BlockSpec((None, de, d), lambda i, e, n, tk: (e[i], 0, 0))],
            out_specs=pl.BlockSpec((te, d), lambda i, e, n, tk: (i, 0)),
            scratch_shapes=[pltpu.VMEM((2, te, d), F32), pltpu.SemaphoreType.DMA((2,))]),
        out_shape=jax.ShapeDtypeStruct((n_slots, d), F32),
        compiler_params=_params("arbitrary"), name="moe_experts",
    )(tile_expert, n_used, slot_tok, h32, wg, wu, wd)


def _combine_kernel(slot_ref, gk_ref, sh_ref, x_ref, gate_ref, ng_ref, ys_ref, o_ref, buf, sem, *, tm, n_tok):
    i = pl.program_id(0)

    def gather(tile, b):
        for k in range(TOP_K):
            def issue(r, _, k=k):
                _row_copy(ys_ref, slot_ref[k * n_tok + tile * tm + r], buf.at[b, k], r, sem.at[b]).start()
                return 0
            lax.fori_loop(0, tm, issue, 0, unroll=8)

    @pl.when(i == 0)
    def _():
        gather(0, 0)

    @pl.when(i + 1 < pl.num_programs(0))
    def _():
        gather(i + 1, (i + 1) % 2)

    b = i % 2
    for k in range(TOP_K):
        _rows_done(ys_ref, buf.at[b, k], sem.at[b])
    y = sh_ref[...].astype(F32)
    gk = gk_ref[...]
    for k in range(TOP_K):
        y = y + gk[:, k:k + 1] * buf[b, k]
    yn = y * lax.rsqrt(jnp.mean(y * y, axis=-1, keepdims=True) + NORM_EPS) * ng_ref[...]
    o_ref[...] = x_ref[...] + gate_ref[...] * yn


def moe_combine(x, ys, slots, gk_t, shared, norm_g, gate):
    bsz, seq, d = x.shape
    tm = COMBINE_TILE
    nt = seq // tm
    return pl.pallas_call(
        functools.partial(_combine_kernel, tm=tm, n_tok=bsz * seq),
        grid_spec=pltpu.PrefetchScalarGridSpec(
            num_scalar_prefetch=1, grid=(bsz * nt,),
            in_specs=[pl.BlockSpec((tm, TOP_K), lambda i, s: (i, 0)),
                      pl.BlockSpec((tm, d), lambda i, s: (i, 0)),
                      pl.BlockSpec((None, tm, d), lambda i, s: (i // nt, i % nt, 0)),
                      pl.BlockSpec((None, 1, d), lambda i, s: (i // nt, 0, 0)),
                      pl.BlockSpec((1, d), lambda i, s: (0, 0)),
                      pl.BlockSpec(memory_space=pl.ANY)],
            out_specs=pl.BlockSpec((None, tm, d), lambda i, s: (i // nt, i % nt, 0)),
            scratch_shapes=[pltpu.VMEM((2, TOP_K, tm, d), F32), pltpu.SemaphoreType.DMA((2,))]),
        out_shape=jax.ShapeDtypeStruct(x.shape, F32),
        compiler_params=_params("arbitrary"), name="moe_combine",
    )(slots, gk_t, shared, x, gate.reshape(bsz, 1, d), norm_g.reshape(1, d), ys)


def moe_layer(x, h, h32, norm_g, gate, router_w, router_bias, e_gate, e_up, e_down, s_gate, s_up, s_down):
    t, d = h.shape
    n_exp = router_w.shape[1]
    te = EXPERT_TILE
    ek, pk, gk, counts = moe_route(h, router_w, router_bias)
    tiles = (counts + te - 1) // te
    tile_end = jnp.cumsum(tiles)
    offsets = (tile_end - tiles) * te
    experts = jnp.arange(n_exp, dtype=I32)
    slot = pk + jnp.sum(jnp.where(ek[..., None] == experts, offsets, 0), axis=-1)
    slots = slot.reshape(-1).astype(I32)
    n_tiles = (t * TOP_K) // te + n_exp
    n_used = tile_end[-1].astype(I32)
    tile_ids = jnp.arange(n_tiles, dtype=I32)
    tile_expert = jnp.sum((tile_end[None, :] <= jnp.minimum(tile_ids, n_used - 1)[:, None]).astype(I32), axis=1)
    slot_tok = moe_slot_tokens(slots, n_tiles * te)
    ys = moe_experts(h32, slot_tok, tile_expert, n_used.reshape(1),
                     e_gate.astype(BF16), e_up.astype(BF16), e_down.astype(BF16))
    act = matmul_swiglu(h, s_gate.astype(BF16), s_up.astype(BF16), tm=1024, tn=512, name="shared_up")
    shared = matmul(act, s_down.astype(BF16), BF16, tm=1024, tn=1024, name="shared_down")
    return moe_combine(x, ys, slots, gk.T, shared, norm_g, gate)


def kernel(x, c, positions, ada_w, ada_b, norm_g, ssm_w_in, ssm_a_re, ssm_a_im, ssm_log_step, ssm_b_re, ssm_b_im, ssm_c_re, ssm_c_im, ssm_d, glu_w, glu_b, ssm_w_out, kv_norm_g, w_k, w_v, attn_w_q, attn_w_o, router_w, router_bias, exp_w_gate, exp_w_up, exp_w_down, sh_w_gate, sh_w_up, sh_w_down):
    bsz, seq, d = x.shape
    depth = ada_w.shape[0]
    n_a = ssm_w_in.shape[0]
    t = bsz * seq
    mod = ada_modulation(c, ada_w, ada_b)
    k = v = None
    for layer in range(depth):
        sh1, sc1, g1, sh2, sc2, g2 = jnp.split(mod[layer], N_MOD, axis=-1)
        if layer < n_a:
            a = layer
            (h,) = norm_modulate(x, norm_g[layer, 0], sc1, sh1)
            u_tm = matmul_to_token_major(h, ssm_w_in[a].astype(BF16), bsz, tm=1024, tn=512, name="ssm_in")
            bblk, cblk, ab = s5_operators(ssm_a_re[a], ssm_a_im[a], ssm_log_step[a], ssm_b_re[a], ssm_b_im[a],
                                          ssm_c_re[a], ssm_c_im[a])
            z = s5_scan(u_tm.reshape(seq * bsz, -1), bblk, cblk, ab, ssm_d[a], bsz)
            zz = matmul_glu(z, glu_w[a].astype(BF16), glu_b[a], tm=1024, tn=512, name="ssm_glu")
            y = matmul_from_token_major(zz.reshape(seq, -1), ssm_w_out[a].astype(BF16), bsz, BF16,
                                        tm=1024, tn=512, name="ssm_out")
        else:
            bl = layer - n_a
            n_dil = attn_w_q.shape[2] // d
            assert all(window // dil == ATTN_BLOCK for window, dil in DILATION_GROUPS[:n_dil])
            dils = tuple(dil for _, dil in DILATION_GROUPS[:n_dil])
            if layer == n_a:
                h, hkv = norm_modulate(x, norm_g[layer, 0], sc1, sh1, kv_g=kv_norm_g)
                kvd = w_k.shape[1]
                k_tabs = rope_tables(positions, 1.0)
                k = matmul_streams(hkv, w_k.astype(BF16), 0, kvd, dils, k_tabs, tn=min(kvd, 512), name="k_proj")
                v = matmul_streams(hkv, w_v.astype(BF16), 0, kvd, dils, None, tn=min(kvd, 512), name="v_proj")
            else:
                (h,) = norm_modulate(x, norm_g[layer, 0], sc1, sh1)
            q_tabs = rope_tables(positions, HEAD_DIM ** -0.5)
            wq = attn_w_q[bl].astype(BF16)
            tnq = min(d, 256)
            qs = [matmul_streams(h, wq, grp * (d // tnq), d, (dil,), q_tabs, tn=tnq, name=f"q_proj_{dil}")[0]
                  for grp, dil in enumerate(dils)]
            o = dilated_attention(qs, k, v, bsz, dils)
            y = matmul(o, attn_w_o[bl].astype(BF16), BF16, tm=1024, tn=512, name="attn_out")
        x = residual_norm(x, y, norm_g[layer, 1], g1)
        h, h32 = norm_modulate(x, norm_g[layer, 2], sc2, sh2, with_f32=True)
        x = moe_layer(x, h, h32, norm_g[layer, 3], g2, router_w[layer], router_bias[layer],
                      exp_w_gate[layer], exp_w_up[layer], exp_w_down[layer],
                      sh_w_gate[layer], sh_w_up[layer], sh_w_down[layer])
    return x
```

```python
import functools
import math

import jax
import jax.numpy as jnp
from jax import lax
from jax.experimental import pallas as pl
from jax.experimental.pallas import tpu as pltpu

F32 = jnp.float32
BF16 = jnp.bfloat16
I32 = jnp.int32

LANE = 128
SUBLANE = 8
VMEM_LIMIT_BYTES = 56 * 1024 * 1024
MXU_DIM = 256

N_MOD = 6
NORM_EPS = 1e-6
HEAD_DIM = 128
ROT_DIM = HEAD_DIM // 4
ROT_HALF = ROT_DIM // 2
ROPE_THETA = 500000.0
DILATION_GROUPS = ((128, 1), (512, 4), (2048, 16))
ATTN_BLOCK = 128
ATTN_SPAN = ATTN_BLOCK * max(dil for _, dil in DILATION_GROUPS)
N_EXPERT_GROUPS = 8
TOPK_GROUPS = 4
TOP_K = 4
ROUTED_SCALE = 2.5
NEG_MASK = -1e30

SSM_PACK_CH = MXU_DIM
SSM_CHUNK_TOKENS = 256
SSM_LANE_CHUNK = 512
EXPERT_TILE = 256
ROW_TILE = 256
COMBINE_TILE = 128
ROUTER_TILE = 512
CAST_BLOCK_BYTES = 4 * 1024 * 1024


def _params(*sem):
    return pltpu.CompilerParams(dimension_semantics=sem, vmem_limit_bytes=VMEM_LIMIT_BYTES)


def _cast_kernel(x_ref, o_ref):
    o_ref[...] = x_ref[...].astype(o_ref.dtype)


def cast_bf16(w, layer):
    lead, c = w.shape[1:-1], w.shape[-1]
    rows = math.prod(lead)
    w3 = w.reshape(w.shape[0], rows, c)
    br = min(rows, 1 << ((CAST_BLOCK_BYTES // (4 * c)).bit_length() - 1))
    assert rows % br == 0 and br % (2 * SUBLANE) == 0
    out = pl.pallas_call(
        _cast_kernel, grid=(rows // br,),
        in_specs=[pl.BlockSpec((None, br, c), lambda i: (layer, i, 0))],
        out_specs=pl.BlockSpec((br, c), lambda i: (i, 0)),
        out_shape=jax.ShapeDtypeStruct((rows, c), BF16),
        compiler_params=_params("parallel"), name="cast_bf16",
    )(w3)
    return out.reshape(*lead, c)


def _ada_kernel(c_ref, w_ref, b_ref, o_ref):
    cond = jax.nn.silu(c_ref[...])
    o_ref[...] = jnp.dot(cond, w_ref[...], precision=lax.Precision.HIGHEST,
                         preferred_element_type=F32) + b_ref[...]


def ada_modulation(c, ada_w, ada_b):
    depth, d, n = ada_w.shape
    bsz = c.shape[0]
    rows = -(-bsz // SUBLANE) * SUBLANE
    cp = jnp.zeros((rows, d), F32).at[:bsz].set(c)
    tn = 512
    out = pl.pallas_call(
        _ada_kernel,
        grid=(depth, n // tn),
        in_specs=[pl.BlockSpec((rows, d), lambda l, j: (0, 0)),
                  pl.BlockSpec((None, d, tn), lambda l, j: (l, 0, j)),
                  pl.BlockSpec((None, 1, tn), lambda l, j: (l, 0, j))],
        out_specs=pl.BlockSpec((None, rows, tn), lambda l, j: (l, 0, j)),
        out_shape=jax.ShapeDtypeStruct((depth, rows, n), F32),
        compiler_params=_params("parallel", "parallel"),
        name="ada_modulation",
    )(cp, ada_w, ada_b.reshape(depth, 1, n))
    return out[:, :bsz]


def _normmod_kernel(x_ref, g_ref, sc_ref, sh_ref, *rest, with_kv, with_f32):
    x = x_ref[...]
    xn = x * lax.rsqrt(jnp.mean(x * x, axis=-1, keepdims=True) + NORM_EPS)
    h = xn * g_ref[...] * (1.0 + sc_ref[...]) + sh_ref[...]
    rest = list(rest)
    if with_kv:
        kvg_ref = rest.pop(0)
    h_ref = rest.pop(0)
    h_ref[...] = h.astype(h_ref.dtype)
    if with_kv:
        rest.pop(0)[...] = (xn * kvg_ref[...]).astype(BF16)
    if with_f32:
        rest.pop(0)[...] = h


def norm_modulate(x, g, sc, sh, kv_g=None, with_f32=False):
    bsz, seq, d = x.shape
    ts = ROW_TILE
    nt = seq // ts
    row = lambda b, i: (b * nt + i, 0)
    vec = pl.BlockSpec((1, d), lambda b, i: (0, 0))
    bvec = pl.BlockSpec((None, 1, d), lambda b, i: (b, 0, 0))
    in_specs = [pl.BlockSpec((None, ts, d), lambda b, i: (b, i, 0)), vec, bvec, bvec]
    args = [x, g.reshape(1, d), sc.reshape(bsz, 1, d), sh.reshape(bsz, 1, d)]
    out_specs = [pl.BlockSpec((ts, d), row)]
    out_shape = [jax.ShapeDtypeStruct((bsz * seq, d), BF16)]
    if kv_g is not None:
        in_specs.append(vec)
        args.append(kv_g.reshape(1, d))
        out_specs.append(pl.BlockSpec((ts, d), row))
        out_shape.append(jax.ShapeDtypeStruct((bsz * seq, d), BF16))
    if with_f32:
        out_specs.append(pl.BlockSpec((ts, d), row))
        out_shape.append(jax.ShapeDtypeStruct((bsz * seq, d), F32))
    return pl.pallas_call(
        functools.partial(_normmod_kernel, with_kv=kv_g is not None, with_f32=with_f32),
        grid=(bsz, nt), in_specs=in_specs, out_specs=out_specs, out_shape=out_shape,
        compiler_params=_params("parallel", "parallel"), name="norm_modulate",
    )(*args)


def _resid_norm_kernel(x_ref, y_ref, g_ref, gate_ref, o_ref):
    y = y_ref[...].astype(F32)
    yn = y * lax.rsqrt(jnp.mean(y * y, axis=-1, keepdims=True) + NORM_EPS) * g_ref[...]
    o_ref[...] = x_ref[...] + gate_ref[...] * yn


def residual_norm(x, y, g, gate):
    bsz, seq, d = x.shape
    ts = ROW_TILE
    nt = seq // ts
    return pl.pallas_call(
        _resid_norm_kernel,
        grid=(bsz, nt),
        in_specs=[pl.BlockSpec((None, ts, d), lambda b, i: (b, i, 0)),
                  pl.BlockSpec((ts, d), lambda b, i: (b * nt + i, 0)),
                  pl.BlockSpec((1, d), lambda b, i: (0, 0)),
                  pl.BlockSpec((None, 1, d), lambda b, i: (b, 0, 0))],
        out_specs=pl.BlockSpec((None, ts, d), lambda b, i: (b, i, 0)),
        out_shape=jax.ShapeDtypeStruct(x.shape, F32),
        compiler_params=_params("parallel", "parallel"), name="residual_norm",
    )(x, y, g.reshape(1, d), gate.reshape(bsz, 1, d))


def _mm_kernel(a_ref, w_ref, o_ref):
    o_ref[...] = jnp.dot(a_ref[...], w_ref[...], preferred_element_type=F32).astype(o_ref.dtype)


def _mm_glu_kernel(a_ref, w_ref, z_ref, b_ref, o_ref):
    acc = jnp.dot(a_ref[...], w_ref[...], preferred_element_type=F32) + b_ref[...]
    o_ref[...] = (z_ref[...].astype(F32) * jax.nn.sigmoid(acc)).astype(o_ref.dtype)


def _mm_swiglu_kernel(a_ref, wg_ref, wu_ref, o_ref):
    a = a_ref[...]
    g = jnp.dot(a, wg_ref[...], preferred_element_type=F32)
    u = jnp.dot(a, wu_ref[...], preferred_element_type=F32)
    o_ref[...] = (jax.nn.silu(g) * u).astype(o_ref.dtype)


def matmul(a, w, out_dtype, *, tm, tn, name):
    m, k = a.shape
    n = w.shape[1]
    tm, tn = min(tm, m), min(tn, n)
    return pl.pallas_call(
        _mm_kernel, grid=(m // tm, n // tn),
        in_specs=[pl.BlockSpec((tm, k), lambda i, j: (i, 0)), pl.BlockSpec((k, tn), lambda i, j: (0, j))],
        out_specs=pl.BlockSpec((tm, tn), lambda i, j: (i, j)),
        out_shape=jax.ShapeDtypeStruct((m, n), out_dtype),
        compiler_params=_params("parallel", "arbitrary"), name=name,
    )(a, w)


def matmul_swiglu(a, wg, wu, *, tm, tn, name):
    m, k = a.shape
    n = wg.shape[1]
    tm, tn = min(tm, m), min(tn, n)
    wspec = pl.BlockSpec((k, tn), lambda i, j: (0, j))
    return pl.pallas_call(
        _mm_swiglu_kernel, grid=(m // tm, n // tn),
        in_specs=[pl.BlockSpec((tm, k), lambda i, j: (i, 0)), wspec, wspec],
        out_specs=pl.BlockSpec((tm, tn), lambda i, j: (i, j)),
        out_shape=jax.ShapeDtypeStruct((m, n), BF16),
        compiler_params=_params("parallel", "arbitrary"), name=name,
    )(a, wg, wu)


def matmul_glu(z, w, b, *, tm, tn, name):
    m, k = z.shape
    n = w.shape[1]
    tm, tn = min(tm, m), min(tn, n)
    return pl.pallas_call(
        _mm_glu_kernel, grid=(m // tm, n // tn),
        in_specs=[pl.BlockSpec((tm, k), lambda i, j: (i, 0)), pl.BlockSpec((k, tn), lambda i, j: (0, j)),
                  pl.BlockSpec((tm, tn), lambda i, j: (i, j)), pl.BlockSpec((1, tn), lambda i, j: (0, j))],
        out_specs=pl.BlockSpec((tm, tn), lambda i, j: (i, j)),
        out_shape=jax.ShapeDtypeStruct((m, n), BF16),
        compiler_params=_params("parallel", "arbitrary"), name=name,
    )(z, w, z, b.reshape(1, n))


def matmul_to_token_major(a, w, bsz, *, tm, tn, name):
    m, k = a.shape
    n = w.shape[1]
    seq = m // bsz
    tm, tn = min(tm, seq), min(tn, n)
    nt, nj = seq // tm, n // tn
    return pl.pallas_call(
        _mm_kernel, grid=(bsz, nt, nj),
        in_specs=[pl.BlockSpec((tm, k), lambda b, i, j: (b * nt + i, 0)),
                  pl.BlockSpec((k, tn), lambda b, i, j: (0, j))],
        out_specs=pl.BlockSpec((tm, tn), lambda b, i, j: (i, b * nj + j)),
        out_shape=jax.ShapeDtypeStruct((seq, bsz * n), F32),
        compiler_params=_params("parallel", "parallel", "arbitrary"), name=name,
    )(a, w)


def matmul_from_token_major(a, w, bsz, out_dtype, *, tm, tn, name):
    seq = a.shape[0]
    k, n = w.shape
    tm, tn = min(tm, seq), min(tn, n)
    nt, nj = seq // tm, n // tn
    return pl.pallas_call(
        _mm_kernel, grid=(bsz, nt, nj),
        in_specs=[pl.BlockSpec((tm, k), lambda b, i, j: (i, b)),
                  pl.BlockSpec((k, tn), lambda b, i, j: (0, j))],
        out_specs=pl.BlockSpec((tm, tn), lambda b, i, j: (b * nt + i, j)),
        out_shape=jax.ShapeDtypeStruct((bsz * seq, n), out_dtype),
        compiler_params=_params("parallel", "parallel", "arbitrary"), name=name,
    )(a, w)


def _s5_discretize_kernel(are_ref, aim_ref, ls_ref, abr_ref, abi_ref, fr_ref, fi_ref):
    lr, li = are_ref[...], aim_ref[...]
    step = jnp.exp(ls_ref[...])
    decay = jnp.exp(lr * step)
    abar_re = decay * jnp.cos(li * step)
    abar_im = decay * jnp.sin(li * step)
    denom = lr * lr + li * li
    nr = abar_re - 1.0
    ni = abar_im
    abr_ref[...] = abar_re
    abi_ref[...] = abar_im
    fr_ref[...] = (nr * lr + ni * li) / denom
    fi_ref[...] = (ni * lr - nr * li) / denom


def s5_discretize(a_re, a_im, log_step):
    g, p = a_re.shape
    shp = jax.ShapeDtypeStruct((g, p), F32)
    return pl.pallas_call(_s5_discretize_kernel, out_shape=[shp] * 4, name="s5_discretize")(
        a_re, a_im, log_step.reshape(g, 1))


def _s5_scan_kernel(u_ref, bblk_ref, cblk_ref, ab_ref, d_ref, z_ref, x_scr, carry_scr, *, rows, nstate, bsz):
    @pl.when(pl.program_id(1) == 0)
    def _():
        carry_scr[...] = jnp.zeros_like(carry_scr)

    u = u_ref[...]
    x_scr[...] = jnp.dot(u.astype(BF16), bblk_ref[...], preferred_element_type=F32)
    lw = SSM_LANE_CHUNK
    second = lax.broadcasted_iota(I32, (SUBLANE, lw), 0) >= bsz
    for lt in range(nstate // lw):
        re = slice(lt * lw, (lt + 1) * lw)
        im = slice(nstate + lt * lw, nstate + (lt + 1) * lw)
        ar, ai = ab_ref[0:1, re], ab_ref[1:2, re]
        a2r, a2i = ar * ar - ai * ai, 2.0 * ar * ai
        amr, ami = jnp.where(second, ar, 0.0), jnp.where(second, ai, 0.0)
        pr, pi = jnp.where(second, a2r, ar), jnp.where(second, a2i, ai)

        def body(n, carry, re=re, im=im, amr=amr, ami=ami, pr=pr, pi=pi):
            cr, ci = carry
            r0 = pl.multiple_of(n * SUBLANE, SUBLANE)
            bur, bui = x_scr[pl.ds(r0, SUBLANE), re], x_scr[pl.ds(r0, SUBLANE), im]
            sr, si = pltpu.roll(bur, bsz, 0), pltpu.roll(bui, bsz, 0)
            wr = bur + amr * sr - ami * si
            wi = bui + amr * si + ami * sr
            cbr = jnp.where(second, cr, pltpu.roll(cr, bsz, 0))
            cbi = jnp.where(second, ci, pltpu.roll(ci, bsz, 0))
            xr = wr + pr * cbr - pi * cbi
            xi = wi + pr * cbi + pi * cbr
            x_scr[pl.ds(r0, SUBLANE), re] = xr
            x_scr[pl.ds(r0, SUBLANE), im] = xi
            return xr, xi

        cr, ci = lax.fori_loop(0, rows // SUBLANE, body, (carry_scr[:, re], carry_scr[:, im]), unroll=2)
        carry_scr[:, re] = cr
        carry_scr[:, im] = ci
    y = jnp.dot(x_scr[...].astype(BF16), cblk_ref[...], preferred_element_type=F32) + d_ref[...] * u
    z_ref[...] = jax.nn.gelu(y).astype(z_ref.dtype)


def s5_scan(u_tm, bblk, cblk, ab, d_skip, bsz):
    m, width = u_tm.shape
    assert 2 * bsz == SUBLANE, "the scan tiles two tokens of every batch element into one sublane tile"
    packs, pack_ch, two_n = bblk.shape
    nstate = two_n // 2
    rows = SSM_CHUNK_TOKENS * bsz
    return pl.pallas_call(
        functools.partial(_s5_scan_kernel, rows=rows, nstate=nstate, bsz=bsz),
        grid=(packs, m // rows),
        in_specs=[pl.BlockSpec((rows, pack_ch), lambda p, c: (c, p)),
                  pl.BlockSpec((None, pack_ch, two_n), lambda p, c: (p, 0, 0)),
                  pl.BlockSpec((None, two_n, pack_ch), lambda p, c: (p, 0, 0)),
                  pl.BlockSpec((None, 2, nstate), lambda p, c: (p, 0, 0)),
                  pl.BlockSpec((None, 1, pack_ch), lambda p, c: (p, 0, 0))],
        out_specs=pl.BlockSpec((rows, pack_ch), lambda p, c: (c, p)),
        out_shape=jax.ShapeDtypeStruct((m, width), BF16),
        scratch_shapes=[pltpu.VMEM((rows, two_n), F32), pltpu.VMEM((SUBLANE, two_n), F32)],
        compiler_params=_params("parallel", "arbitrary"), name="s5_scan",
    )(u_tm, bblk, cblk, ab, d_skip.reshape(packs, 1, pack_ch))


def s5_operators(a_re, a_im, log_step, b_re, b_im, c_re, c_im):
    g, p = a_re.shape
    gsz = b_re.shape[-1]
    gpp = SSM_PACK_CH // gsz
    packs = g // gpp
    abr, abi, fr, fi = s5_discretize(a_re, a_im, log_step)
    bbar_re = fr[..., None] * b_re - fi[..., None] * b_im
    bbar_im = fr[..., None] * b_im + fi[..., None] * b_re
    eye = jnp.eye(gpp, dtype=F32)

    def b_block(t):
        t = t.reshape(packs, gpp, p, gsz).transpose(0, 1, 3, 2)
        return (t[:, :, :, None, :] * eye[None, :, None, :, None]).reshape(packs, gpp * gsz, gpp * p)

    def c_block(t):
        t = t.reshape(packs, gpp, gsz, p).transpose(0, 1, 3, 2)
        return (t[:, :, :, None, :] * eye[None, :, None, :, None]).reshape(packs, gpp * p, gpp * gsz)

    bblk = jnp.concatenate([b_block(bbar_re), b_block(bbar_im)], axis=2).astype(BF16)
    cblk = jnp.concatenate([c_block(c_re), -c_block(c_im)], axis=1).astype(BF16)
    ab = jnp.stack([abr.reshape(packs, gpp * p), abi.reshape(packs, gpp * p)], axis=1)
    return bblk, cblk, ab


def _rope_table_kernel(pos_ref, invf_ref, cos_ref, sin_lo_ref, sin_hi_ref, *, scale):
    ang = pos_ref[...].astype(F32) * invf_ref[...]
    lane = lax.broadcasted_iota(I32, ang.shape, 1)
    c, s = jnp.cos(ang), jnp.sin(ang)
    cos_ref[...] = jnp.where(lane < ROT_DIM, c, 1.0) * scale
    sin_lo_ref[...] = jnp.where(lane < ROT_HALF, -s, 0.0) * scale
    sin_hi_ref[...] = jnp.where((lane >= ROT_HALF) & (lane < ROT_DIM), s, 0.0) * scale


def rope_tables(positions, scale):
    t = positions.size
    inv_freq = ROPE_THETA ** (-jnp.arange(ROT_HALF, dtype=F32) * 2.0 / ROT_DIM)
    invf = jnp.zeros((1, HEAD_DIM), F32).at[0, :ROT_DIM].set(jnp.concatenate([inv_freq, inv_freq]))
    ts = 1024
    shp = jax.ShapeDtypeStruct((t, HEAD_DIM), F32)
    spec = pl.BlockSpec((ts, HEAD_DIM), lambda i: (i, 0))
    return pl.pallas_call(
        functools.partial(_rope_table_kernel, scale=scale), grid=(t // ts,),
        in_specs=[pl.BlockSpec((ts, 1), lambda i: (i, 0)), pl.BlockSpec((1, HEAD_DIM), lambda i: (0, 0))],
        out_specs=[spec] * 3, out_shape=[shp] * 3,
        compiler_params=_params("parallel"), name="rope_tables",
    )(positions.reshape(t, 1), invf)


def _stream_rows(chunk, residue, dil):
    start = chunk * (ATTN_BLOCK * dil) + residue
    return pl.ds(start, ATTN_BLOCK) if dil == 1 else pl.ds(start, ATTN_BLOCK, stride=dil)


def _stream_major_store(o_ref, src_ref, dil):
    span = ATTN_BLOCK * dil
    for hd in range(src_ref.shape[0]):
        cols = slice(hd * HEAD_DIM, (hd + 1) * HEAD_DIM)
        for chunk in range(src_ref.shape[1] // span):
            for residue in range(dil):
                dst = pl.ds(chunk * span + residue * ATTN_BLOCK, ATTN_BLOCK)
                o_ref[dst, cols] = src_ref[hd, _stream_rows(chunk, residue, dil), :].astype(o_ref.dtype)


def _mm_streams_kernel(a_ref, w_ref, *rest, dils, rotary):
    rest = list(rest)
    if rotary:
        cos, sin_lo, sin_hi = [rest.pop(0)[...] for _ in range(3)]
    o_refs, scr = rest[:len(dils)], rest[len(dils)]
    acc = jnp.dot(a_ref[...], w_ref[...], preferred_element_type=F32)
    for hd in range(acc.shape[1] // HEAD_DIM):
        t = acc[:, hd * HEAD_DIM:(hd + 1) * HEAD_DIM]
        if rotary:
            t = t * cos + pltpu.roll(t, HEAD_DIM - ROT_HALF, 1) * sin_lo + pltpu.roll(t, ROT_HALF, 1) * sin_hi
        scr[hd] = t
    for o_ref, dil in zip(o_refs, dils):
        _stream_major_store(o_ref, scr, dil)


def matmul_streams(a, w, col_block0, n, dils, tables, *, tn, name):
    m, k = a.shape
    tm = ATTN_SPAN
    rotary = tables is not None
    in_specs = [pl.BlockSpec((tm, k), lambda i, j: (i, 0), pipeline_mode=pl.Buffered(1)),
                pl.BlockSpec((k, tn), lambda i, j: (0, col_block0 + j))]
    args = [a, w]
    if rotary:
        in_specs += [pl.BlockSpec((tm, HEAD_DIM), lambda i, j: (i, 0))] * 3
        args += list(tables)
    return pl.pallas_call(
        functools.partial(_mm_streams_kernel, dils=dils, rotary=rotary), grid=(m // tm, n // tn),
        in_specs=in_specs,
        out_specs=[pl.BlockSpec((tm, tn), lambda i, j: (i, j))] * len(dils),
        out_shape=[jax.ShapeDtypeStruct((m, n), BF16)] * len(dils),
        scratch_shapes=[pltpu.VMEM((tn // HEAD_DIM, tm, HEAD_DIM), F32)],
        compiler_params=_params("parallel", "arbitrary"), name=name,
    )(*args)


def _attn_kernel(*refs, dils):
    n = len(dils)
    q_refs = refs[:n]
    kv_refs = refs[n:5 * n]
    o_ref, o_scr, l_scr = refs[5 * n], refs[5 * n + 1], refs[5 * n + 2]
    blk = ATTN_BLOCK
    nb = ATTN_SPAN // blk
    seq_start = pl.program_id(1) == 0
    qi = lax.broadcasted_iota(I32, (nb, blk, blk), 1)
    kj = lax.broadcasted_iota(I32, (nb, blk, blk), 2)
    bi = lax.broadcasted_iota(I32, (nb, blk, blk), 0)
    cur_ok = kj <= qi
    qk = (((2,), (2,)), ((0,), (0,)))
    pv = (((2,), (1,)), ((0,), (0,)))
    for g, dil in enumerate(dils):
        kp_ref, kc_ref, vp_ref, vc_ref = kv_refs[4 * g:4 * g + 4]
        span = blk * dil
        kc, vc = kc_ref[...], vc_ref[...]
        if span == ATTN_SPAN:
            kp, vp = kp_ref[...], vp_ref[...]
        else:
            kp = jnp.concatenate([kp_ref[...], kc[:ATTN_SPAN - span]], axis=0)
            vp = jnp.concatenate([vp_ref[...], vc[:ATTN_SPAN - span]], axis=0)
        split = lambda t: t.reshape(nb, blk, HEAD_DIM)
        q3, kp3, kc3, vp3, vc3 = split(q_refs[g][...]), split(kp), split(kc), split(vp), split(vc)
        prev_ok = (kj >= qi) & jnp.logical_not(seq_start & (bi < dil))
        sp = jnp.where(prev_ok, lax.dot_general(q3, kp3, qk, preferred_element_type=F32), NEG_MASK)
        sc = jnp.where(cur_ok, lax.dot_general(q3, kc3, qk, preferred_element_type=F32), NEG_MASK)
        mx = jnp.maximum(jnp.max(sp, axis=-1, keepdims=True), jnp.max(sc, axis=-1, keepdims=True))
        pp, pc = jnp.exp(sp - mx), jnp.exp(sc - mx)
        den = jnp.sum(pp, axis=-1, keepdims=True) + jnp.sum(pc, axis=-1, keepdims=True)
        o = (lax.dot_general((pp / den).astype(BF16), vp3, pv, preferred_element_type=F32)
             + lax.dot_general((pc / den).astype(BF16), vc3, pv, preferred_element_type=F32))
        lse = jnp.broadcast_to(mx + jnp.log(den), (nb, blk, HEAD_DIM))
        for b in range(nb):
            dst = _stream_rows(b // dil, b % dil, dil)
            o_scr[g, dst, :] = o[b]
            l_scr[g, dst, :] = lse[b]
    ls = [l_scr[g] for g in range(n)]
    mx = functools.reduce(jnp.maximum, ls)
    es = [jnp.exp(l - mx) for l in ls]
    tot = functools.reduce(jnp.add, es)
    acc = o_scr[0] * (es[0] / tot)
    for g in range(1, n):
        acc = acc + o_scr[g] * (es[g] / tot)
    o_ref[...] = acc.astype(o_ref.dtype)


def dilated_attention(qs, ks, vs, bsz, dils):
    t, d = qs[0].shape
    kvd = ks[0].shape[1]
    rep = d // kvd
    seq = t // bsz
    assert seq % ATTN_SPAN == 0
    ns = seq // ATTN_SPAN
    cur = lambda b, c, h: (b * ns + c, h // rep)
    in_specs = [pl.BlockSpec((ATTN_SPAN, HEAD_DIM), lambda b, c, h: (b * ns + c, h))] * len(dils)
    args = list(qs)
    for g, dil in enumerate(dils):
        span = ATTN_BLOCK * dil
        per = ATTN_SPAN // span
        prev = lambda b, c, h, per=per: (jnp.maximum((b * ns + c) * per - 1, 0), h // rep)
        in_specs += [pl.BlockSpec((span, HEAD_DIM), prev), pl.BlockSpec((ATTN_SPAN, HEAD_DIM), cur)] * 2
        args += [ks[g], ks[g], vs[g], vs[g]]
    return pl.pallas_call(
        functools.partial(_attn_kernel, dils=dils),
        grid=(bsz, ns, d // HEAD_DIM),
        in_specs=in_specs,
        out_specs=pl.BlockSpec((ATTN_SPAN, HEAD_DIM), lambda b, c, h: (b * ns + c, h)),
        out_shape=jax.ShapeDtypeStruct((t, d), BF16),
        scratch_shapes=[pltpu.VMEM((len(dils), ATTN_SPAN, HEAD_DIM), F32)] * 2,
        compiler_params=_params("parallel", "arbitrary", "arbitrary"), name="dilated_attention",
    )(*args)


def _router_kernel(h_ref, wr_ref, bias_ref, ek_ref, pk_ref, gk_ref, cnt_ref, run_scr, *, n_exp):
    tm = h_ref.shape[0]
    per = n_exp // N_EXPERT_GROUPS
    ng = N_EXPERT_GROUPS

    @pl.when(pl.program_id(0) == 0)
    def _():
        run_scr[...] = jnp.zeros_like(run_scr)

    logits = lax.dot_general(wr_ref[...], h_ref[...], (((1,), (1,)), ((), ())), preferred_element_type=F32)
    scores = jax.nn.sigmoid(logits)
    biased = scores + bias_ref[...]
    sc = [scores[w * ng:(w + 1) * ng] for w in range(per)]
    bi = [biased[w * ng:(w + 1) * ng] for w in range(per)]
    assert per == 4
    hi01, lo01 = jnp.maximum(bi[0], bi[1]), jnp.minimum(bi[0], bi[1])
    hi23, lo23 = jnp.maximum(bi[2], bi[3]), jnp.minimum(bi[2], bi[3])
    group_score = jnp.maximum(hi01, hi23) + jnp.maximum(jnp.minimum(hi01, hi23), jnp.maximum(lo01, lo23))
    gidx = lax.broadcasted_iota(I32, (ng, tm), 0)
    grank = jnp.zeros((ng, tm), I32)
    for g2 in range(ng):
        row = group_score[g2:g2 + 1]
        grank += ((row > group_score) | ((row == group_score) & (g2 < gidx))).astype(I32)
    gsel = grank < TOPK_GROUPS
    cand = [jnp.where(gsel, b, -jnp.inf) for b in bi]
    eid = [gidx * per + w for w in range(per)]
    erank = [jnp.zeros((ng, tm), I32) for _ in range(per)]
    for w2 in range(per):
        for g2 in range(ng):
            row = cand[w2][g2:g2 + 1]
            e2 = g2 * per + w2
            for w in range(per):
                erank[w] += ((row > cand[w]) | ((row == cand[w]) & (e2 < eid[w]))).astype(I32)
    sel = [r < TOP_K for r in erank]
    ssum = functools.reduce(jnp.add, [jnp.sum(jnp.where(m, s, 0.0), axis=0, keepdims=True) for m, s in zip(sel, sc)])
    gate = [jnp.where(m, s / ssum * ROUTED_SCALE, 0.0) for m, s in zip(sel, sc)]
    sel_all = jnp.concatenate([m.astype(F32) for m in sel], axis=0)
    earlier = (lax.broadcasted_iota(I32, (tm, tm), 0) < lax.broadcasted_iota(I32, (tm, tm), 1)).astype(BF16)
    pos = jnp.dot(sel_all.astype(BF16), earlier, preferred_element_type=F32) + run_scr[...]
    run_scr[...] += jnp.sum(sel_all, axis=1, keepdims=True)
    cnt_ref[...] = run_scr[...].astype(I32)
    above = (lax.broadcasted_iota(I32, (n_exp, n_exp), 1) < lax.broadcasted_iota(I32, (n_exp, n_exp), 0)).astype(BF16)
    kidx = jnp.dot(above, sel_all.astype(BF16), preferred_element_type=F32)
    eid_all = jnp.concatenate(eid, axis=0).astype(F32)
    gate_all = jnp.concatenate(gate, axis=0)
    for k in range(TOP_K):
        mk = (sel_all > 0.0) & (kidx == float(k))
        ek_ref[k:k + 1, :] = jnp.sum(jnp.where(mk, eid_all, 0.0), axis=0, keepdims=True).astype(I32)
        pk_ref[k:k + 1, :] = jnp.sum(jnp.where(mk, pos, 0.0), axis=0, keepdims=True).astype(I32)
        gk_ref[k:k + 1, :] = jnp.sum(jnp.where(mk, gate_all, 0.0), axis=0, keepdims=True)


def moe_route(h, router_w, router_bias):
    t, d = h.shape
    n_exp = router_w.shape[1]
    per = n_exp // N_EXPERT_GROUPS
    wr = router_w.T.reshape(N_EXPERT_GROUPS, per, d).transpose(1, 0, 2).reshape(n_exp, d).astype(BF16)
    bias = router_bias.astype(F32).reshape(N_EXPERT_GROUPS, per).T.reshape(n_exp, 1)
    tm = ROUTER_TILE
    kspec = pl.BlockSpec((TOP_K, tm), lambda i: (0, i))
    ek, pk, gk, cnt = pl.pallas_call(
        functools.partial(_router_kernel, n_exp=n_exp), grid=(t // tm,),
        in_specs=[pl.BlockSpec((tm, d), lambda i: (i, 0)), pl.BlockSpec((n_exp, d), lambda i: (0, 0)),
                  pl.BlockSpec((n_exp, 1), lambda i: (0, 0))],
        out_specs=[kspec, kspec, kspec, pl.BlockSpec((n_exp, 1), lambda i: (0, 0))],
        out_shape=[jax.ShapeDtypeStruct((TOP_K, t), I32), jax.ShapeDtypeStruct((TOP_K, t), I32),
                   jax.ShapeDtypeStruct((TOP_K, t), F32), jax.ShapeDtypeStruct((n_exp, 1), I32)],
        scratch_shapes=[pltpu.VMEM((n_exp, 1), F32)],
        compiler_params=_params("arbitrary"), name="moe_router",
    )(h, wr, bias)
    counts = cnt.reshape(per, N_EXPERT_GROUPS).T.reshape(n_exp)
    return ek, pk, gk, counts


def _row_copy(src_ref, src_row, dst_ref, dst_row, sem):
    return pltpu.make_async_copy(src_ref.at[pl.ds(src_row, 1)], dst_ref.at[pl.ds(dst_row, 1)], sem)


def _rows_done(src_ref, dst_ref, sem):
    pltpu.make_async_copy(src_ref.at[pl.ds(0, dst_ref.shape[0])], dst_ref, sem).wait()


def _slot_token_kernel(slot_ref, pad_lo_ref, pad_hi_ref, tok_ref, *, n_tok, steps):
    j = pl.program_id(0)

    @pl.when(j == 0)
    def _():
        for e in range(pad_lo_ref.shape[0]):
            def clear(s, _):
                tok_ref[s] = 0
                return 0
            lax.fori_loop(pad_lo_ref[e], pad_hi_ref[e], clear, 0)

    per = n_tok // steps
    for k in range(TOP_K):
        def put(r, _, k=k):
            tok = j * per + r
            tok_ref[slot_ref[k * n_tok + tok]] = tok
            return 0
        lax.fori_loop(0, per, put, 0, unroll=8)


def moe_slot_tokens(slots, pad_lo, pad_hi, n_slots):
    n_tok = slots.shape[0] // TOP_K
    steps = 64
    assert n_tok % steps == 0
    return pl.pallas_call(
        functools.partial(_slot_token_kernel, n_tok=n_tok, steps=steps),
        grid_spec=pltpu.PrefetchScalarGridSpec(
            num_scalar_prefetch=3, grid=(steps,), in_specs=[],
            out_specs=pl.BlockSpec(memory_space=pltpu.SMEM)),
        out_shape=jax.ShapeDtypeStruct((n_slots,), I32),
        compiler_params=_params("arbitrary"), name="moe_slot_tokens",
    )(slots, pad_lo, pad_hi)


def _expert_kernel(te_ref, nused_ref, tok_ref, h_ref, wg_ref, wu_ref, wd_ref, ys_ref, xbuf0, xbuf1, sem, *, te):
    i = pl.program_id(0)
    n_used = nused_ref[0]
    bufs = (xbuf0, xbuf1)

    @pl.when(i == 0)
    def _():
        def issue(r, _):
            _row_copy(h_ref, tok_ref[r], xbuf0, r, sem.at[0]).start()
            return 0
        lax.fori_loop(0, te, issue, 0, unroll=8)

    for parity in range(2):
        cur, nxt = bufs[parity], bufs[1 - parity]

        def mlp(fetch_next, cur=cur, nxt=nxt, parity=parity):
            _rows_done(h_ref, cur, sem.at[parity])
            if fetch_next:
                for r in range(te):
                    _row_copy(h_ref, tok_ref[(i + 1) * te + r], nxt, r, sem.at[1 - parity]).start()
            x = cur[...].astype(BF16)
            g = jnp.dot(x, wg_ref[...], preferred_element_type=F32)
            u = jnp.dot(x, wu_ref[...], preferred_element_type=F32)
            a = (jax.nn.silu(g) * u).astype(BF16)
            ys_ref[...] = jnp.dot(a, wd_ref[...], preferred_element_type=F32)

        pl.when((i % 2 == parity) & (i + 1 < n_used))(functools.partial(mlp, True))
        pl.when((i % 2 == parity) & (i + 1 == n_used))(functools.partial(mlp, False))

    @pl.when(i >= n_used)
    def _():
        ys_ref[...] = jnp.zeros_like(ys_ref)


def moe_experts(h32, slot_tok, tile_expert, n_used, wg, wu, wd):
    t, d = h32.shape
    n_slots = slot_tok.shape[0]
    de = wg.shape[2]
    te = EXPERT_TILE
    return pl.pallas_call(
        functools.partial(_expert_kernel, te=te),
        grid_spec=pltpu.PrefetchScalarGridSpec(
            num_scalar_prefetch=3, grid=(n_slots // te,),
            in_specs=[pl.BlockSpec(memory_space=pl.ANY),
                      pl.BlockSpec((None, d, de), lambda i, e, n, tk: (e[i], 0, 0)),
                      pl.BlockSpec((None, d, de), lambda i, e, n, tk: (e[i], 0, 0)),
                      pl.BlockSpec((None, de, d), lambda i, e, n, tk: (e[i], 0, 0))],
            out_specs=pl.BlockSpec((te, d), lambda i, e, n, tk: (i, 0)),
            scratch_shapes=[pltpu.VMEM((te, d), F32), pltpu.VMEM((te, d), F32), pltpu.SemaphoreType.DMA((2,))]),
        out_shape=jax.ShapeDtypeStruct((n_slots, d), F32),
        compiler_params=_params("arbitrary"), name="moe_experts",
    )(tile_expert, n_used, slot_tok, h32, wg, wu, wd)


def _combine_kernel(slot_ref, gk_ref, sh_ref, x_ref, gate_ref, ng_ref, ys_ref, o_ref, buf, sem, *, tm, n_tok):
    i = pl.program_id(0)

    def gather(tile, b):
        for k in range(TOP_K):
            def issue(r, _, k=k):
                _row_copy(ys_ref, slot_ref[k * n_tok + tile * tm + r], buf.at[b, k], r, sem.at[b]).start()
                return 0
            lax.fori_loop(0, tm, issue, 0, unroll=8)

    @pl.when(i == 0)
    def _():
        gather(0, 0)

    @pl.when(i + 1 < pl.num_programs(0))
    def _():
        gather(i + 1, (i + 1) % 2)

    b = i % 2
    for k in range(TOP_K):
        _rows_done(ys_ref, buf.at[b, k], sem.at[b])
    y = sh_ref[...].astype(F32)
    gk = gk_ref[...]
    for k in range(TOP_K):
        y = y + gk[:, k:k + 1] * buf[b, k]
    yn = y * lax.rsqrt(jnp.mean(y * y, axis=-1, keepdims=True) + NORM_EPS) * ng_ref[...]
    o_ref[...] = x_ref[...] + gate_ref[...] * yn


def moe_combine(x, ys, slots, gk_t, shared, norm_g, gate):
    bsz, seq, d = x.shape
    tm = COMBINE_TILE
    nt = seq // tm
    return pl.pallas_call(
        functools.partial(_combine_kernel, tm=tm, n_tok=bsz * seq),
        grid_spec=pltpu.PrefetchScalarGridSpec(
            num_scalar_prefetch=1, grid=(bsz * nt,),
            in_specs=[pl.BlockSpec((tm, TOP_K), lambda i, s: (i, 0)),
                      pl.BlockSpec((tm, d), lambda i, s: (i, 0)),
                      pl.BlockSpec((None, tm, d), lambda i, s: (i // nt, i % nt, 0)),
                      pl.BlockSpec((None, 1, d), lambda i, s: (i // nt, 0, 0)),
                      pl.BlockSpec((1, d), lambda i, s: (0, 0)),
                      pl.BlockSpec(memory_space=pl.ANY)],
            out_specs=pl.BlockSpec((None, tm, d), lambda i, s: (i // nt, i % nt, 0)),
            scratch_shapes=[pltpu.VMEM((2, TOP_K, tm, d), F32), pltpu.SemaphoreType.DMA((2,))]),
        out_shape=jax.ShapeDtypeStruct(x.shape, F32),
        compiler_params=_params("arbitrary"), name="moe_combine",
    )(slots, gk_t, shared, x, gate.reshape(bsz, 1, d), norm_g.reshape(1, d), ys)


def moe_layer(x, h, h32, norm_g, gate, router_w, router_bias, e_gate, e_up, e_down, s_gate, s_up, s_down):
    t, d = h.shape
    n_exp = router_w.shape[1]
    te = EXPERT_TILE
    ek, pk, gk, counts = moe_route(h, router_w, router_bias)
    tiles = (counts + te - 1) // te
    tile_end = jnp.cumsum(tiles)
    offsets = (tile_end - tiles) * te
    experts = jnp.arange(n_exp, dtype=I32)
    slot = pk + jnp.sum(jnp.where(ek[..., None] == experts, offsets, 0), axis=-1)
    slots = slot.reshape(-1).astype(I32)
    n_tiles = (t * TOP_K) // te + n_exp
    n_used = tile_end[-1].astype(I32)
    tile_ids = jnp.arange(n_tiles, dtype=I32)
    tile_expert = jnp.sum((tile_end[None, :] <= jnp.minimum(tile_ids, n_used - 1)[:, None]).astype(I32), axis=1)
    pad_lo = jnp.concatenate([offsets + counts, tile_end[-1:] * te]).astype(I32)
    pad_hi = jnp.concatenate([tile_end * te, jnp.full((1,), n_tiles * te, tile_end.dtype)]).astype(I32)
    slot_tok = moe_slot_tokens(slots, pad_lo, pad_hi, n_tiles * te)
    ys = moe_experts(h32, slot_tok, tile_expert, n_used.reshape(1), e_gate, e_up, e_down)
    act = matmul_swiglu(h, s_gate, s_up, tm=1024, tn=512, name="shared_up")
    shared = matmul(act, s_down, BF16, tm=1024, tn=1024, name="shared_down")
    return moe_combine(x, ys, slots, gk.T, shared, norm_g, gate)


def kernel(x, c, positions, ada_w, ada_b, norm_g, ssm_w_in, ssm_a_re, ssm_a_im, ssm_log_step, ssm_b_re, ssm_b_im, ssm_c_re, ssm_c_im, ssm_d, glu_w, glu_b, ssm_w_out, kv_norm_g, w_k, w_v, attn_w_q, attn_w_o, router_w, router_bias, exp_w_gate, exp_w_up, exp_w_down, sh_w_gate, sh_w_up, sh_w_down):
    bsz, seq, d = x.shape
    depth = ada_w.shape[0]
    n_a = ssm_w_in.shape[0]
    t = bsz * seq
    mod = ada_modulation(c, ada_w, ada_b)
    k = v = None
    for layer in range(depth):
        sh1, sc1, g1, sh2, sc2, g2 = jnp.split(mod[layer], N_MOD, axis=-1)
        if layer < n_a:
            a = layer
            (h,) = norm_modulate(x, norm_g[layer, 0], sc1, sh1)
            u_tm = matmul_to_token_major(h, cast_bf16(ssm_w_in, a), bsz, tm=1024, tn=512, name="ssm_in")
            bblk, cblk, ab = s5_operators(ssm_a_re[a], ssm_a_im[a], ssm_log_step[a], ssm_b_re[a], ssm_b_im[a],
                                          ssm_c_re[a], ssm_c_im[a])
            z = s5_scan(u_tm.reshape(seq * bsz, -1), bblk, cblk, ab, ssm_d[a], bsz)
            zz = matmul_glu(z, cast_bf16(glu_w, a), glu_b[a], tm=1024, tn=512, name="ssm_glu")
            y = matmul_from_token_major(zz.reshape(seq, -1), cast_bf16(ssm_w_out, a), bsz, BF16,
                                        tm=1024, tn=512, name="ssm_out")
        else:
            bl = layer - n_a
            n_dil = attn_w_q.shape[2] // d
            assert all(window // dil == ATTN_BLOCK for window, dil in DILATION_GROUPS[:n_dil])
            dils = tuple(dil for _, dil in DILATION_GROUPS[:n_dil])
            if layer == n_a:
                h, hkv = norm_modulate(x, norm_g[layer, 0], sc1, sh1, kv_g=kv_norm_g)
                kvd = w_k.shape[1]
                k_tabs = rope_tables(positions, 1.0)
                k = matmul_streams(hkv, cast_bf16(w_k[None], 0), 0, kvd, dils, k_tabs, tn=min(kvd, 512), name="k_proj")
                v = matmul_streams(hkv, cast_bf16(w_v[None], 0), 0, kvd, dils, None, tn=min(kvd, 512), name="v_proj")
            else:
                (h,) = norm_modulate(x, norm_g[layer, 0], sc1, sh1)
            q_tabs = rope_tables(positions, HEAD_DIM ** -0.5)
            wq = cast_bf16(attn_w_q, bl)
            tnq = min(d, 256)
            qs = [matmul_streams(h, wq, grp * (d // tnq), d, (dil,), q_tabs, tn=tnq, name=f"q_proj_{dil}")[0]
                  for grp, dil in enumerate(dils)]
            o = dilated_attention(qs, k, v, bsz, dils)
            y = matmul(o, cast_bf16(attn_w_o, bl), BF16, tm=1024, tn=512, name="attn_out")
        x = residual_norm(x, y, norm_g[layer, 1], g1)
        h, h32 = norm_modulate(x, norm_g[layer, 2], sc2, sh2, with_f32=True)
        x = moe_layer(x, h, h32, norm_g[layer, 3], g2, router_w[layer], router_bias[layer],
                      cast_bf16(exp_w_gate, layer), cast_bf16(exp_w_up, layer), cast_bf16(exp_w_down, layer),
                      cast_bf16(sh_w_gate, layer), cast_bf16(sh_w_up, layer), cast_bf16(sh_w_down, layer))
    return x
```

```python
import functools
import math

import jax
import jax.numpy as jnp
from jax import lax
from jax.experimental import pallas as pl
from jax.experimental.pallas import tpu as pltpu

F32 = jnp.float32
BF16 = jnp.bfloat16
I32 = jnp.int32

LANE = 128
SUBLANE = 8
VMEM_LIMIT_BYTES = 56 * 1024 * 1024
MXU_DIM = 256

N_MOD = 6
NORM_EPS = 1e-6
HEAD_DIM = 128
ROT_DIM = HEAD_DIM // 4
ROT_HALF = ROT_DIM // 2
ROPE_THETA = 500000.0
DILATION_GROUPS = ((128, 1), (512, 4), (2048, 16))
ATTN_BLOCK = 128
ATTN_SPAN = ATTN_BLOCK * max(dil for _, dil in DILATION_GROUPS)
N_EXPERT_GROUPS = 8
TOPK_GROUPS = 4
TOP_K = 4
ROUTED_SCALE = 2.5
NEG_MASK = -1e30

SSM_PACK_CH = MXU_DIM
SSM_CHUNK_TOKENS = 256
SSM_LANE_CHUNK = 512
EXPERT_TILE = 256
ROW_TILE = 256
COMBINE_TILE = 128
ROUTER_TILE = 512
CAST_BLOCK_BYTES = 4 * 1024 * 1024


def _params(*sem):
    return pltpu.CompilerParams(dimension_semantics=sem, vmem_limit_bytes=VMEM_LIMIT_BYTES)


def _cast_kernel(x_ref, o_ref):
    o_ref[...] = x_ref[...].astype(o_ref.dtype)


def cast_bf16(w, layer):
    lead, c = w.shape[1:-1], w.shape[-1]
    rows = math.prod(lead)
    w3 = w.reshape(w.shape[0], rows, c)
    br = min(rows, 1 << ((CAST_BLOCK_BYTES // (4 * c)).bit_length() - 1))
    assert rows % br == 0 and br % (2 * SUBLANE) == 0
    out = pl.pallas_call(
        _cast_kernel, grid=(rows // br,),
        in_specs=[pl.BlockSpec((None, br, c), lambda i: (layer, i, 0))],
        out_specs=pl.BlockSpec((br, c), lambda i: (i, 0)),
        out_shape=jax.ShapeDtypeStruct((rows, c), BF16),
        compiler_params=_params("parallel"), name="cast_bf16",
    )(w3)
    return out.reshape(*lead, c)


def _ada_kernel(c_ref, w_ref, b_ref, o_ref):
    cond = jax.nn.silu(c_ref[...])
    o_ref[...] = jnp.dot(cond, w_ref[...], precision=lax.Precision.HIGHEST,
                         preferred_element_type=F32) + b_ref[...]


def ada_modulation(c, ada_w, ada_b):
    depth, d, n = ada_w.shape
    bsz = c.shape[0]
    rows = -(-bsz // SUBLANE) * SUBLANE
    cp = jnp.zeros((rows, d), F32).at[:bsz].set(c)
    tn = 512
    out = pl.pallas_call(
        _ada_kernel,
        grid=(depth, n // tn),
        in_specs=[pl.BlockSpec((rows, d), lambda l, j: (0, 0)),
                  pl.BlockSpec((None, d, tn), lambda l, j: (l, 0, j)),
                  pl.BlockSpec((None, 1, tn), lambda l, j: (l, 0, j))],
        out_specs=pl.BlockSpec((None, rows, tn), lambda l, j: (l, 0, j)),
        out_shape=jax.ShapeDtypeStruct((depth, rows, n), F32),
        compiler_params=_params("parallel", "parallel"),
        name="ada_modulation",
    )(cp, ada_w, ada_b.reshape(depth, 1, n))
    return out[:, :bsz]


def _bf16_bits(v):
    return lax.bitcast_convert_type(v.astype(BF16).astype(F32), jnp.uint32)


def _pack_row_halves(v):
    half = v.shape[1] // 2
    return (_bf16_bits(v[:, :half]) >> 16) | _bf16_bits(v[:, half:])


def _unpack_row_halves(p):
    lo = lax.bitcast_convert_type(p << 16, F32)
    hi = lax.bitcast_convert_type(p & jnp.uint32(0xFFFF0000), F32)
    return lo, hi


def _normmod_kernel(x_ref, g_ref, sc_ref, sh_ref, *rest, with_kv, with_rows, with_resid):
    x = x_ref[...]
    rest = list(rest)
    if with_resid:
        y_ref, rg_ref, gate_ref = rest.pop(0), rest.pop(0), rest.pop(0)
        y = y_ref[...].astype(F32)
        x = x + gate_ref[...] * (y * lax.rsqrt(jnp.mean(y * y, axis=-1, keepdims=True) + NORM_EPS) * rg_ref[...])
    xn = x * lax.rsqrt(jnp.mean(x * x, axis=-1, keepdims=True) + NORM_EPS)
    h = xn * g_ref[...] * (1.0 + sc_ref[...]) + sh_ref[...]
    if with_kv:
        kvg_ref = rest.pop(0)
    if with_resid:
        rest.pop(0)[...] = x
    h_ref = rest.pop(0)
    h_ref[...] = h.astype(h_ref.dtype)
    if with_kv:
        rest.pop(0)[...] = (xn * kvg_ref[...]).astype(BF16)
    if with_rows:
        rest.pop(0)[...] = _pack_row_halves(h)


def norm_modulate(x, g, sc, sh, kv_g=None, with_rows=False, resid=None):
    bsz, seq, d = x.shape
    ts = ROW_TILE
    nt = seq // ts
    row = lambda b, i: (b * nt + i, 0)
    vec = pl.BlockSpec((1, d), lambda b, i: (0, 0))
    bvec = pl.BlockSpec((None, 1, d), lambda b, i: (b, 0, 0))
    xspec = pl.BlockSpec((None, ts, d), lambda b, i: (b, i, 0))
    in_specs = [xspec, vec, bvec, bvec]
    args = [x, g.reshape(1, d), sc.reshape(bsz, 1, d), sh.reshape(bsz, 1, d)]
    out_specs = [pl.BlockSpec((ts, d), row)]
    out_shape = [jax.ShapeDtypeStruct((bsz * seq, d), BF16)]
    if resid is not None:
        y, rg, gate = resid
        in_specs += [pl.BlockSpec((ts, d), row), vec, bvec]
        args += [y, rg.reshape(1, d), gate.reshape(bsz, 1, d)]
        out_specs.insert(0, xspec)
        out_shape.insert(0, jax.ShapeDtypeStruct(x.shape, F32))
    if kv_g is not None:
        in_specs.append(vec)
        args.append(kv_g.reshape(1, d))
        out_specs.append(pl.BlockSpec((ts, d), row))
        out_shape.append(jax.ShapeDtypeStruct((bsz * seq, d), BF16))
    if with_rows:
        out_specs.append(pl.BlockSpec((ts, d // 2), row))
        out_shape.append(jax.ShapeDtypeStruct((bsz * seq, d // 2), jnp.uint32))
    return pl.pallas_call(
        functools.partial(_normmod_kernel, with_kv=kv_g is not None, with_rows=with_rows,
                          with_resid=resid is not None),
        grid=(bsz, nt), in_specs=in_specs, out_specs=out_specs, out_shape=out_shape,
        compiler_params=_params("parallel", "parallel"), name="norm_modulate",
    )(*args)


def _mm_kernel(a_ref, w_ref, o_ref):
    o_ref[...] = jnp.dot(a_ref[...], w_ref[...], preferred_element_type=F32).astype(o_ref.dtype)


def _mm_glu_kernel(a_ref, w_ref, z_ref, b_ref, o_ref):
    acc = jnp.dot(a_ref[...], w_ref[...], preferred_element_type=F32) + b_ref[...]
    o_ref[...] = (z_ref[...].astype(F32) * jax.nn.sigmoid(acc)).astype(o_ref.dtype)


def _mm_swiglu_kernel(a_ref, wg_ref, wu_ref, o_ref):
    a = a_ref[...]
    g = jnp.dot(a, wg_ref[...], preferred_element_type=F32)
    u = jnp.dot(a, wu_ref[...], preferred_element_type=F32)
    o_ref[...] = (jax.nn.silu(g) * u).astype(o_ref.dtype)


def matmul(a, w, out_dtype, *, tm, tn, name):
    m, k = a.shape
    n = w.shape[1]
    tm, tn = min(tm, m), min(tn, n)
    return pl.pallas_call(
        _mm_kernel, grid=(m // tm, n // tn),
        in_specs=[pl.BlockSpec((tm, k), lambda i, j: (i, 0)), pl.BlockSpec((k, tn), lambda i, j: (0, j))],
        out_specs=pl.BlockSpec((tm, tn), lambda i, j: (i, j)),
        out_shape=jax.ShapeDtypeStruct((m, n), out_dtype),
        compiler_params=_params("parallel", "arbitrary"), name=name,
    )(a, w)


def matmul_swiglu(a, wg, wu, *, tm, tn, name):
    m, k = a.shape
    n = wg.shape[1]
    tm, tn = min(tm, m), min(tn, n)
    wspec = pl.BlockSpec((k, tn), lambda i, j: (0, j))
    return pl.pallas_call(
        _mm_swiglu_kernel, grid=(m // tm, n // tn),
        in_specs=[pl.BlockSpec((tm, k), lambda i, j: (i, 0)), wspec, wspec],
        out_specs=pl.BlockSpec((tm, tn), lambda i, j: (i, j)),
        out_shape=jax.ShapeDtypeStruct((m, n), BF16),
        compiler_params=_params("parallel", "arbitrary"), name=name,
    )(a, wg, wu)


def matmul_glu(z, w, b, *, tm, tn, name):
    m, k = z.shape
    n = w.shape[1]
    tm, tn = min(tm, m), min(tn, n)
    return pl.pallas_call(
        _mm_glu_kernel, grid=(m // tm, n // tn),
        in_specs=[pl.BlockSpec((tm, k), lambda i, j: (i, 0)), pl.BlockSpec((k, tn), lambda i, j: (0, j)),
                  pl.BlockSpec((tm, tn), lambda i, j: (i, j)), pl.BlockSpec((1, tn), lambda i, j: (0, j))],
        out_specs=pl.BlockSpec((tm, tn), lambda i, j: (i, j)),
        out_shape=jax.ShapeDtypeStruct((m, n), BF16),
        compiler_params=_params("parallel", "arbitrary"), name=name,
    )(z, w, z, b.reshape(1, n))


def matmul_to_token_major(a, w, bsz, *, tm, tn, name):
    m, k = a.shape
    n = w.shape[1]
    seq = m // bsz
    tm, tn = min(tm, seq), min(tn, n)
    nt, nj = seq // tm, n // tn
    return pl.pallas_call(
        _mm_kernel, grid=(bsz, nt, nj),
        in_specs=[pl.BlockSpec((tm, k), lambda b, i, j: (b * nt + i, 0)),
                  pl.BlockSpec((k, tn), lambda b, i, j: (0, j))],
        out_specs=pl.BlockSpec((tm, tn), lambda b, i, j: (i, b * nj + j)),
        out_shape=jax.ShapeDtypeStruct((seq, bsz * n), F32),
        compiler_params=_params("parallel", "parallel", "arbitrary"), name=name,
    )(a, w)


def matmul_from_token_major(a, w, bsz, out_dtype, *, tm, tn, name):
    seq = a.shape[0]
    k, n = w.shape
    tm, tn = min(tm, seq), min(tn, n)
    nt, nj = seq // tm, n // tn
    return pl.pallas_call(
        _mm_kernel, grid=(bsz, nt, nj),
        in_specs=[pl.BlockSpec((tm, k), lambda b, i, j: (i, b)),
                  pl.BlockSpec((k, tn), lambda b, i, j: (0, j))],
        out_specs=pl.BlockSpec((tm, tn), lambda b, i, j: (b * nt + i, j)),
        out_shape=jax.ShapeDtypeStruct((bsz * seq, n), out_dtype),
        compiler_params=_params("parallel", "parallel", "arbitrary"), name=name,
    )(a, w)


def _s5_discretize_kernel(are_ref, aim_ref, ls_ref, abr_ref, abi_ref, fr_ref, fi_ref):
    lr, li = are_ref[...], aim_ref[...]
    step = jnp.exp(ls_ref[...])
    decay = jnp.exp(lr * step)
    abar_re = decay * jnp.cos(li * step)
    abar_im = decay * jnp.sin(li * step)
    denom = lr * lr + li * li
    nr = abar_re - 1.0
    ni = abar_im
    abr_ref[...] = abar_re
    abi_ref[...] = abar_im
    fr_ref[...] = (nr * lr + ni * li) / denom
    fi_ref[...] = (ni * lr - nr * li) / denom


def s5_discretize(a_re, a_im, log_step):
    g, p = a_re.shape
    shp = jax.ShapeDtypeStruct((g, p), F32)
    return pl.pallas_call(_s5_discretize_kernel, out_shape=[shp] * 4, name="s5_discretize")(
        a_re, a_im, log_step.reshape(g, 1))


def _s5_scan_kernel(u_ref, bblk_ref, cblk_ref, ab_ref, d_ref, z_ref, x_scr, xb_scr, carry_scr, *, rows, nstate, bsz):
    @pl.when(pl.program_id(1) == 0)
    def _():
        carry_scr[...] = jnp.zeros_like(carry_scr)

    u = u_ref[...]
    x_scr[...] = jnp.dot(u.astype(BF16), bblk_ref[...], preferred_element_type=F32)
    lw = SSM_LANE_CHUNK
    second = lax.broadcasted_iota(I32, (SUBLANE, lw), 0) >= bsz
    for lt in range(nstate // lw):
        re = slice(lt * lw, (lt + 1) * lw)
        im = slice(nstate + lt * lw, nstate + (lt + 1) * lw)
        ar, ai = ab_ref[0:1, re], ab_ref[1:2, re]
        a2r, a2i = ar * ar - ai * ai, 2.0 * ar * ai
        amr, ami = jnp.where(second, ar, 0.0), jnp.where(second, ai, 0.0)
        pr, pi = jnp.where(second, a2r, ar), jnp.where(second, a2i, ai)

        def advance(r0, cr, ci, re=re, im=im, amr=amr, ami=ami, pr=pr, pi=pi):
            bur, bui = x_scr[pl.ds(r0, SUBLANE), re], x_scr[pl.ds(r0, SUBLANE), im]
            sr, si = pltpu.roll(bur, bsz, 0), pltpu.roll(bui, bsz, 0)
            wr = bur + amr * sr - ami * si
            wi = bui + amr * si + ami * sr
            cbr = jnp.where(second, cr, pltpu.roll(cr, bsz, 0))
            cbi = jnp.where(second, ci, pltpu.roll(ci, bsz, 0))
            return wr + pr * cbr - pi * cbi, wi + pr * cbi + pi * cbr

        def body(n, carry, re=re, im=im, advance=advance):
            r0 = pl.multiple_of(n * (2 * SUBLANE), 2 * SUBLANE)
            xr0, xi0 = advance(r0, *carry)
            xr1, xi1 = advance(r0 + SUBLANE, xr0, xi0)
            xb_scr[pl.ds(r0, 2 * SUBLANE), re] = jnp.concatenate([xr0, xr1], axis=0).astype(BF16)
            xb_scr[pl.ds(r0, 2 * SUBLANE), im] = jnp.concatenate([xi0, xi1], axis=0).astype(BF16)
            return xr1, xi1

        cr, ci = lax.fori_loop(0, rows // (2 * SUBLANE), body, (carry_scr[:, re], carry_scr[:, im]))
        carry_scr[:, re] = cr
        carry_scr[:, im] = ci
    y = jnp.dot(xb_scr[...], cblk_ref[...], preferred_element_type=F32) + d_ref[...] * u
    z_ref[...] = jax.nn.gelu(y).astype(z_ref.dtype)


def s5_scan(u_tm, bblk, cblk, ab, d_skip, bsz):
    m, width = u_tm.shape
    assert 2 * bsz == SUBLANE, "the scan tiles two tokens of every batch element into one sublane tile"
    packs, pack_ch, two_n = bblk.shape
    nstate = two_n // 2
    rows = SSM_CHUNK_TOKENS * bsz
    return pl.pallas_call(
        functools.partial(_s5_scan_kernel, rows=rows, nstate=nstate, bsz=bsz),
        grid=(packs, m // rows),
        in_specs=[pl.BlockSpec((rows, pack_ch), lambda p, c: (c, p)),
                  pl.BlockSpec((None, pack_ch, two_n), lambda p, c: (p, 0, 0)),
                  pl.BlockSpec((None, two_n, pack_ch), lambda p, c: (p, 0, 0)),
                  pl.BlockSpec((None, 2, nstate), lambda p, c: (p, 0, 0)),
                  pl.BlockSpec((None, 1, pack_ch), lambda p, c: (p, 0, 0))],
        out_specs=pl.BlockSpec((rows, pack_ch), lambda p, c: (c, p)),
        out_shape=jax.ShapeDtypeStruct((m, width), BF16),
        scratch_shapes=[pltpu.VMEM((rows, two_n), F32), pltpu.VMEM((rows, two_n), BF16),
                        pltpu.VMEM((SUBLANE, two_n), F32)],
        compiler_params=_params("parallel", "arbitrary"), name="s5_scan",
    )(u_tm, bblk, cblk, ab, d_skip.reshape(packs, 1, pack_ch))


def s5_operators(a_re, a_im, log_step, b_re, b_im, c_re, c_im):
    g, p = a_re.shape
    gsz = b_re.shape[-1]
    gpp = SSM_PACK_CH // gsz
    packs = g // gpp
    abr, abi, fr, fi = s5_discretize(a_re, a_im, log_step)
    bbar_re = fr[..., None] * b_re - fi[..., None] * b_im
    bbar_im = fr[..., None] * b_im + fi[..., None] * b_re
    eye = jnp.eye(gpp, dtype=F32)

    def b_block(t):
        t = t.reshape(packs, gpp, p, gsz).transpose(0, 1, 3, 2)
        return (t[:, :, :, None, :] * eye[None, :, None, :, None]).reshape(packs, gpp * gsz, gpp * p)

    def c_block(t):
        t = t.reshape(packs, gpp, gsz, p).transpose(0, 1, 3, 2)
        return (t[:, :, :, None, :] * eye[None, :, None, :, None]).reshape(packs, gpp * p, gpp * gsz)

    bblk = jnp.concatenate([b_block(bbar_re), b_block(bbar_im)], axis=2).astype(BF16)
    cblk = jnp.concatenate([c_block(c_re), -c_block(c_im)], axis=1).astype(BF16)
    ab = jnp.stack([abr.reshape(packs, gpp * p), abi.reshape(packs, gpp * p)], axis=1)
    return bblk, cblk, ab


def _rope_table_kernel(pos_ref, invf_ref, cos_ref, sin_lo_ref, sin_hi_ref, *, scale):
    ang = pos_ref[...].astype(F32) * invf_ref[...]
    lane = lax.broadcasted_iota(I32, ang.shape, 1)
    c, s = jnp.cos(ang), jnp.sin(ang)
    cos_ref[...] = jnp.where(lane < ROT_DIM, c, 1.0) * scale
    sin_lo_ref[...] = jnp.where(lane < ROT_HALF, -s, 0.0) * scale
    sin_hi_ref[...] = jnp.where((lane >= ROT_HALF) & (lane < ROT_DIM), s, 0.0) * scale


def rope_tables(positions, scale):
    t = positions.size
    inv_freq = ROPE_THETA ** (-jnp.arange(ROT_HALF, dtype=F32) * 2.0 / ROT_DIM)
    invf = jnp.zeros((1, HEAD_DIM), F32).at[0, :ROT_DIM].set(jnp.concatenate([inv_freq, inv_freq]))
    ts = 1024
    shp = jax.ShapeDtypeStruct((t, HEAD_DIM), F32)
    spec = pl.BlockSpec((ts, HEAD_DIM), lambda i: (i, 0))
    return pl.pallas_call(
        functools.partial(_rope_table_kernel, scale=scale), grid=(t // ts,),
        in_specs=[pl.BlockSpec((ts, 1), lambda i: (i, 0)), pl.BlockSpec((1, HEAD_DIM), lambda i: (0, 0))],
        out_specs=[spec] * 3, out_shape=[shp] * 3,
        compiler_params=_params("parallel"), name="rope_tables",
    )(positions.reshape(t, 1), invf)


def _stream_rows(chunk, residue, dil):
    start = chunk * (ATTN_BLOCK * dil) + residue
    return pl.ds(start, ATTN_BLOCK) if dil == 1 else pl.ds(start, ATTN_BLOCK, stride=dil)


def _stream_major_store(o_ref, src_ref, dil):
    span = ATTN_BLOCK * dil
    for hd in range(src_ref.shape[0]):
        cols = slice(hd * HEAD_DIM, (hd + 1) * HEAD_DIM)
        for chunk in range(src_ref.shape[1] // span):
            for residue in range(dil):
                dst = pl.ds(chunk * span + residue * ATTN_BLOCK, ATTN_BLOCK)
                o_ref[dst, cols] = src_ref[hd, _stream_rows(chunk, residue, dil), :].astype(o_ref.dtype)


def _mm_streams_kernel(a_ref, w_ref, *rest, dils, rotary):
    rest = list(rest)
    if rotary:
        cos, sin_lo, sin_hi = [rest.pop(0)[...] for _ in range(3)]
    o_refs, scr = rest[:len(dils)], rest[len(dils)]
    acc = jnp.dot(a_ref[...], w_ref[...], preferred_element_type=F32)
    for hd in range(acc.shape[1] // HEAD_DIM):
        t = acc[:, hd * HEAD_DIM:(hd + 1) * HEAD_DIM]
        if rotary:
            t = t * cos + pltpu.roll(t, HEAD_DIM - ROT_HALF, 1) * sin_lo + pltpu.roll(t, ROT_HALF, 1) * sin_hi
        scr[hd] = t
    for o_ref, dil in zip(o_refs, dils):
        _stream_major_store(o_ref, scr, dil)


def matmul_streams(a, w, col_block0, n, dils, tables, *, tn, name):
    m, k = a.shape
    tm = ATTN_SPAN
    rotary = tables is not None
    in_specs = [pl.BlockSpec((tm, k), lambda i, j: (i, 0), pipeline_mode=pl.Buffered(1)),
                pl.BlockSpec((k, tn), lambda i, j: (0, col_block0 + j))]
    args = [a, w]
    if rotary:
        in_specs += [pl.BlockSpec((tm, HEAD_DIM), lambda i, j: (i, 0))] * 3
        args += list(tables)
    return pl.pallas_call(
        functools.partial(_mm_streams_kernel, dils=dils, rotary=rotary), grid=(m // tm, n // tn),
        in_specs=in_specs,
        out_specs=[pl.BlockSpec((tm, tn), lambda i, j: (i, j))] * len(dils),
        out_shape=[jax.ShapeDtypeStruct((m, n), BF16)] * len(dils),
        scratch_shapes=[pltpu.VMEM((tn // HEAD_DIM, tm, HEAD_DIM), F32)],
        compiler_params=_params("parallel", "arbitrary"), name=name,
    )(*args)


def _attn_kernel(*refs, dils):
    n = len(dils)
    q_refs = refs[:n]
    kv_refs = refs[n:5 * n]
    o_ref, o_scr, l_scr = refs[5 * n], refs[5 * n + 1], refs[5 * n + 2]
    blk = ATTN_BLOCK
    nb = ATTN_SPAN // blk
    seq_start = pl.program_id(1) == 0
    qi = lax.broadcasted_iota(I32, (nb, blk, blk), 1)
    kj = lax.broadcasted_iota(I32, (nb, blk, blk), 2)
    bi = lax.broadcasted_iota(I32, (nb, blk, blk), 0)
    cur_ok = kj <= qi
    qk = (((2,), (2,)), ((0,), (0,)))
    pv = (((2,), (1,)), ((0,), (0,)))
    for g, dil in enumerate(dils):
        kp_ref, kc_ref, vp_ref, vc_ref = kv_refs[4 * g:4 * g + 4]
        span = blk * dil
        kc, vc = kc_ref[...], vc_ref[...]
        if span == ATTN_SPAN:
            kp, vp = kp_ref[...], vp_ref[...]
        else:
            kp = jnp.concatenate([kp_ref[...], kc[:ATTN_SPAN - span]], axis=0)
            vp = jnp.concatenate([vp_ref[...], vc[:ATTN_SPAN - span]], axis=0)
        split = lambda t: t.reshape(nb, blk, HEAD_DIM)
        q3, kp3, kc3, vp3, vc3 = split(q_refs[g][...]), split(kp), split(kc), split(vp), split(vc)
        prev_ok = (kj >= qi) & jnp.logical_not(seq_start & (bi < dil))
        sp = jnp.where(prev_ok, lax.dot_general(q3, kp3, qk, preferred_element_type=F32), NEG_MASK)
        sc = jnp.where(cur_ok, lax.dot_general(q3, kc3, qk, preferred_element_type=F32), NEG_MASK)
        mx = jnp.maximum(jnp.max(sp, axis=-1, keepdims=True), jnp.max(sc, axis=-1, keepdims=True))
        pp, pc = jnp.exp(sp - mx), jnp.exp(sc - mx)
        den = jnp.sum(pp, axis=-1, keepdims=True) + jnp.sum(pc, axis=-1, keepdims=True)
        o = (lax.dot_general((pp / den).astype(BF16), vp3, pv, preferred_element_type=F32)
             + lax.dot_general((pc / den).astype(BF16), vc3, pv, preferred_element_type=F32))
        lse = jnp.broadcast_to(mx + jnp.log(den), (nb, blk, HEAD_DIM))
        for b in range(nb):
            dst = _stream_rows(b // dil, b % dil, dil)
            o_scr[g, dst, :] = o[b]
            l_scr[g, dst, :] = lse[b]
    ls = [l_scr[g] for g in range(n)]
    mx = functools.reduce(jnp.maximum, ls)
    es = [jnp.exp(l - mx) for l in ls]
    tot = functools.reduce(jnp.add, es)
    acc = o_scr[0] * (es[0] / tot)
    for g in range(1, n):
        acc = acc + o_scr[g] * (es[g] / tot)
    o_ref[...] = acc.astype(o_ref.dtype)


def dilated_attention(qs, ks, vs, bsz, dils):
    t, d = qs[0].shape
    kvd = ks[0].shape[1]
    rep = d // kvd
    seq = t // bsz
    assert seq % ATTN_SPAN == 0
    ns = seq // ATTN_SPAN
    cur = lambda b, c, h: (b * ns + c, h // rep)
    in_specs = [pl.BlockSpec((ATTN_SPAN, HEAD_DIM), lambda b, c, h: (b * ns + c, h))] * len(dils)
    args = list(qs)
    for g, dil in enumerate(dils):
        span = ATTN_BLOCK * dil
        per = ATTN_SPAN // span
        prev = lambda b, c, h, per=per: (jnp.maximum((b * ns + c) * per - 1, 0), h // rep)
        in_specs += [pl.BlockSpec((span, HEAD_DIM), prev), pl.BlockSpec((ATTN_SPAN, HEAD_DIM), cur)] * 2
        args += [ks[g], ks[g], vs[g], vs[g]]
    return pl.pallas_call(
        functools.partial(_attn_kernel, dils=dils),
        grid=(bsz, ns, d // HEAD_DIM),
        in_specs=in_specs,
        out_specs=pl.BlockSpec((ATTN_SPAN, HEAD_DIM), lambda b, c, h: (b * ns + c, h)),
        out_shape=jax.ShapeDtypeStruct((t, d), BF16),
        scratch_shapes=[pltpu.VMEM((len(dils), ATTN_SPAN, HEAD_DIM), F32)] * 2,
        compiler_params=_params("parallel", "arbitrary", "arbitrary"), name="dilated_attention",
    )(*args)


def _router_kernel(h_ref, wr_ref, bias_ref, ek_ref, pk_ref, gk_ref, cnt_ref, run_scr, *, n_exp):
    tm = h_ref.shape[0]
    per = n_exp // N_EXPERT_GROUPS
    ng = N_EXPERT_GROUPS

    @pl.when(pl.program_id(0) == 0)
    def _():
        run_scr[...] = jnp.zeros_like(run_scr)

    logits = lax.dot_general(wr_ref[...], h_ref[...], (((1,), (1,)), ((), ())), preferred_element_type=F32)
    scores = jax.nn.sigmoid(logits)
    biased = scores + bias_ref[...]
    sc = [scores[w * ng:(w + 1) * ng] for w in range(per)]
    bi = [biased[w * ng:(w + 1) * ng] for w in range(per)]
    assert per == 4
    hi01, lo01 = jnp.maximum(bi[0], bi[1]), jnp.minimum(bi[0], bi[1])
    hi23, lo23 = jnp.maximum(bi[2], bi[3]), jnp.minimum(bi[2], bi[3])
    group_score = jnp.maximum(hi01, hi23) + jnp.maximum(jnp.minimum(hi01, hi23), jnp.maximum(lo01, lo23))
    gidx = lax.broadcasted_iota(I32, (ng, tm), 0)
    grank = jnp.zeros((ng, tm), I32)
    for g2 in range(ng):
        row = group_score[g2:g2 + 1]
        grank += ((row > group_score) | ((row == group_score) & (g2 < gidx))).astype(I32)
    gsel = grank < TOPK_GROUPS
    cand = [jnp.where(gsel, b, -jnp.inf) for b in bi]
    eid = [gidx * per + w for w in range(per)]
    erank = [jnp.zeros((ng, tm), I32) for _ in range(per)]
    for w2 in range(per):
        for g2 in range(ng):
            row = cand[w2][g2:g2 + 1]
            e2 = g2 * per + w2
            for w in range(per):
                erank[w] += ((row > cand[w]) | ((row == cand[w]) & (e2 < eid[w]))).astype(I32)
    sel = [r < TOP_K for r in erank]
    ssum = functools.reduce(jnp.add, [jnp.sum(jnp.where(m, s, 0.0), axis=0, keepdims=True) for m, s in zip(sel, sc)])
    gate = [jnp.where(m, s / ssum * ROUTED_SCALE, 0.0) for m, s in zip(sel, sc)]
    sel_all = jnp.concatenate([m.astype(F32) for m in sel], axis=0)
    earlier = (lax.broadcasted_iota(I32, (tm, tm), 0) < lax.broadcasted_iota(I32, (tm, tm), 1)).astype(BF16)
    pos = jnp.dot(sel_all.astype(BF16), earlier, preferred_element_type=F32) + run_scr[...]
    run_scr[...] += jnp.sum(sel_all, axis=1, keepdims=True)
    cnt_ref[...] = run_scr[...].astype(I32)
    above = (lax.broadcasted_iota(I32, (n_exp, n_exp), 1) < lax.broadcasted_iota(I32, (n_exp, n_exp), 0)).astype(BF16)
    kidx = jnp.dot(above, sel_all.astype(BF16), preferred_element_type=F32)
    eid_all = jnp.concatenate(eid, axis=0).astype(F32)
    gate_all = jnp.concatenate(gate, axis=0)
    for k in range(TOP_K):
        mk = (sel_all > 0.0) & (kidx == float(k))
        ek_ref[k:k + 1, :] = jnp.sum(jnp.where(mk, eid_all, 0.0), axis=0, keepdims=True).astype(I32)
        pk_ref[k:k + 1, :] = jnp.sum(jnp.where(mk, pos, 0.0), axis=0, keepdims=True).astype(I32)
        gk_ref[k:k + 1, :] = jnp.sum(jnp.where(mk, gate_all, 0.0), axis=0, keepdims=True)


def moe_route(h, router_w, router_bias):
    t, d = h.shape
    n_exp = router_w.shape[1]
    per = n_exp // N_EXPERT_GROUPS
    wr = router_w.T.reshape(N_EXPERT_GROUPS, per, d).transpose(1, 0, 2).reshape(n_exp, d).astype(BF16)
    bias = router_bias.astype(F32).reshape(N_EXPERT_GROUPS, per).T.reshape(n_exp, 1)
    tm = ROUTER_TILE
    kspec = pl.BlockSpec((TOP_K, tm), lambda i: (0, i))
    ek, pk, gk, cnt = pl.pallas_call(
        functools.partial(_router_kernel, n_exp=n_exp), grid=(t // tm,),
        in_specs=[pl.BlockSpec((tm, d), lambda i: (i, 0)), pl.BlockSpec((n_exp, d), lambda i: (0, 0)),
                  pl.BlockSpec((n_exp, 1), lambda i: (0, 0))],
        out_specs=[kspec, kspec, kspec, pl.BlockSpec((n_exp, 1), lambda i: (0, 0))],
        out_shape=[jax.ShapeDtypeStruct((TOP_K, t), I32), jax.ShapeDtypeStruct((TOP_K, t), I32),
                   jax.ShapeDtypeStruct((TOP_K, t), F32), jax.ShapeDtypeStruct((n_exp, 1), I32)],
        scratch_shapes=[pltpu.VMEM((n_exp, 1), F32)],
        compiler_params=_params("arbitrary"), name="moe_router",
    )(h, wr, bias)
    counts = cnt.reshape(per, N_EXPERT_GROUPS).T.reshape(n_exp)
    return ek, pk, gk, counts


def _row_copy(src_ref, src_row, dst_ref, dst_row, sem):
    return pltpu.make_async_copy(src_ref.at[pl.ds(src_row, 1)], dst_ref.at[pl.ds(dst_row, 1)], sem)


def _rows_done(src_ref, dst_ref, sem):
    pltpu.make_async_copy(src_ref.at[pl.ds(0, dst_ref.shape[0])], dst_ref, sem).wait()


def _slot_token_kernel(slot_ref, pad_lo_ref, pad_hi_ref, tok_ref, *, n_tok, steps):
    j = pl.program_id(0)

    @pl.when(j == 0)
    def _():
        for e in range(pad_lo_ref.shape[0]):
            def clear(s, _):
                tok_ref[s] = 0
                return 0
            lax.fori_loop(pad_lo_ref[e], pad_hi_ref[e], clear, 0)

    per = n_tok // steps
    for k in range(TOP_K):
        def put(r, _, k=k):
            tok = j * per + r
            tok_ref[slot_ref[k * n_tok + tok]] = tok
            return 0
        lax.fori_loop(0, per, put, 0, unroll=8)


def moe_slot_tokens(slots, pad_lo, pad_hi, n_slots):
    n_tok = slots.shape[0] // TOP_K
    steps = 64
    assert n_tok % steps == 0
    return pl.pallas_call(
        functools.partial(_slot_token_kernel, n_tok=n_tok, steps=steps),
        grid_spec=pltpu.PrefetchScalarGridSpec(
            num_scalar_prefetch=3, grid=(steps,), in_specs=[],
            out_specs=pl.BlockSpec(memory_space=pltpu.SMEM)),
        out_shape=jax.ShapeDtypeStruct((n_slots,), I32),
        compiler_params=_params("arbitrary"), name="moe_slot_tokens",
    )(slots, pad_lo, pad_hi)


def _expert_kernel(te_ref, nused_ref, tok_ref, h_ref, wg_ref, wu_ref, wd_ref, ys_ref, xbuf0, xbuf1, sem, *, te):
    i = pl.program_id(0)
    n_used = nused_ref[0]
    bufs = (xbuf0, xbuf1)

    @pl.when(i == 0)
    def _():
        def issue(r, _):
            _row_copy(h_ref, tok_ref[r], xbuf0, r, sem.at[0]).start()
            return 0
        lax.fori_loop(0, te, issue, 0, unroll=8)

    for parity in range(2):
        cur, nxt = bufs[parity], bufs[1 - parity]

        def mlp(fetch_next, cur=cur, nxt=nxt, parity=parity):
            _rows_done(h_ref, cur, sem.at[parity])
            if fetch_next:
                for r in range(te):
                    _row_copy(h_ref, tok_ref[(i + 1) * te + r], nxt, r, sem.at[1 - parity]).start()
            x_lo, x_hi = [t.astype(BF16) for t in _unpack_row_halves(cur[...])]
            half = x_lo.shape[1]
            lo, hi = pl.ds(0, half), pl.ds(half, half)
            g = (jnp.dot(x_lo, wg_ref[lo, :], preferred_element_type=F32)
                 + jnp.dot(x_hi, wg_ref[hi, :], preferred_element_type=F32))
            u = (jnp.dot(x_lo, wu_ref[lo, :], preferred_element_type=F32)
                 + jnp.dot(x_hi, wu_ref[hi, :], preferred_element_type=F32))
            a = (jax.nn.silu(g) * u).astype(BF16)
            ys_ref[...] = _pack_row_halves(jnp.dot(a, wd_ref[...], preferred_element_type=F32))

        pl.when((i % 2 == parity) & (i + 1 < n_used))(functools.partial(mlp, True))
        pl.when((i % 2 == parity) & (i + 1 == n_used))(functools.partial(mlp, False))

    @pl.when(i >= n_used)
    def _():
        ys_ref[...] = jnp.zeros_like(ys_ref)


def moe_experts(h_rows, slot_tok, tile_expert, n_used, wg, wu, wd):
    dh = h_rows.shape[1]
    d = 2 * dh
    n_slots = slot_tok.shape[0]
    de = wg.shape[2]
    te = EXPERT_TILE
    return pl.pallas_call(
        functools.partial(_expert_kernel, te=te),
        grid_spec=pltpu.PrefetchScalarGridSpec(
            num_scalar_prefetch=3, grid=(n_slots // te,),
            in_specs=[pl.BlockSpec(memory_space=pl.ANY),
                      pl.BlockSpec((None, d, de), lambda i, e, n, tk: (e[i], 0, 0)),
                      pl.BlockSpec((None, d, de), lambda i, e, n, tk: (e[i], 0, 0)),
                      pl.BlockSpec((None, de, d), lambda i, e, n, tk: (e[i], 0, 0))],
            out_specs=pl.BlockSpec((te, dh), lambda i, e, n, tk: (i, 0)),
            scratch_shapes=[pltpu.VMEM((te, dh), jnp.uint32), pltpu.VMEM((te, dh), jnp.uint32),
                            pltpu.SemaphoreType.DMA((2,))]),
        out_shape=jax.ShapeDtypeStruct((n_slots, dh), jnp.uint32),
        compiler_params=_params("arbitrary"), name="moe_experts",
    )(tile_expert, n_used, slot_tok, h_rows, wg, wu, wd)


def _combine_kernel(slot_ref, gk_ref, sh_ref, x_ref, gate_ref, ng_ref, ys_ref, o_ref, buf, sem, *, tm, n_tok):
    i = pl.program_id(0)

    def gather(tile, b):
        for k in range(TOP_K):
            def issue(r, _, k=k):
                _row_copy(ys_ref, slot_ref[k * n_tok + tile * tm + r], buf.at[b, k], r, sem.at[b]).start()
                return 0
            lax.fori_loop(0, tm, issue, 0, unroll=8)

    @pl.when(i == 0)
    def _():
        gather(0, 0)

    @pl.when(i + 1 < pl.num_programs(0))
    def _():
        gather(i + 1, (i + 1) % 2)

    b = i % 2
    for k in range(TOP_K):
        _rows_done(ys_ref, buf.at[b, k], sem.at[b])
    half = buf.shape[-1]
    lo, hi = pl.ds(0, half), pl.ds(half, half)
    y_lo, y_hi = sh_ref[:, lo].astype(F32), sh_ref[:, hi].astype(F32)
    gk = gk_ref[...]
    for k in range(TOP_K):
        e_lo, e_hi = _unpack_row_halves(buf[b, k])
        y_lo = y_lo + gk[:, k:k + 1] * e_lo
        y_hi = y_hi + gk[:, k:k + 1] * e_hi
    ssq = jnp.sum(y_lo * y_lo, axis=-1, keepdims=True) + jnp.sum(y_hi * y_hi, axis=-1, keepdims=True)
    inv = lax.rsqrt(ssq / (2 * half) + NORM_EPS)
    o_ref[:, lo] = x_ref[:, lo] + gate_ref[:, lo] * (y_lo * inv * ng_ref[:, lo])
    o_ref[:, hi] = x_ref[:, hi] + gate_ref[:, hi] * (y_hi * inv * ng_ref[:, hi])


def moe_combine(x, ys, slots, gk_t, shared, norm_g, gate):
    bsz, seq, d = x.shape
    tm = COMBINE_TILE
    nt = seq // tm
    return pl.pallas_call(
        functools.partial(_combine_kernel, tm=tm, n_tok=bsz * seq),
        grid_spec=pltpu.PrefetchScalarGridSpec(
            num_scalar_prefetch=1, grid=(bsz * nt,),
            in_specs=[pl.BlockSpec((tm, TOP_K), lambda i, s: (i, 0)),
                      pl.BlockSpec((tm, d), lambda i, s: (i, 0)),
                      pl.BlockSpec((None, tm, d), lambda i, s: (i // nt, i % nt, 0)),
                      pl.BlockSpec((None, 1, d), lambda i, s: (i // nt, 0, 0)),
                      pl.BlockSpec((1, d), lambda i, s: (0, 0)),
                      pl.BlockSpec(memory_space=pl.ANY)],
            out_specs=pl.BlockSpec((None, tm, d), lambda i, s: (i // nt, i % nt, 0)),
            scratch_shapes=[pltpu.VMEM((2, TOP_K, tm, d // 2), jnp.uint32), pltpu.SemaphoreType.DMA((2,))]),
        out_shape=jax.ShapeDtypeStruct(x.shape, F32),
        compiler_params=_params("arbitrary"), name="moe_combine",
    )(slots, gk_t, shared, x, gate.reshape(bsz, 1, d), norm_g.reshape(1, d), ys)


def moe_layer(x, h, h_rows, norm_g, gate, router_w, router_bias, e_gate, e_up, e_down, s_gate, s_up, s_down):
    t, d = h.shape
    n_exp = router_w.shape[1]
    te = EXPERT_TILE
    ek, pk, gk, counts = moe_route(h, router_w, router_bias)
    tiles = (counts + te - 1) // te
    tile_end = jnp.cumsum(tiles)
    offsets = (tile_end - tiles) * te
    experts = jnp.arange(n_exp, dtype=I32)
    slot = pk + jnp.sum(jnp.where(ek[..., None] == experts, offsets, 0), axis=-1)
    slots = slot.reshape(-1).astype(I32)
    n_tiles = (t * TOP_K) // te + n_exp
    n_used = tile_end[-1].astype(I32)
    tile_ids = jnp.arange(n_tiles, dtype=I32)
    tile_expert = jnp.sum((tile_end[None, :] <= jnp.minimum(tile_ids, n_used - 1)[:, None]).astype(I32), axis=1)
    pad_lo = jnp.concatenate([offsets + counts, tile_end[-1:] * te]).astype(I32)
    pad_hi = jnp.concatenate([tile_end * te, jnp.full((1,), n_tiles * te, tile_end.dtype)]).astype(I32)
    slot_tok = moe_slot_tokens(slots, pad_lo, pad_hi, n_tiles * te)
    ys = moe_experts(h_rows, slot_tok, tile_expert, n_used.reshape(1), e_gate, e_up, e_down)
    act = matmul_swiglu(h, s_gate, s_up, tm=1024, tn=512, name="shared_up")
    shared = matmul(act, s_down, BF16, tm=1024, tn=1024, name="shared_down")
    return moe_combine(x, ys, slots, gk.T, shared, norm_g, gate)


def kernel(x, c, positions, ada_w, ada_b, norm_g, ssm_w_in, ssm_a_re, ssm_a_im, ssm_log_step, ssm_b_re, ssm_b_im, ssm_c_re, ssm_c_im, ssm_d, glu_w, glu_b, ssm_w_out, kv_norm_g, w_k, w_v, attn_w_q, attn_w_o, router_w, router_bias, exp_w_gate, exp_w_up, exp_w_down, sh_w_gate, sh_w_up, sh_w_down):
    bsz, seq, d = x.shape
    depth = ada_w.shape[0]
    n_a = ssm_w_in.shape[0]
    t = bsz * seq
    mod = ada_modulation(c, ada_w, ada_b)
    k = v = None
    for layer in range(depth):
        sh1, sc1, g1, sh2, sc2, g2 = jnp.split(mod[layer], N_MOD, axis=-1)
        if layer < n_a:
            a = layer
            (h,) = norm_modulate(x, norm_g[layer, 0], sc1, sh1)
            u_tm = matmul_to_token_major(h, cast_bf16(ssm_w_in, a), bsz, tm=1024, tn=512, name="ssm_in")
            bblk, cblk, ab = s5_operators(ssm_a_re[a], ssm_a_im[a], ssm_log_step[a], ssm_b_re[a], ssm_b_im[a],
                                          ssm_c_re[a], ssm_c_im[a])
            z = s5_scan(u_tm.reshape(seq * bsz, -1), bblk, cblk, ab, ssm_d[a], bsz)
            zz = matmul_glu(z, cast_bf16(glu_w, a), glu_b[a], tm=1024, tn=512, name="ssm_glu")
            y = matmul_from_token_major(zz.reshape(seq, -1), cast_bf16(ssm_w_out, a), bsz, BF16,
                                        tm=1024, tn=512, name="ssm_out")
        else:
            bl = layer - n_a
            n_dil = attn_w_q.shape[2] // d
            assert all(window // dil == ATTN_BLOCK for window, dil in DILATION_GROUPS[:n_dil])
            dils = tuple(dil for _, dil in DILATION_GROUPS[:n_dil])
            if layer == n_a:
                h, hkv = norm_modulate(x, norm_g[layer, 0], sc1, sh1, kv_g=kv_norm_g)
                kvd = w_k.shape[1]
                k_tabs = rope_tables(positions, 1.0)
                k = matmul_streams(hkv, cast_bf16(w_k[None], 0), 0, kvd, dils, k_tabs, tn=min(kvd, 512), name="k_proj")
                v = matmul_streams(hkv, cast_bf16(w_v[None], 0), 0, kvd, dils, None, tn=min(kvd, 512), name="v_proj")
            else:
                (h,) = norm_modulate(x, norm_g[layer, 0], sc1, sh1)
            q_tabs = rope_tables(positions, HEAD_DIM ** -0.5)
            wq = cast_bf16(attn_w_q, bl)
            tnq = min(d, 256)
            qs = [matmul_streams(h, wq, grp * (d // tnq), d, (dil,), q_tabs, tn=tnq, name=f"q_proj_{dil}")[0]
                  for grp, dil in enumerate(dils)]
            o = dilated_attention(qs, k, v, bsz, dils)
            y = matmul(o, cast_bf16(attn_w_o, bl), BF16, tm=1024, tn=512, name="attn_out")
        x, h, h_rows = norm_modulate(x, norm_g[layer, 2], sc2, sh2, with_rows=True,
                                     resid=(y, norm_g[layer, 1], g1))
        x = moe_layer(x, h, h_rows, norm_g[layer, 3], g2, router_w[layer], router_bias[layer],
                      cast_bf16(exp_w_gate, layer), cast_bf16(exp_w_up, layer), cast_bf16(exp_w_down, layer),
                      cast_bf16(sh_w_gate, layer), cast_bf16(sh_w_up, layer), cast_bf16(sh_w_down, layer))
    return x
```

```python
import functools
import math

import jax
import jax.numpy as jnp
from jax import lax
from jax.experimental import pallas as pl
from jax.experimental.pallas import tpu as pltpu

F32 = jnp.float32
BF16 = jnp.bfloat16
I32 = jnp.int32

LANE = 128
SUBLANE = 8
VMEM_LIMIT_BYTES = 56 * 1024 * 1024
MXU_DIM = 256

N_MOD = 6
NORM_EPS = 1e-6
HEAD_DIM = 128
ROT_DIM = HEAD_DIM // 4
ROT_HALF = ROT_DIM // 2
ROPE_THETA = 500000.0
DILATION_GROUPS = ((128, 1), (512, 4), (2048, 16))
ATTN_BLOCK = 128
ATTN_SPAN = ATTN_BLOCK * max(dil for _, dil in DILATION_GROUPS)
N_EXPERT_GROUPS = 8
TOPK_GROUPS = 4
TOP_K = 4
ROUTED_SCALE = 2.5
NEG_MASK = -1e30

SSM_PACK_CH = MXU_DIM
SSM_CHUNK_TOKENS = 256
SSM_LANE_CHUNK = 512
EXPERT_TILE = 256
ROW_TILE = 256
COMBINE_TILE = 128
GATHER_AHEAD = 2
ROUTER_TILE = 512
CAST_BLOCK_BYTES = 4 * 1024 * 1024


def _params(*sem):
    return pltpu.CompilerParams(dimension_semantics=sem, vmem_limit_bytes=VMEM_LIMIT_BYTES)


def _cast_kernel(x_ref, o_ref):
    o_ref[...] = x_ref[...].astype(o_ref.dtype)


def cast_bf16(w, layer):
    lead, c = w.shape[1:-1], w.shape[-1]
    rows = math.prod(lead)
    w3 = w.reshape(w.shape[0], rows, c)
    br = min(rows, 1 << ((CAST_BLOCK_BYTES // (4 * c)).bit_length() - 1))
    assert rows % br == 0 and br % (2 * SUBLANE) == 0
    out = pl.pallas_call(
        _cast_kernel, grid=(rows // br,),
        in_specs=[pl.BlockSpec((None, br, c), lambda i: (layer, i, 0))],
        out_specs=pl.BlockSpec((br, c), lambda i: (i, 0)),
        out_shape=jax.ShapeDtypeStruct((rows, c), BF16),
        compiler_params=_params("parallel"), name="cast_bf16",
    )(w3)
    return out.reshape(*lead, c)


def _ada_kernel(c_ref, w_ref, b_ref, o_ref):
    cond = jax.nn.silu(c_ref[...])
    o_ref[...] = jnp.dot(cond, w_ref[...], precision=lax.Precision.HIGHEST,
                         preferred_element_type=F32) + b_ref[...]


def ada_modulation(c, ada_w, ada_b):
    depth, d, n = ada_w.shape
    bsz = c.shape[0]
    rows = -(-bsz // SUBLANE) * SUBLANE
    cp = jnp.zeros((rows, d), F32).at[:bsz].set(c)
    tn = 512
    out = pl.pallas_call(
        _ada_kernel,
        grid=(depth, n // tn),
        in_specs=[pl.BlockSpec((rows, d), lambda l, j: (0, 0)),
                  pl.BlockSpec((None, d, tn), lambda l, j: (l, 0, j)),
                  pl.BlockSpec((None, 1, tn), lambda l, j: (l, 0, j))],
        out_specs=pl.BlockSpec((None, rows, tn), lambda l, j: (l, 0, j)),
        out_shape=jax.ShapeDtypeStruct((depth, rows, n), F32),
        compiler_params=_params("parallel", "parallel"),
        name="ada_modulation",
    )(cp, ada_w, ada_b.reshape(depth, 1, n))
    return out[:, :bsz]


def _bf16_bits(v):
    return lax.bitcast_convert_type(v.astype(BF16).astype(F32), jnp.uint32)


def _pack_row_halves(v):
    half = v.shape[1] // 2
    return (_bf16_bits(v[:, :half]) >> 16) | _bf16_bits(v[:, half:])


def _unpack_row_halves(p):
    lo = lax.bitcast_convert_type(p << 16, F32)
    hi = lax.bitcast_convert_type(p & jnp.uint32(0xFFFF0000), F32)
    return lo, hi


def _normmod_kernel(x_ref, g_ref, sc_ref, sh_ref, *rest, with_kv, with_rows, with_resid):
    x = x_ref[...]
    rest = list(rest)
    if with_resid:
        y_ref, rg_ref, gate_ref = rest.pop(0), rest.pop(0), rest.pop(0)
        y = y_ref[...].astype(F32)
        x = x + gate_ref[...] * (y * lax.rsqrt(jnp.mean(y * y, axis=-1, keepdims=True) + NORM_EPS) * rg_ref[...])
    xn = x * lax.rsqrt(jnp.mean(x * x, axis=-1, keepdims=True) + NORM_EPS)
    h = xn * g_ref[...] * (1.0 + sc_ref[...]) + sh_ref[...]
    if with_kv:
        kvg_ref = rest.pop(0)
    if with_resid:
        rest.pop(0)[...] = x
    h_ref = rest.pop(0)
    h_ref[...] = h.astype(h_ref.dtype)
    if with_kv:
        rest.pop(0)[...] = (xn * kvg_ref[...]).astype(BF16)
    if with_rows:
        rest.pop(0)[...] = _pack_row_halves(h)


def norm_modulate(x, g, sc, sh, kv_g=None, with_rows=False, resid=None):
    bsz, seq, d = x.shape
    ts = ROW_TILE
    nt = seq // ts
    row = lambda b, i: (b * nt + i, 0)
    vec = pl.BlockSpec((1, d), lambda b, i: (0, 0))
    bvec = pl.BlockSpec((None, 1, d), lambda b, i: (b, 0, 0))
    xspec = pl.BlockSpec((None, ts, d), lambda b, i: (b, i, 0))
    in_specs = [xspec, vec, bvec, bvec]
    args = [x, g.reshape(1, d), sc.reshape(bsz, 1, d), sh.reshape(bsz, 1, d)]
    out_specs = [pl.BlockSpec((ts, d), row)]
    out_shape = [jax.ShapeDtypeStruct((bsz * seq, d), BF16)]
    if resid is not None:
        y, rg, gate = resid
        in_specs += [pl.BlockSpec((ts, d), row), vec, bvec]
        args += [y, rg.reshape(1, d), gate.reshape(bsz, 1, d)]
        out_specs.insert(0, xspec)
        out_shape.insert(0, jax.ShapeDtypeStruct(x.shape, F32))
    if kv_g is not None:
        in_specs.append(vec)
        args.append(kv_g.reshape(1, d))
        out_specs.append(pl.BlockSpec((ts, d), row))
        out_shape.append(jax.ShapeDtypeStruct((bsz * seq, d), BF16))
    if with_rows:
        out_specs.append(pl.BlockSpec((ts, d // 2), row))
        out_shape.append(jax.ShapeDtypeStruct((bsz * seq, d // 2), jnp.uint32))
    return pl.pallas_call(
        functools.partial(_normmod_kernel, with_kv=kv_g is not None, with_rows=with_rows,
                          with_resid=resid is not None),
        grid=(bsz, nt), in_specs=in_specs, out_specs=out_specs, out_shape=out_shape,
        compiler_params=_params("parallel", "parallel"), name="norm_modulate",
    )(*args)


def _mm_kernel(a_ref, w_ref, o_ref):
    o_ref[...] = jnp.dot(a_ref[...], w_ref[...], preferred_element_type=F32).astype(o_ref.dtype)


def _mm_glu_kernel(a_ref, w_ref, z_ref, b_ref, o_ref):
    acc = jnp.dot(a_ref[...], w_ref[...], preferred_element_type=F32) + b_ref[...]
    o_ref[...] = (z_ref[...].astype(F32) * jax.nn.sigmoid(acc)).astype(o_ref.dtype)


def _mm_swiglu_kernel(a_ref, wg_ref, wu_ref, o_ref):
    a = a_ref[...]
    g = jnp.dot(a, wg_ref[...], preferred_element_type=F32)
    u = jnp.dot(a, wu_ref[...], preferred_element_type=F32)
    o_ref[...] = (jax.nn.silu(g) * u).astype(o_ref.dtype)


def matmul(a, w, out_dtype, *, tm, tn, name):
    m, k = a.shape
    n = w.shape[1]
    tm, tn = min(tm, m), min(tn, n)
    return pl.pallas_call(
        _mm_kernel, grid=(m // tm, n // tn),
        in_specs=[pl.BlockSpec((tm, k), lambda i, j: (i, 0)), pl.BlockSpec((k, tn), lambda i, j: (0, j))],
        out_specs=pl.BlockSpec((tm, tn), lambda i, j: (i, j)),
        out_shape=jax.ShapeDtypeStruct((m, n), out_dtype),
        compiler_params=_params("parallel", "arbitrary"), name=name,
    )(a, w)


def matmul_swiglu(a, wg, wu, *, tm, tn, name):
    m, k = a.shape
    n = wg.shape[1]
    tm, tn = min(tm, m), min(tn, n)
    wspec = pl.BlockSpec((k, tn), lambda i, j: (0, j))
    return pl.pallas_call(
        _mm_swiglu_kernel, grid=(m // tm, n // tn),
        in_specs=[pl.BlockSpec((tm, k), lambda i, j: (i, 0)), wspec, wspec],
        out_specs=pl.BlockSpec((tm, tn), lambda i, j: (i, j)),
        out_shape=jax.ShapeDtypeStruct((m, n), BF16),
        compiler_params=_params("parallel", "arbitrary"), name=name,
    )(a, wg, wu)


def matmul_glu(z, w, b, *, tm, tn, name):
    m, k = z.shape
    n = w.shape[1]
    tm, tn = min(tm, m), min(tn, n)
    return pl.pallas_call(
        _mm_glu_kernel, grid=(m // tm, n // tn),
        in_specs=[pl.BlockSpec((tm, k), lambda i, j: (i, 0)), pl.BlockSpec((k, tn), lambda i, j: (0, j)),
                  pl.BlockSpec((tm, tn), lambda i, j: (i, j)), pl.BlockSpec((1, tn), lambda i, j: (0, j))],
        out_specs=pl.BlockSpec((tm, tn), lambda i, j: (i, j)),
        out_shape=jax.ShapeDtypeStruct((m, n), BF16),
        compiler_params=_params("parallel", "arbitrary"), name=name,
    )(z, w, z, b.reshape(1, n))


def matmul_to_token_major(a, w, bsz, *, tm, tn, name):
    m, k = a.shape
    n = w.shape[1]
    seq = m // bsz
    tm, tn = min(tm, seq), min(tn, n)
    nt, nj = seq // tm, n // tn
    return pl.pallas_call(
        _mm_kernel, grid=(bsz, nt, nj),
        in_specs=[pl.BlockSpec((tm, k), lambda b, i, j: (b * nt + i, 0)),
                  pl.BlockSpec((k, tn), lambda b, i, j: (0, j))],
        out_specs=pl.BlockSpec((tm, tn), lambda b, i, j: (i, b * nj + j)),
        out_shape=jax.ShapeDtypeStruct((seq, bsz * n), F32),
        compiler_params=_params("parallel", "parallel", "arbitrary"), name=name,
    )(a, w)


def matmul_from_token_major(a, w, bsz, out_dtype, *, tm, tn, name):
    seq = a.shape[0]
    k, n = w.shape
    tm, tn = min(tm, seq), min(tn, n)
    nt, nj = seq // tm, n // tn
    return pl.pallas_call(
        _mm_kernel, grid=(bsz, nt, nj),
        in_specs=[pl.BlockSpec((tm, k), lambda b, i, j: (i, b)),
                  pl.BlockSpec((k, tn), lambda b, i, j: (0, j))],
        out_specs=pl.BlockSpec((tm, tn), lambda b, i, j: (b * nt + i, j)),
        out_shape=jax.ShapeDtypeStruct((bsz * seq, n), out_dtype),
        compiler_params=_params("parallel", "parallel", "arbitrary"), name=name,
    )(a, w)


def _s5_discretize_kernel(are_ref, aim_ref, ls_ref, abr_ref, abi_ref, fr_ref, fi_ref):
    lr, li = are_ref[...], aim_ref[...]
    step = jnp.exp(ls_ref[...])
    decay = jnp.exp(lr * step)
    abar_re = decay * jnp.cos(li * step)
    abar_im = decay * jnp.sin(li * step)
    denom = lr * lr + li * li
    nr = abar_re - 1.0
    ni = abar_im
    abr_ref[...] = abar_re
    abi_ref[...] = abar_im
    fr_ref[...] = (nr * lr + ni * li) / denom
    fi_ref[...] = (ni * lr - nr * li) / denom


def s5_discretize(a_re, a_im, log_step):
    g, p = a_re.shape
    shp = jax.ShapeDtypeStruct((g, p), F32)
    return pl.pallas_call(_s5_discretize_kernel, out_shape=[shp] * 4, name="s5_discretize")(
        a_re, a_im, log_step.reshape(g, 1))


def _s5_scan_kernel(u_ref, bblk_ref, cblk_ref, ab_ref, d_ref, z_ref, x_scr, xb_scr, carry_scr, *, rows, nstate, bsz):
    @pl.when(pl.program_id(1) == 0)
    def _():
        carry_scr[...] = jnp.zeros_like(carry_scr)

    u = u_ref[...]
    x_scr[...] = jnp.dot(u.astype(BF16), bblk_ref[...], preferred_element_type=F32)
    lw = SSM_LANE_CHUNK
    second = lax.broadcasted_iota(I32, (SUBLANE, lw), 0) >= bsz
    for lt in range(nstate // lw):
        re = slice(lt * lw, (lt + 1) * lw)
        im = slice(nstate + lt * lw, nstate + (lt + 1) * lw)
        ar, ai = ab_ref[0:1, re], ab_ref[1:2, re]
        a2r, a2i = ar * ar - ai * ai, 2.0 * ar * ai
        amr, ami = jnp.where(second, ar, 0.0), jnp.where(second, ai, 0.0)
        pr, pi = jnp.where(second, a2r, ar), jnp.where(second, a2i, ai)

        def advance(r0, cr, ci, re=re, im=im, amr=amr, ami=ami, pr=pr, pi=pi):
            bur, bui = x_scr[pl.ds(r0, SUBLANE), re], x_scr[pl.ds(r0, SUBLANE), im]
            sr, si = pltpu.roll(bur, bsz, 0), pltpu.roll(bui, bsz, 0)
            wr = bur + amr * sr - ami * si
            wi = bui + amr * si + ami * sr
            cbr = jnp.where(second, cr, pltpu.roll(cr, bsz, 0))
            cbi = jnp.where(second, ci, pltpu.roll(ci, bsz, 0))
            return wr + pr * cbr - pi * cbi, wi + pr * cbi + pi * cbr

        def body(n, carry, re=re, im=im, advance=advance):
            r0 = pl.multiple_of(n * (2 * SUBLANE), 2 * SUBLANE)
            xr0, xi0 = advance(r0, *carry)
            xr1, xi1 = advance(r0 + SUBLANE, xr0, xi0)
            xb_scr[pl.ds(r0, 2 * SUBLANE), re] = jnp.concatenate([xr0, xr1], axis=0).astype(BF16)
            xb_scr[pl.ds(r0, 2 * SUBLANE), im] = jnp.concatenate([xi0, xi1], axis=0).astype(BF16)
            return xr1, xi1

        cr, ci = lax.fori_loop(0, rows // (2 * SUBLANE), body, (carry_scr[:, re], carry_scr[:, im]))
        carry_scr[:, re] = cr
        carry_scr[:, im] = ci
    y = jnp.dot(xb_scr[...], cblk_ref[...], preferred_element_type=F32) + d_ref[...] * u
    z_ref[...] = jax.nn.gelu(y).astype(z_ref.dtype)


def s5_scan(u_tm, bblk, cblk, ab, d_skip, bsz):
    m, width = u_tm.shape
    assert 2 * bsz == SUBLANE, "the scan tiles two tokens of every batch element into one sublane tile"
    packs, pack_ch, two_n = bblk.shape
    nstate = two_n // 2
    rows = SSM_CHUNK_TOKENS * bsz
    return pl.pallas_call(
        functools.partial(_s5_scan_kernel, rows=rows, nstate=nstate, bsz=bsz),
        grid=(packs, m // rows),
        in_specs=[pl.BlockSpec((rows, pack_ch), lambda p, c: (c, p)),
                  pl.BlockSpec((None, pack_ch, two_n), lambda p, c: (p, 0, 0)),
                  pl.BlockSpec((None, two_n, pack_ch), lambda p, c: (p, 0, 0)),
                  pl.BlockSpec((None, 2, nstate), lambda p, c: (p, 0, 0)),
                  pl.BlockSpec((None, 1, pack_ch), lambda p, c: (p, 0, 0))],
        out_specs=pl.BlockSpec((rows, pack_ch), lambda p, c: (c, p)),
        out_shape=jax.ShapeDtypeStruct((m, width), BF16),
        scratch_shapes=[pltpu.VMEM((rows, two_n), F32), pltpu.VMEM((rows, two_n), BF16),
                        pltpu.VMEM((SUBLANE, two_n), F32)],
        compiler_params=_params("parallel", "arbitrary"), name="s5_scan",
    )(u_tm, bblk, cblk, ab, d_skip.reshape(packs, 1, pack_ch))


def s5_operators(a_re, a_im, log_step, b_re, b_im, c_re, c_im):
    g, p = a_re.shape
    gsz = b_re.shape[-1]
    gpp = SSM_PACK_CH // gsz
    packs = g // gpp
    abr, abi, fr, fi = s5_discretize(a_re, a_im, log_step)
    bbar_re = fr[..., None] * b_re - fi[..., None] * b_im
    bbar_im = fr[..., None] * b_im + fi[..., None] * b_re
    eye = jnp.eye(gpp, dtype=F32)

    def b_block(t):
        t = t.reshape(packs, gpp, p, gsz).transpose(0, 1, 3, 2)
        return (t[:, :, :, None, :] * eye[None, :, None, :, None]).reshape(packs, gpp * gsz, gpp * p)

    def c_block(t):
        t = t.reshape(packs, gpp, gsz, p).transpose(0, 1, 3, 2)
        return (t[:, :, :, None, :] * eye[None, :, None, :, None]).reshape(packs, gpp * p, gpp * gsz)

    bblk = jnp.concatenate([b_block(bbar_re), b_block(bbar_im)], axis=2).astype(BF16)
    cblk = jnp.concatenate([c_block(c_re), -c_block(c_im)], axis=1).astype(BF16)
    ab = jnp.stack([abr.reshape(packs, gpp * p), abi.reshape(packs, gpp * p)], axis=1)
    return bblk, cblk, ab


def _rope_table_kernel(pos_ref, invf_ref, cos_ref, sin_lo_ref, sin_hi_ref, *, scale):
    ang = pos_ref[...].astype(F32) * invf_ref[...]
    lane = lax.broadcasted_iota(I32, ang.shape, 1)
    c, s = jnp.cos(ang), jnp.sin(ang)
    cos_ref[...] = jnp.where(lane < ROT_DIM, c, 1.0) * scale
    sin_lo_ref[...] = jnp.where(lane < ROT_HALF, -s, 0.0) * scale
    sin_hi_ref[...] = jnp.where((lane >= ROT_HALF) & (lane < ROT_DIM), s, 0.0) * scale


def rope_tables(positions, scale):
    t = positions.size
    inv_freq = ROPE_THETA ** (-jnp.arange(ROT_HALF, dtype=F32) * 2.0 / ROT_DIM)
    invf = jnp.zeros((1, HEAD_DIM), F32).at[0, :ROT_DIM].set(jnp.concatenate([inv_freq, inv_freq]))
    ts = 1024
    shp = jax.ShapeDtypeStruct((t, HEAD_DIM), F32)
    spec = pl.BlockSpec((ts, HEAD_DIM), lambda i: (i, 0))
    return pl.pallas_call(
        functools.partial(_rope_table_kernel, scale=scale), grid=(t // ts,),
        in_specs=[pl.BlockSpec((ts, 1), lambda i: (i, 0)), pl.BlockSpec((1, HEAD_DIM), lambda i: (0, 0))],
        out_specs=[spec] * 3, out_shape=[shp] * 3,
        compiler_params=_params("parallel"), name="rope_tables",
    )(positions.reshape(t, 1), invf)


def _stream_rows(chunk, residue, dil):
    start = chunk * (ATTN_BLOCK * dil) + residue
    return pl.ds(start, ATTN_BLOCK) if dil == 1 else pl.ds(start, ATTN_BLOCK, stride=dil)


def _stream_major_store(o_ref, src_ref, dil):
    span = ATTN_BLOCK * dil
    for hd in range(src_ref.shape[0]):
        cols = slice(hd * HEAD_DIM, (hd + 1) * HEAD_DIM)
        for chunk in range(src_ref.shape[1] // span):
            for residue in range(dil):
                dst = pl.ds(chunk * span + residue * ATTN_BLOCK, ATTN_BLOCK)
                o_ref[dst, cols] = src_ref[hd, _stream_rows(chunk, residue, dil), :].astype(o_ref.dtype)


def _mm_streams_kernel(a_ref, w_ref, *rest, dils, rotary):
    rest = list(rest)
    if rotary:
        cos, sin_lo, sin_hi = [rest.pop(0)[...] for _ in range(3)]
    o_refs, scr = rest[:len(dils)], rest[len(dils)]
    acc = jnp.dot(a_ref[...], w_ref[...], preferred_element_type=F32)
    for hd in range(acc.shape[1] // HEAD_DIM):
        t = acc[:, hd * HEAD_DIM:(hd + 1) * HEAD_DIM]
        if rotary:
            t = t * cos + pltpu.roll(t, HEAD_DIM - ROT_HALF, 1) * sin_lo + pltpu.roll(t, ROT_HALF, 1) * sin_hi
        scr[hd] = t
    for o_ref, dil in zip(o_refs, dils):
        _stream_major_store(o_ref, scr, dil)


def matmul_streams(a, w, col_block0, n, dils, tables, *, tn, name):
    m, k = a.shape
    tm = ATTN_SPAN
    rotary = tables is not None
    in_specs = [pl.BlockSpec((tm, k), lambda i, j: (i, 0), pipeline_mode=pl.Buffered(1)),
                pl.BlockSpec((k, tn), lambda i, j: (0, col_block0 + j))]
    args = [a, w]
    if rotary:
        in_specs += [pl.BlockSpec((tm, HEAD_DIM), lambda i, j: (i, 0))] * 3
        args += list(tables)
    return pl.pallas_call(
        functools.partial(_mm_streams_kernel, dils=dils, rotary=rotary), grid=(m // tm, n // tn),
        in_specs=in_specs,
        out_specs=[pl.BlockSpec((tm, tn), lambda i, j: (i, j))] * len(dils),
        out_shape=[jax.ShapeDtypeStruct((m, n), BF16)] * len(dils),
        scratch_shapes=[pltpu.VMEM((tn // HEAD_DIM, tm, HEAD_DIM), F32)],
        compiler_params=_params("parallel", "arbitrary"), name=name,
    )(*args)


def _attn_kernel(*refs, dils):
    n = len(dils)
    q_refs = refs[:n]
    kv_refs = refs[n:5 * n]
    o_ref, o_scr, l_scr = refs[5 * n], refs[5 * n + 1], refs[5 * n + 2]
    blk = ATTN_BLOCK
    nb = ATTN_SPAN // blk
    seq_start = pl.program_id(1) == 0
    qi = lax.broadcasted_iota(I32, (nb, blk, blk), 1)
    kj = lax.broadcasted_iota(I32, (nb, blk, blk), 2)
    bi = lax.broadcasted_iota(I32, (nb, blk, blk), 0)
    cur_ok = kj <= qi
    qk = (((2,), (2,)), ((0,), (0,)))
    pv = (((2,), (1,)), ((0,), (0,)))
    for g, dil in enumerate(dils):
        kp_ref, kc_ref, vp_ref, vc_ref = kv_refs[4 * g:4 * g + 4]
        span = blk * dil
        kc, vc = kc_ref[...], vc_ref[...]
        if span == ATTN_SPAN:
            kp, vp = kp_ref[...], vp_ref[...]
        else:
            kp = jnp.concatenate([kp_ref[...], kc[:ATTN_SPAN - span]], axis=0)
            vp = jnp.concatenate([vp_ref[...], vc[:ATTN_SPAN - span]], axis=0)
        split = lambda t: t.reshape(nb, blk, HEAD_DIM)
        q3, kp3, kc3, vp3, vc3 = split(q_refs[g][...]), split(kp), split(kc), split(vp), split(vc)
        prev_ok = (kj >= qi) & jnp.logical_not(seq_start & (bi < dil))
        sp = jnp.where(prev_ok, lax.dot_general(q3, kp3, qk, preferred_element_type=F32), NEG_MASK)
        sc = jnp.where(cur_ok, lax.dot_general(q3, kc3, qk, preferred_element_type=F32), NEG_MASK)
        mx = jnp.maximum(jnp.max(sp, axis=-1, keepdims=True), jnp.max(sc, axis=-1, keepdims=True))
        pp, pc = jnp.exp(sp - mx), jnp.exp(sc - mx)
        den = jnp.sum(pp, axis=-1, keepdims=True) + jnp.sum(pc, axis=-1, keepdims=True)
        o = (lax.dot_general((pp / den).astype(BF16), vp3, pv, preferred_element_type=F32)
             + lax.dot_general((pc / den).astype(BF16), vc3, pv, preferred_element_type=F32))
        lse = jnp.broadcast_to(mx + jnp.log(den), (nb, blk, HEAD_DIM))
        for b in range(nb):
            dst = _stream_rows(b // dil, b % dil, dil)
            o_scr[g, dst, :] = o[b]
            l_scr[g, dst, :] = lse[b]
    ls = [l_scr[g] for g in range(n)]
    mx = functools.reduce(jnp.maximum, ls)
    es = [jnp.exp(l - mx) for l in ls]
    tot = functools.reduce(jnp.add, es)
    acc = o_scr[0] * (es[0] / tot)
    for g in range(1, n):
        acc = acc + o_scr[g] * (es[g] / tot)
    o_ref[...] = acc.astype(o_ref.dtype)


def dilated_attention(qs, ks, vs, bsz, dils):
    t, d = qs[0].shape
    kvd = ks[0].shape[1]
    rep = d // kvd
    seq = t // bsz
    assert seq % ATTN_SPAN == 0
    ns = seq // ATTN_SPAN
    cur = lambda b, c, h: (b * ns + c, h // rep)
    in_specs = [pl.BlockSpec((ATTN_SPAN, HEAD_DIM), lambda b, c, h: (b * ns + c, h))] * len(dils)
    args = list(qs)
    for g, dil in enumerate(dils):
        span = ATTN_BLOCK * dil
        per = ATTN_SPAN // span
        prev = lambda b, c, h, per=per: (jnp.maximum((b * ns + c) * per - 1, 0), h // rep)
        in_specs += [pl.BlockSpec((span, HEAD_DIM), prev), pl.BlockSpec((ATTN_SPAN, HEAD_DIM), cur)] * 2
        args += [ks[g], ks[g], vs[g], vs[g]]
    return pl.pallas_call(
        functools.partial(_attn_kernel, dils=dils),
        grid=(bsz, ns, d // HEAD_DIM),
        in_specs=in_specs,
        out_specs=pl.BlockSpec((ATTN_SPAN, HEAD_DIM), lambda b, c, h: (b * ns + c, h)),
        out_shape=jax.ShapeDtypeStruct((t, d), BF16),
        scratch_shapes=[pltpu.VMEM((len(dils), ATTN_SPAN, HEAD_DIM), F32)] * 2,
        compiler_params=_params("parallel", "arbitrary", "arbitrary"), name="dilated_attention",
    )(*args)


def _router_kernel(h_ref, wr_ref, bias_ref, ek_ref, pk_ref, gk_ref, cnt_ref, run_scr, *, n_exp):
    tm = h_ref.shape[0]
    per = n_exp // N_EXPERT_GROUPS
    ng = N_EXPERT_GROUPS

    @pl.when(pl.program_id(0) == 0)
    def _():
        run_scr[...] = jnp.zeros_like(run_scr)

    logits = lax.dot_general(wr_ref[...], h_ref[...], (((1,), (1,)), ((), ())), preferred_element_type=F32)
    scores = jax.nn.sigmoid(logits)
    biased = scores + bias_ref[...]
    sc = [scores[w * ng:(w + 1) * ng] for w in range(per)]
    bi = [biased[w * ng:(w + 1) * ng] for w in range(per)]
    assert per == 4
    hi01, lo01 = jnp.maximum(bi[0], bi[1]), jnp.minimum(bi[0], bi[1])
    hi23, lo23 = jnp.maximum(bi[2], bi[3]), jnp.minimum(bi[2], bi[3])
    group_score = jnp.maximum(hi01, hi23) + jnp.maximum(jnp.minimum(hi01, hi23), jnp.maximum(lo01, lo23))
    gidx = lax.broadcasted_iota(I32, (ng, tm), 0)
    grank = jnp.zeros((ng, tm), I32)
    for g2 in range(ng):
        row = group_score[g2:g2 + 1]
        grank += ((row > group_score) | ((row == group_score) & (g2 < gidx))).astype(I32)
    gsel = grank < TOPK_GROUPS
    cand = [jnp.where(gsel, b, -jnp.inf) for b in bi]
    eid = [gidx * per + w for w in range(per)]
    erank = [jnp.zeros((ng, tm), I32) for _ in range(per)]
    for w2 in range(per):
        for g2 in range(ng):
            row = cand[w2][g2:g2 + 1]
            e2 = g2 * per + w2
            for w in range(per):
                erank[w] += ((row > cand[w]) | ((row == cand[w]) & (e2 < eid[w]))).astype(I32)
    sel = [r < TOP_K for r in erank]
    ssum = functools.reduce(jnp.add, [jnp.sum(jnp.where(m, s, 0.0), axis=0, keepdims=True) for m, s in zip(sel, sc)])
    gate = [jnp.where(m, s / ssum * ROUTED_SCALE, 0.0) for m, s in zip(sel, sc)]
    sel_all = jnp.concatenate([m.astype(F32) for m in sel], axis=0)
    earlier = (lax.broadcasted_iota(I32, (tm, tm), 0) < lax.broadcasted_iota(I32, (tm, tm), 1)).astype(BF16)
    pos = jnp.dot(sel_all.astype(BF16), earlier, preferred_element_type=F32) + run_scr[...]
    run_scr[...] += jnp.sum(sel_all, axis=1, keepdims=True)
    cnt_ref[...] = run_scr[...].astype(I32)
    above = (lax.broadcasted_iota(I32, (n_exp, n_exp), 1) < lax.broadcasted_iota(I32, (n_exp, n_exp), 0)).astype(BF16)
    kidx = jnp.dot(above, sel_all.astype(BF16), preferred_element_type=F32)
    eid_all = jnp.concatenate(eid, axis=0).astype(F32)
    gate_all = jnp.concatenate(gate, axis=0)
    for k in range(TOP_K):
        mk = (sel_all > 0.0) & (kidx == float(k))
        ek_ref[k:k + 1, :] = jnp.sum(jnp.where(mk, eid_all, 0.0), axis=0, keepdims=True).astype(I32)
        pk_ref[k:k + 1, :] = jnp.sum(jnp.where(mk, pos, 0.0), axis=0, keepdims=True).astype(I32)
        gk_ref[k:k + 1, :] = jnp.sum(jnp.where(mk, gate_all, 0.0), axis=0, keepdims=True)


def moe_route(h, router_w, router_bias):
    t, d = h.shape
    n_exp = router_w.shape[1]
    per = n_exp // N_EXPERT_GROUPS
    wr = router_w.T.reshape(N_EXPERT_GROUPS, per, d).transpose(1, 0, 2).reshape(n_exp, d).astype(BF16)
    bias = router_bias.astype(F32).reshape(N_EXPERT_GROUPS, per).T.reshape(n_exp, 1)
    tm = ROUTER_TILE
    kspec = pl.BlockSpec((TOP_K, tm), lambda i: (0, i))
    ek, pk, gk, cnt = pl.pallas_call(
        functools.partial(_router_kernel, n_exp=n_exp), grid=(t // tm,),
        in_specs=[pl.BlockSpec((tm, d), lambda i: (i, 0)), pl.BlockSpec((n_exp, d), lambda i: (0, 0)),
                  pl.BlockSpec((n_exp, 1), lambda i: (0, 0))],
        out_specs=[kspec, kspec, kspec, pl.BlockSpec((n_exp, 1), lambda i: (0, 0))],
        out_shape=[jax.ShapeDtypeStruct((TOP_K, t), I32), jax.ShapeDtypeStruct((TOP_K, t), I32),
                   jax.ShapeDtypeStruct((TOP_K, t), F32), jax.ShapeDtypeStruct((n_exp, 1), I32)],
        scratch_shapes=[pltpu.VMEM((n_exp, 1), F32)],
        compiler_params=_params("arbitrary"), name="moe_router",
    )(h, wr, bias)
    counts = cnt.reshape(per, N_EXPERT_GROUPS).T.reshape(n_exp)
    return ek, pk, gk, counts


def _row_copy(src_ref, src_row, dst_ref, dst_row, sem):
    return pltpu.make_async_copy(src_ref.at[pl.ds(src_row, 1)], dst_ref.at[pl.ds(dst_row, 1)], sem)


def _rows_done(src_ref, dst_ref, sem):
    pltpu.make_async_copy(src_ref.at[pl.ds(0, dst_ref.shape[0])], dst_ref, sem).wait()


def _slot_token_kernel(slot_ref, pad_lo_ref, pad_hi_ref, tok_ref, *, n_tok, steps):
    j = pl.program_id(0)

    @pl.when(j == 0)
    def _():
        for e in range(pad_lo_ref.shape[0]):
            def clear(s, _):
                tok_ref[s] = 0
                return 0
            lax.fori_loop(pad_lo_ref[e], pad_hi_ref[e], clear, 0)

    per = n_tok // steps
    for k in range(TOP_K):
        def put(r, _, k=k):
            tok = j * per + r
            tok_ref[slot_ref[k * n_tok + tok]] = tok
            return 0
        lax.fori_loop(0, per, put, 0, unroll=8)


def moe_slot_tokens(slots, pad_lo, pad_hi, n_slots):
    n_tok = slots.shape[0] // TOP_K
    steps = 64
    assert n_tok % steps == 0
    return pl.pallas_call(
        functools.partial(_slot_token_kernel, n_tok=n_tok, steps=steps),
        grid_spec=pltpu.PrefetchScalarGridSpec(
            num_scalar_prefetch=3, grid=(steps,), in_specs=[],
            out_specs=pl.BlockSpec(memory_space=pltpu.SMEM)),
        out_shape=jax.ShapeDtypeStruct((n_slots,), I32),
        compiler_params=_params("arbitrary"), name="moe_slot_tokens",
    )(slots, pad_lo, pad_hi)


def _expert_kernel(te_ref, nused_ref, tok_ref, h_ref, wg_ref, wu_ref, wd_ref, ys_ref, xbuf, sem, *, te):
    i = pl.program_id(0)
    n_used = nused_ref[0]
    nbuf = GATHER_AHEAD + 1

    @pl.when(i == 0)
    def _():
        for tile in range(GATHER_AHEAD):
            @pl.when(tile < n_used)
            def _(tile=tile):
                def issue(r, _):
                    _row_copy(h_ref, tok_ref[tile * te + r], xbuf.at[tile], r, sem.at[tile]).start()
                    return 0
                lax.fori_loop(0, te, issue, 0, unroll=8)

    def mlp(fetch):
        cur = i % nbuf
        _rows_done(h_ref, xbuf.at[cur], sem.at[cur])
        if fetch:
            nxt = (i + GATHER_AHEAD) % nbuf
            for r in range(te):
                _row_copy(h_ref, tok_ref[(i + GATHER_AHEAD) * te + r], xbuf.at[nxt], r, sem.at[nxt]).start()
        x_lo, x_hi = [t.astype(BF16) for t in _unpack_row_halves(xbuf[cur])]
        half = x_lo.shape[1]
        lo, hi = pl.ds(0, half), pl.ds(half, half)
        g = (jnp.dot(x_lo, wg_ref[lo, :], preferred_element_type=F32)
             + jnp.dot(x_hi, wg_ref[hi, :], preferred_element_type=F32))
        u = (jnp.dot(x_lo, wu_ref[lo, :], preferred_element_type=F32)
             + jnp.dot(x_hi, wu_ref[hi, :], preferred_element_type=F32))
        a = (jax.nn.silu(g) * u).astype(BF16)
        ys_ref[...] = _pack_row_halves(jnp.dot(a, wd_ref[...], preferred_element_type=F32))

    pl.when(i + GATHER_AHEAD < n_used)(functools.partial(mlp, True))
    pl.when((i < n_used) & (i + GATHER_AHEAD >= n_used))(functools.partial(mlp, False))

    @pl.when(i >= n_used)
    def _():
        ys_ref[...] = jnp.zeros_like(ys_ref)


def moe_experts(h_rows, slot_tok, tile_expert, n_used, wg, wu, wd):
    dh = h_rows.shape[1]
    d = 2 * dh
    n_slots = slot_tok.shape[0]
    de = wg.shape[2]
    te = EXPERT_TILE
    return pl.pallas_call(
        functools.partial(_expert_kernel, te=te),
        grid_spec=pltpu.PrefetchScalarGridSpec(
            num_scalar_prefetch=3, grid=(n_slots // te,),
            in_specs=[pl.BlockSpec(memory_space=pl.ANY),
                      pl.BlockSpec((None, d, de), lambda i, e, n, tk: (e[i], 0, 0)),
                      pl.BlockSpec((None, d, de), lambda i, e, n, tk: (e[i], 0, 0)),
                      pl.BlockSpec((None, de, d), lambda i, e, n, tk: (e[i], 0, 0))],
            out_specs=pl.BlockSpec((te, dh), lambda i, e, n, tk: (i, 0)),
            scratch_shapes=[pltpu.VMEM((GATHER_AHEAD + 1, te, dh), jnp.uint32),
                            pltpu.SemaphoreType.DMA((GATHER_AHEAD + 1,))]),
        out_shape=jax.ShapeDtypeStruct((n_slots, dh), jnp.uint32),
        compiler_params=_params("arbitrary"), name="moe_experts",
    )(tile_expert, n_used, slot_tok, h_rows, wg, wu, wd)


def _combine_kernel(slot_ref, gk_ref, sh_ref, x_ref, gate_ref, ng_ref, ys_ref, o_ref, buf, sem, *, tm, n_tok):
    i = pl.program_id(0)
    nbuf = GATHER_AHEAD + 1

    @pl.when(i == 0)
    def _():
        for tile in range(GATHER_AHEAD):
            for k in range(TOP_K):
                def issue(r, _, k=k, tile=tile):
                    _row_copy(ys_ref, slot_ref[k * n_tok + tile * tm + r], buf.at[tile, k], r, sem.at[tile]).start()
                    return 0
                lax.fori_loop(0, tm, issue, 0, unroll=8)

    def tile(fetch):
        b = i % nbuf
        for k in range(TOP_K):
            _rows_done(ys_ref, buf.at[b, k], sem.at[b])
        if fetch:
            nxt = (i + GATHER_AHEAD) % nbuf
            for k in range(TOP_K):
                for r in range(tm):
                    src = slot_ref[k * n_tok + (i + GATHER_AHEAD) * tm + r]
                    _row_copy(ys_ref, src, buf.at[nxt, k], r, sem.at[nxt]).start()
        half = buf.shape[-1]
        lo, hi = pl.ds(0, half), pl.ds(half, half)
        y_lo, y_hi = sh_ref[:, lo].astype(F32), sh_ref[:, hi].astype(F32)
        gk = gk_ref[...]
        for k in range(TOP_K):
            e_lo, e_hi = _unpack_row_halves(buf[b, k])
            y_lo = y_lo + gk[:, k:k + 1] * e_lo
            y_hi = y_hi + gk[:, k:k + 1] * e_hi
        ssq = jnp.sum(y_lo * y_lo, axis=-1, keepdims=True) + jnp.sum(y_hi * y_hi, axis=-1, keepdims=True)
        inv = lax.rsqrt(ssq / (2 * half) + NORM_EPS)
        o_ref[:, lo] = x_ref[:, lo] + gate_ref[:, lo] * (y_lo * inv * ng_ref[:, lo])
        o_ref[:, hi] = x_ref[:, hi] + gate_ref[:, hi] * (y_hi * inv * ng_ref[:, hi])

    pl.when(i + GATHER_AHEAD < pl.num_programs(0))(functools.partial(tile, True))
    pl.when(i + GATHER_AHEAD >= pl.num_programs(0))(functools.partial(tile, False))


def moe_combine(x, ys, slots, gk_t, shared, norm_g, gate):
    bsz, seq, d = x.shape
    tm = COMBINE_TILE
    nt = seq // tm
    return pl.pallas_call(
        functools.partial(_combine_kernel, tm=tm, n_tok=bsz * seq),
        grid_spec=pltpu.PrefetchScalarGridSpec(
            num_scalar_prefetch=1, grid=(bsz * nt,),
            in_specs=[pl.BlockSpec((tm, TOP_K), lambda i, s: (i, 0)),
                      pl.BlockSpec((tm, d), lambda i, s: (i, 0)),
                      pl.BlockSpec((None, tm, d), lambda i, s: (i // nt, i % nt, 0)),
                      pl.BlockSpec((None, 1, d), lambda i, s: (i // nt, 0, 0)),
                      pl.BlockSpec((1, d), lambda i, s: (0, 0)),
                      pl.BlockSpec(memory_space=pl.ANY)],
            out_specs=pl.BlockSpec((None, tm, d), lambda i, s: (i // nt, i % nt, 0)),
            scratch_shapes=[pltpu.VMEM((GATHER_AHEAD + 1, TOP_K, tm, d // 2), jnp.uint32),
                            pltpu.SemaphoreType.DMA((GATHER_AHEAD + 1,))]),
        out_shape=jax.ShapeDtypeStruct(x.shape, F32),
        compiler_params=_params("arbitrary"), name="moe_combine",
    )(slots, gk_t, shared, x, gate.reshape(bsz, 1, d), norm_g.reshape(1, d), ys)


def moe_layer(x, h, h_rows, norm_g, gate, router_w, router_bias, e_gate, e_up, e_down, s_gate, s_up, s_down):
    t, d = h.shape
    n_exp = router_w.shape[1]
    te = EXPERT_TILE
    ek, pk, gk, counts = moe_route(h, router_w, router_bias)
    tiles = (counts + te - 1) // te
    tile_end = jnp.cumsum(tiles)
    offsets = (tile_end - tiles) * te
    experts = jnp.arange(n_exp, dtype=I32)
    slot = pk + jnp.sum(jnp.where(ek[..., None] == experts, offsets, 0), axis=-1)
    slots = slot.reshape(-1).astype(I32)
    n_tiles = (t * TOP_K) // te + n_exp
    n_used = tile_end[-1].astype(I32)
    tile_ids = jnp.arange(n_tiles, dtype=I32)
    tile_expert = jnp.sum((tile_end[None, :] <= jnp.minimum(tile_ids, n_used - 1)[:, None]).astype(I32), axis=1)
    pad_lo = jnp.concatenate([offsets + counts, tile_end[-1:] * te]).astype(I32)
    pad_hi = jnp.concatenate([tile_end * te, jnp.full((1,), n_tiles * te, tile_end.dtype)]).astype(I32)
    slot_tok = moe_slot_tokens(slots, pad_lo, pad_hi, n_tiles * te)
    ys = moe_experts(h_rows, slot_tok, tile_expert, n_used.reshape(1), e_gate, e_up, e_down)
    act = matmul_swiglu(h, s_gate, s_up, tm=1024, tn=512, name="shared_up")
    shared = matmul(act, s_down, BF16, tm=1024, tn=1024, name="shared_down")
    return moe_combine(x, ys, slots, gk.T, shared, norm_g, gate)


def kernel(x, c, positions, ada_w, ada_b, norm_g, ssm_w_in, ssm_a_re, ssm_a_im, ssm_log_step, ssm_b_re, ssm_b_im, ssm_c_re, ssm_c_im, ssm_d, glu_w, glu_b, ssm_w_out, kv_norm_g, w_k, w_v, attn_w_q, attn_w_o, router_w, router_bias, exp_w_gate, exp_w_up, exp_w_down, sh_w_gate, sh_w_up, sh_w_down):
    bsz, seq, d = x.shape
    depth = ada_w.shape[0]
    n_a = ssm_w_in.shape[0]
    t = bsz * seq
    mod = ada_modulation(c, ada_w, ada_b)
    k = v = None
    for layer in range(depth):
        sh1, sc1, g1, sh2, sc2, g2 = jnp.split(mod[layer], N_MOD, axis=-1)
        if layer < n_a:
            a = layer
            (h,) = norm_modulate(x, norm_g[layer, 0], sc1, sh1)
            u_tm = matmul_to_token_major(h, cast_bf16(ssm_w_in, a), bsz, tm=1024, tn=512, name="ssm_in")
            bblk, cblk, ab = s5_operators(ssm_a_re[a], ssm_a_im[a], ssm_log_step[a], ssm_b_re[a], ssm_b_im[a],
                                          ssm_c_re[a], ssm_c_im[a])
            z = s5_scan(u_tm.reshape(seq * bsz, -1), bblk, cblk, ab, ssm_d[a], bsz)
            zz = matmul_glu(z, cast_bf16(glu_w, a), glu_b[a], tm=1024, tn=512, name="ssm_glu")
            y = matmul_from_token_major(zz.reshape(seq, -1), cast_bf16(ssm_w_out, a), bsz, BF16,
                                        tm=1024, tn=512, name="ssm_out")
        else:
            bl = layer - n_a
            n_dil = attn_w_q.shape[2] // d
            assert all(window // dil == ATTN_BLOCK for window, dil in DILATION_GROUPS[:n_dil])
            dils = tuple(dil for _, dil in DILATION_GROUPS[:n_dil])
            if layer == n_a:
                h, hkv = norm_modulate(x, norm_g[layer, 0], sc1, sh1, kv_g=kv_norm_g)
                kvd = w_k.shape[1]
                k_tabs = rope_tables(positions, 1.0)
                k = matmul_streams(hkv, cast_bf16(w_k[None], 0), 0, kvd, dils, k_tabs, tn=min(kvd, 512), name="k_proj")
                v = matmul_streams(hkv, cast_bf16(w_v[None], 0), 0, kvd, dils, None, tn=min(kvd, 512), name="v_proj")
            else:
                (h,) = norm_modulate(x, norm_g[layer, 0], sc1, sh1)
            q_tabs = rope_tables(positions, HEAD_DIM ** -0.5)
            wq = cast_bf16(attn_w_q, bl)
            tnq = min(d, 256)
            qs = [matmul_streams(h, wq, grp * (d // tnq), d, (dil,), q_tabs, tn=tnq, name=f"q_proj_{dil}")[0]
                  for grp, dil in enumerate(dils)]
            o = dilated_attention(qs, k, v, bsz, dils)
            y = matmul(o, cast_bf16(attn_w_o, bl), BF16, tm=1024, tn=512, name="attn_out")
        x, h, h_rows = norm_modulate(x, norm_g[layer, 2], sc2, sh2, with_rows=True,
                                     resid=(y, norm_g[layer, 1], g1))
        x = moe_layer(x, h, h_rows, norm_g[layer, 3], g2, router_w[layer], router_bias[layer],
                      cast_bf16(exp_w_gate, layer), cast_bf16(exp_w_up, layer), cast_bf16(exp_w_down, layer),
                      cast_bf16(sh_w_gate, layer), cast_bf16(sh_w_up, layer), cast_bf16(sh_w_down, layer))
    return x
```

```python
import functools
import math

import jax
import jax.numpy as jnp
from jax import lax
from jax.experimental import pallas as pl
from jax.experimental.pallas import tpu as pltpu

F32 = jnp.float32
BF16 = jnp.bfloat16
I32 = jnp.int32

LANE = 128
SUBLANE = 8
VMEM_LIMIT_BYTES = 56 * 1024 * 1024
MXU_DIM = 256

N_MOD = 6
NORM_EPS = 1e-6
HEAD_DIM = 128
ROT_DIM = HEAD_DIM // 4
ROT_HALF = ROT_DIM // 2
ROPE_THETA = 500000.0
DILATION_GROUPS = ((128, 1), (512, 4), (2048, 16))
ATTN_BLOCK = 128
ATTN_SPAN = ATTN_BLOCK * max(dil for _, dil in DILATION_GROUPS)
N_EXPERT_GROUPS = 8
TOPK_GROUPS = 4
TOP_K = 4
ROUTED_SCALE = 2.5
NEG_MASK = -1e30

SSM_PACK_CH = MXU_DIM
SSM_CHUNK_TOKENS = 256
SSM_LANE_CHUNK = 512
EXPERT_TILE = 256
ROW_TILE = 256
COMBINE_TILE = 128
GATHER_AHEAD = 2
ROUTER_TILE = 512
CAST_BLOCK_BYTES = 4 * 1024 * 1024


def _params(*sem):
    return pltpu.CompilerParams(dimension_semantics=sem, vmem_limit_bytes=VMEM_LIMIT_BYTES)


def _cast_kernel(x_ref, o_ref):
    o_ref[...] = x_ref[...].astype(o_ref.dtype)


def cast_bf16(w, layer):
    lead, c = w.shape[1:-1], w.shape[-1]
    rows = math.prod(lead)
    w3 = w.reshape(w.shape[0], rows, c)
    br = min(rows, 1 << ((CAST_BLOCK_BYTES // (4 * c)).bit_length() - 1))
    assert rows % br == 0 and br % (2 * SUBLANE) == 0
    out = pl.pallas_call(
        _cast_kernel, grid=(rows // br,),
        in_specs=[pl.BlockSpec((None, br, c), lambda i: (layer, i, 0))],
        out_specs=pl.BlockSpec((br, c), lambda i: (i, 0)),
        out_shape=jax.ShapeDtypeStruct((rows, c), BF16),
        compiler_params=_params("parallel"), name="cast_bf16",
    )(w3)
    return out.reshape(*lead, c)


def _ada_kernel(c_ref, w_ref, b_ref, o_ref):
    cond = jax.nn.silu(c_ref[...])
    o_ref[...] = jnp.dot(cond, w_ref[...], precision=lax.Precision.HIGHEST,
                         preferred_element_type=F32) + b_ref[...]


def ada_modulation(c, ada_w, ada_b):
    depth, d, n = ada_w.shape
    bsz = c.shape[0]
    rows = -(-bsz // SUBLANE) * SUBLANE
    cp = jnp.zeros((rows, d), F32).at[:bsz].set(c)
    tn = 512
    out = pl.pallas_call(
        _ada_kernel,
        grid=(depth, n // tn),
        in_specs=[pl.BlockSpec((rows, d), lambda l, j: (0, 0)),
                  pl.BlockSpec((None, d, tn), lambda l, j: (l, 0, j)),
                  pl.BlockSpec((None, 1, tn), lambda l, j: (l, 0, j))],
        out_specs=pl.BlockSpec((None, rows, tn), lambda l, j: (l, 0, j)),
        out_shape=jax.ShapeDtypeStruct((depth, rows, n), F32),
        compiler_params=_params("parallel", "parallel"),
        name="ada_modulation",
    )(cp, ada_w, ada_b.reshape(depth, 1, n))
    return out[:, :bsz]


def _bf16_bits(v):
    return lax.bitcast_convert_type(v.astype(BF16).astype(F32), jnp.uint32)


def _pack_row_halves(v):
    half = v.shape[1] // 2
    return (_bf16_bits(v[:, :half]) >> 16) | _bf16_bits(v[:, half:])


def _unpack_row_halves(p):
    lo = lax.bitcast_convert_type(p << 16, F32)
    hi = lax.bitcast_convert_type(p & jnp.uint32(0xFFFF0000), F32)
    return lo, hi


def _normmod_kernel(x_ref, g_ref, sc_ref, sh_ref, *rest, with_kv, with_rows, with_resid):
    x = x_ref[...]
    rest = list(rest)
    if with_resid:
        y_ref, rg_ref, gate_ref = rest.pop(0), rest.pop(0), rest.pop(0)
        y = y_ref[...].astype(F32)
        x = x + gate_ref[...] * (y * lax.rsqrt(jnp.mean(y * y, axis=-1, keepdims=True) + NORM_EPS) * rg_ref[...])
    xn = x * lax.rsqrt(jnp.mean(x * x, axis=-1, keepdims=True) + NORM_EPS)
    h = xn * g_ref[...] * (1.0 + sc_ref[...]) + sh_ref[...]
    if with_kv:
        kvg_ref = rest.pop(0)
    if with_resid:
        rest.pop(0)[...] = x
    h_ref = rest.pop(0)
    h_ref[...] = h.astype(h_ref.dtype)
    if with_kv:
        rest.pop(0)[...] = (xn * kvg_ref[...]).astype(BF16)
    if with_rows:
        rest.pop(0)[...] = _pack_row_halves(h)


def norm_modulate(x, g, sc, sh, kv_g=None, with_rows=False, resid=None):
    bsz, seq, d = x.shape
    ts = ROW_TILE
    nt = seq // ts
    row = lambda b, i: (b * nt + i, 0)
    vec = pl.BlockSpec((1, d), lambda b, i: (0, 0))
    bvec = pl.BlockSpec((None, 1, d), lambda b, i: (b, 0, 0))
    xspec = pl.BlockSpec((None, ts, d), lambda b, i: (b, i, 0))
    in_specs = [xspec, vec, bvec, bvec]
    args = [x, g.reshape(1, d), sc.reshape(bsz, 1, d), sh.reshape(bsz, 1, d)]
    out_specs = [pl.BlockSpec((ts, d), row)]
    out_shape = [jax.ShapeDtypeStruct((bsz * seq, d), BF16)]
    if resid is not None:
        y, rg, gate = resid
        in_specs += [pl.BlockSpec((ts, d), row), vec, bvec]
        args += [y, rg.reshape(1, d), gate.reshape(bsz, 1, d)]
        out_specs.insert(0, xspec)
        out_shape.insert(0, jax.ShapeDtypeStruct(x.shape, F32))
    if kv_g is not None:
        in_specs.append(vec)
        args.append(kv_g.reshape(1, d))
        out_specs.append(pl.BlockSpec((ts, d), row))
        out_shape.append(jax.ShapeDtypeStruct((bsz * seq, d), BF16))
    if with_rows:
        out_specs.append(pl.BlockSpec((ts, d // 2), row))
        out_shape.append(jax.ShapeDtypeStruct((bsz * seq, d // 2), jnp.uint32))
    return pl.pallas_call(
        functools.partial(_normmod_kernel, with_kv=kv_g is not None, with_rows=with_rows,
                          with_resid=resid is not None),
        grid=(bsz, nt), in_specs=in_specs, out_specs=out_specs, out_shape=out_shape,
        compiler_params=_params("parallel", "parallel"), name="norm_modulate",
    )(*args)


def _mm_kernel(a_ref, w_ref, o_ref):
    o_ref[...] = jnp.dot(a_ref[...], w_ref[...], preferred_element_type=F32).astype(o_ref.dtype)


def _mm_glu_kernel(a_ref, w_ref, z_ref, b_ref, o_ref):
    acc = jnp.dot(a_ref[...], w_ref[...], preferred_element_type=F32) + b_ref[...]
    o_ref[...] = (z_ref[...].astype(F32) * jax.nn.sigmoid(acc)).astype(o_ref.dtype)


def _mm_swiglu_kernel(a_ref, wg_ref, wu_ref, o_ref):
    a = a_ref[...]
    g = jnp.dot(a, wg_ref[...], preferred_element_type=F32)
    u = jnp.dot(a, wu_ref[...], preferred_element_type=F32)
    o_ref[...] = (jax.nn.silu(g) * u).astype(o_ref.dtype)


def matmul(a, w, out_dtype, *, tm, tn, name):
    m, k = a.shape
    n = w.shape[1]
    tm, tn = min(tm, m), min(tn, n)
    return pl.pallas_call(
        _mm_kernel, grid=(m // tm, n // tn),
        in_specs=[pl.BlockSpec((tm, k), lambda i, j: (i, 0)), pl.BlockSpec((k, tn), lambda i, j: (0, j))],
        out_specs=pl.BlockSpec((tm, tn), lambda i, j: (i, j)),
        out_shape=jax.ShapeDtypeStruct((m, n), out_dtype),
        compiler_params=_params("parallel", "arbitrary"), name=name,
    )(a, w)


def matmul_swiglu(a, wg, wu, *, tm, tn, name):
    m, k = a.shape
    n = wg.shape[1]
    tm, tn = min(tm, m), min(tn, n)
    wspec = pl.BlockSpec((k, tn), lambda i, j: (0, j))
    return pl.pallas_call(
        _mm_swiglu_kernel, grid=(m // tm, n // tn),
        in_specs=[pl.BlockSpec((tm, k), lambda i, j: (i, 0)), wspec, wspec],
        out_specs=pl.BlockSpec((tm, tn), lambda i, j: (i, j)),
        out_shape=jax.ShapeDtypeStruct((m, n), BF16),
        compiler_params=_params("parallel", "arbitrary"), name=name,
    )(a, wg, wu)


def matmul_glu(z, w, b, *, tm, tn, name):
    m, k = z.shape
    n = w.shape[1]
    tm, tn = min(tm, m), min(tn, n)
    return pl.pallas_call(
        _mm_glu_kernel, grid=(m // tm, n // tn),
        in_specs=[pl.BlockSpec((tm, k), lambda i, j: (i, 0)), pl.BlockSpec((k, tn), lambda i, j: (0, j)),
                  pl.BlockSpec((tm, tn), lambda i, j: (i, j)), pl.BlockSpec((1, tn), lambda i, j: (0, j))],
        out_specs=pl.BlockSpec((tm, tn), lambda i, j: (i, j)),
        out_shape=jax.ShapeDtypeStruct((m, n), BF16),
        compiler_params=_params("parallel", "arbitrary"), name=name,
    )(z, w, z, b.reshape(1, n))


def _mm_to_token_major_kernel(a_ref, w_ref, o_ref, scr):
    bsz, tm, _ = a_ref.shape
    for b in range(bsz):
        acc = jnp.dot(a_ref[b], w_ref[...], preferred_element_type=F32)
        for j in range(scr.shape[0]):
            scr[j, pl.ds(b, tm, stride=bsz), :] = acc[:, j * LANE:(j + 1) * LANE]
    for j in range(scr.shape[0]):
        o_ref[:, j * LANE:(j + 1) * LANE] = scr[j].astype(o_ref.dtype)


def _mm_from_token_major_kernel(a_ref, w_ref, o_ref, scr):
    bsz, tm, _ = o_ref.shape
    acc = jnp.dot(a_ref[...], w_ref[...], preferred_element_type=F32)
    for j in range(scr.shape[0]):
        scr[j] = acc[:, j * LANE:(j + 1) * LANE]
    for b in range(bsz):
        for j in range(scr.shape[0]):
            o_ref[b, :, j * LANE:(j + 1) * LANE] = scr[j, pl.ds(b, tm, stride=bsz), :].astype(o_ref.dtype)


def matmul_to_token_major(a, w, bsz, *, tm, tn, name):
    m, k = a.shape
    n = w.shape[1]
    seq = m // bsz
    tm, tn = min(tm, seq), min(tn, n)
    return pl.pallas_call(
        _mm_to_token_major_kernel, grid=(seq // tm, n // tn),
        in_specs=[pl.BlockSpec((bsz, tm, k), lambda i, j: (0, i, 0)),
                  pl.BlockSpec((k, tn), lambda i, j: (0, j))],
        out_specs=pl.BlockSpec((bsz * tm, tn), lambda i, j: (i, j)),
        out_shape=jax.ShapeDtypeStruct((m, n), F32),
        scratch_shapes=[pltpu.VMEM((tn // LANE, bsz * tm, LANE), F32)],
        compiler_params=_params("parallel", "arbitrary"), name=name,
    )(a.reshape(bsz, seq, k), w)


def matmul_from_token_major(a, w, bsz, out_dtype, *, tm, tn, name):
    m, k = a.shape
    n = w.shape[1]
    seq = m // bsz
    tm, tn = min(tm, seq), min(tn, n)
    out = pl.pallas_call(
        _mm_from_token_major_kernel, grid=(seq // tm, n // tn),
        in_specs=[pl.BlockSpec((bsz * tm, k), lambda i, j: (i, 0)),
                  pl.BlockSpec((k, tn), lambda i, j: (0, j))],
        out_specs=pl.BlockSpec((bsz, tm, tn), lambda i, j: (0, i, j)),
        out_shape=jax.ShapeDtypeStruct((bsz, seq, n), out_dtype),
        scratch_shapes=[pltpu.VMEM((tn // LANE, bsz * tm, LANE), F32)],
        compiler_params=_params("parallel", "arbitrary"), name=name,
    )(a, w)
    return out.reshape(m, n)


def _s5_discretize_kernel(are_ref, aim_ref, ls_ref, abr_ref, abi_ref, fr_ref, fi_ref):
    lr, li = are_ref[...], aim_ref[...]
    step = jnp.exp(ls_ref[...])
    decay = jnp.exp(lr * step)
    abar_re = decay * jnp.cos(li * step)
    abar_im = decay * jnp.sin(li * step)
    denom = lr * lr + li * li
    nr = abar_re - 1.0
    ni = abar_im
    abr_ref[...] = abar_re
    abi_ref[...] = abar_im
    fr_ref[...] = (nr * lr + ni * li) / denom
    fi_ref[...] = (ni * lr - nr * li) / denom


def s5_discretize(a_re, a_im, log_step):
    g, p = a_re.shape
    shp = jax.ShapeDtypeStruct((g, p), F32)
    return pl.pallas_call(_s5_discretize_kernel, out_shape=[shp] * 4, name="s5_discretize")(
        a_re, a_im, log_step.reshape(g, 1))


def _s5_scan_kernel(un_ref, up_ref, bblk_ref, cblk_ref, ab_ref, d_ref, z_ref, x2_scr, xb2_scr, carry_scr,
                    *, rows, nstate, bsz):
    c = pl.program_id(1)
    n_chunks = pl.num_programs(1) - 1

    @pl.when(c == 0)
    def _():
        carry_scr[...] = jnp.zeros_like(carry_scr)
        xb2_scr[1] = jnp.zeros(xb2_scr.shape[1:], xb2_scr.dtype)
        x2_scr[0] = jnp.dot(up_ref[...].astype(BF16), bblk_ref[...], preferred_element_type=F32)

    u_prev = up_ref[...]
    x2_scr[(c + 1) % 2] = jnp.dot(un_ref[...].astype(BF16), bblk_ref[...], preferred_element_type=F32)
    y = jnp.dot(xb2_scr[(c + 1) % 2], cblk_ref[...], preferred_element_type=F32) + d_ref[...] * u_prev
    z_ref[...] = jax.nn.gelu(y).astype(z_ref.dtype)

    @pl.when(c < n_chunks)
    def _():
        _s5_recurrence(x2_scr.at[c % 2], xb2_scr.at[c % 2], ab_ref, carry_scr, rows=rows, nstate=nstate, bsz=bsz)


def _s5_recurrence(x_scr, xb_scr, ab_ref, carry_scr, *, rows, nstate, bsz):
    lw = SSM_LANE_CHUNK
    second = lax.broadcasted_iota(I32, (SUBLANE, lw), 0) >= bsz
    for lt in range(nstate // lw):
        re = slice(lt * lw, (lt + 1) * lw)
        im = slice(nstate + lt * lw, nstate + (lt + 1) * lw)
        ar, ai = ab_ref[0:1, re], ab_ref[1:2, re]
        a2r, a2i = ar * ar - ai * ai, 2.0 * ar * ai
        amr, ami = jnp.where(second, ar, 0.0), jnp.where(second, ai, 0.0)
        pr, pi = jnp.where(second, a2r, ar), jnp.where(second, a2i, ai)

        def advance(r0, cr, ci, re=re, im=im, amr=amr, ami=ami, pr=pr, pi=pi):
            bur, bui = x_scr[pl.ds(r0, SUBLANE), re], x_scr[pl.ds(r0, SUBLANE), im]
            sr, si = pltpu.roll(bur, bsz, 0), pltpu.roll(bui, bsz, 0)
            wr = bur + amr * sr - ami * si
            wi = bui + amr * si + ami * sr
            cbr = jnp.where(second, cr, pltpu.roll(cr, bsz, 0))
            cbi = jnp.where(second, ci, pltpu.roll(ci, bsz, 0))
            return wr + pr * cbr - pi * cbi, wi + pr * cbi + pi * cbr

        def body(n, carry, re=re, im=im, advance=advance):
            r0 = pl.multiple_of(n * (2 * SUBLANE), 2 * SUBLANE)
            xr0, xi0 = advance(r0, *carry)
            xr1, xi1 = advance(r0 + SUBLANE, xr0, xi0)
            xb_scr[pl.ds(r0, 2 * SUBLANE), re] = jnp.concatenate([xr0, xr1], axis=0).astype(BF16)
            xb_scr[pl.ds(r0, 2 * SUBLANE), im] = jnp.concatenate([xi0, xi1], axis=0).astype(BF16)
            return xr1, xi1

        cr, ci = lax.fori_loop(0, rows // (2 * SUBLANE), body, (carry_scr[:, re], carry_scr[:, im]))
        carry_scr[:, re] = cr
        carry_scr[:, im] = ci


def s5_scan(u_tm, bblk, cblk, ab, d_skip, bsz):
    m, width = u_tm.shape
    assert 2 * bsz == SUBLANE, "the scan tiles two tokens of every batch element into one sublane tile"
    packs, pack_ch, two_n = bblk.shape
    nstate = two_n // 2
    rows = SSM_CHUNK_TOKENS * bsz
    n_chunks = m // rows
    nxt = lambda p, c: (jnp.minimum(c + 1, n_chunks - 1), p)
    prv = lambda p, c: (jnp.clip(c - 1, 0, n_chunks - 1), p)
    return pl.pallas_call(
        functools.partial(_s5_scan_kernel, rows=rows, nstate=nstate, bsz=bsz),
        grid=(packs, n_chunks + 1),
        in_specs=[pl.BlockSpec((rows, pack_ch), nxt), pl.BlockSpec((rows, pack_ch), prv),
                  pl.BlockSpec((None, pack_ch, two_n), lambda p, c: (p, 0, 0)),
                  pl.BlockSpec((None, two_n, pack_ch), lambda p, c: (p, 0, 0)),
                  pl.BlockSpec((None, 2, nstate), lambda p, c: (p, 0, 0)),
                  pl.BlockSpec((None, 1, pack_ch), lambda p, c: (p, 0, 0))],
        out_specs=pl.BlockSpec((rows, pack_ch), prv),
        out_shape=jax.ShapeDtypeStruct((m, width), BF16),
        scratch_shapes=[pltpu.VMEM((2, rows, two_n), F32), pltpu.VMEM((2, rows, two_n), BF16),
                        pltpu.VMEM((SUBLANE, two_n), F32)],
        compiler_params=_params("parallel", "arbitrary"), name="s5_scan",
    )(u_tm, u_tm, bblk, cblk, ab, d_skip.reshape(packs, 1, pack_ch))


def s5_operators(a_re, a_im, log_step, b_re, b_im, c_re, c_im):
    g, p = a_re.shape
    gsz = b_re.shape[-1]
    gpp = SSM_PACK_CH // gsz
    packs = g // gpp
    abr, abi, fr, fi = s5_discretize(a_re, a_im, log_step)
    bbar_re = fr[..., None] * b_re - fi[..., None] * b_im
    bbar_im = fr[..., None] * b_im + fi[..., None] * b_re
    eye = jnp.eye(gpp, dtype=F32)

    def b_block(t):
        t = t.reshape(packs, gpp, p, gsz).transpose(0, 1, 3, 2)
        return (t[:, :, :, None, :] * eye[None, :, None, :, None]).reshape(packs, gpp * gsz, gpp * p)

    def c_block(t):
        t = t.reshape(packs, gpp, gsz, p).transpose(0, 1, 3, 2)
        return (t[:, :, :, None, :] * eye[None, :, None, :, None]).reshape(packs, gpp * p, gpp * gsz)

    bblk = jnp.concatenate([b_block(bbar_re), b_block(bbar_im)], axis=2).astype(BF16)
    cblk = jnp.concatenate([c_block(c_re), -c_block(c_im)], axis=1).astype(BF16)
    ab = jnp.stack([abr.reshape(packs, gpp * p), abi.reshape(packs, gpp * p)], axis=1)
    return bblk, cblk, ab


def _rope_table_kernel(pos_ref, invf_ref, cos_ref, sin_lo_ref, sin_hi_ref, *, scale):
    ang = pos_ref[...].astype(F32) * invf_ref[...]
    lane = lax.broadcasted_iota(I32, ang.shape, 1)
    c, s = jnp.cos(ang), jnp.sin(ang)
    cos_ref[...] = jnp.where(lane < ROT_DIM, c, 1.0) * scale
    sin_lo_ref[...] = jnp.where(lane < ROT_HALF, -s, 0.0) * scale
    sin_hi_ref[...] = jnp.where((lane >= ROT_HALF) & (lane < ROT_DIM), s, 0.0) * scale


def rope_tables(positions, scale):
    t = positions.size
    inv_freq = ROPE_THETA ** (-jnp.arange(ROT_HALF, dtype=F32) * 2.0 / ROT_DIM)
    invf = jnp.zeros((1, HEAD_DIM), F32).at[0, :ROT_DIM].set(jnp.concatenate([inv_freq, inv_freq]))
    ts = 1024
    shp = jax.ShapeDtypeStruct((t, HEAD_DIM), F32)
    spec = pl.BlockSpec((ts, HEAD_DIM), lambda i: (i, 0))
    return pl.pallas_call(
        functools.partial(_rope_table_kernel, scale=scale), grid=(t // ts,),
        in_specs=[pl.BlockSpec((ts, 1), lambda i: (i, 0)), pl.BlockSpec((1, HEAD_DIM), lambda i: (0, 0))],
        out_specs=[spec] * 3, out_shape=[shp] * 3,
        compiler_params=_params("parallel"), name="rope_tables",
    )(positions.reshape(t, 1), invf)


def _stream_rows(chunk, residue, dil):
    start = chunk * (ATTN_BLOCK * dil) + residue
    return pl.ds(start, ATTN_BLOCK) if dil == 1 else pl.ds(start, ATTN_BLOCK, stride=dil)


def _stream_major_store(o_ref, src_ref, dil):
    span = ATTN_BLOCK * dil
    for hd in range(src_ref.shape[0]):
        cols = slice(hd * HEAD_DIM, (hd + 1) * HEAD_DIM)
        for chunk in range(src_ref.shape[1] // span):
            for residue in range(dil):
                dst = pl.ds(chunk * span + residue * ATTN_BLOCK, ATTN_BLOCK)
                o_ref[dst, cols] = src_ref[hd, _stream_rows(chunk, residue, dil), :].astype(o_ref.dtype)


def _mm_streams_kernel(a_ref, w_ref, *rest, dils, rotary):
    rest = list(rest)
    if rotary:
        cos, sin_lo, sin_hi = [rest.pop(0)[...] for _ in range(3)]
    o_refs, scr = rest[:len(dils)], rest[len(dils)]
    acc = jnp.dot(a_ref[...], w_ref[...], preferred_element_type=F32)
    for hd in range(acc.shape[1] // HEAD_DIM):
        t = acc[:, hd * HEAD_DIM:(hd + 1) * HEAD_DIM]
        if rotary:
            t = t * cos + pltpu.roll(t, HEAD_DIM - ROT_HALF, 1) * sin_lo + pltpu.roll(t, ROT_HALF, 1) * sin_hi
        scr[hd] = t
    for o_ref, dil in zip(o_refs, dils):
        _stream_major_store(o_ref, scr, dil)


def matmul_streams(a, w, col_block0, n, dils, tables, *, tn, name):
    m, k = a.shape
    tm = ATTN_SPAN
    rotary = tables is not None
    in_specs = [pl.BlockSpec((tm, k), lambda i, j: (i, 0), pipeline_mode=pl.Buffered(1)),
                pl.BlockSpec((k, tn), lambda i, j: (0, col_block0 + j))]
    args = [a, w]
    if rotary:
        in_specs += [pl.BlockSpec((tm, HEAD_DIM), lambda i, j: (i, 0))] * 3
        args += list(tables)
    return pl.pallas_call(
        functools.partial(_mm_streams_kernel, dils=dils, rotary=rotary), grid=(m // tm, n // tn),
        in_specs=in_specs,
        out_specs=[pl.BlockSpec((tm, tn), lambda i, j: (i, j))] * len(dils),
        out_shape=[jax.ShapeDtypeStruct((m, n), BF16)] * len(dils),
        scratch_shapes=[pltpu.VMEM((tn // HEAD_DIM, tm, HEAD_DIM), F32)],
        compiler_params=_params("parallel", "arbitrary"), name=name,
    )(*args)


def _attn_kernel(*refs, dils):
    n = len(dils)
    q_refs = refs[:n]
    kv_refs = refs[n:5 * n]
    o_ref, o_scr, l_scr = refs[5 * n], refs[5 * n + 1], refs[5 * n + 2]
    blk = ATTN_BLOCK
    nb = ATTN_SPAN // blk
    seq_start = pl.program_id(1) == 0
    qi = lax.broadcasted_iota(I32, (nb, blk, blk), 1)
    kj = lax.broadcasted_iota(I32, (nb, blk, blk), 2)
    bi = lax.broadcasted_iota(I32, (nb, blk, blk), 0)
    cur_ok = kj <= qi
    qk = (((2,), (2,)), ((0,), (0,)))
    pv = (((2,), (1,)), ((0,), (0,)))
    for g, dil in enumerate(dils):
        kp_ref, kc_ref, vp_ref, vc_ref = kv_refs[4 * g:4 * g + 4]
        span = blk * dil
        kc, vc = kc_ref[...], vc_ref[...]
        if span == ATTN_SPAN:
            kp, vp = kp_ref[...], vp_ref[...]
        else:
            kp = jnp.concatenate([kp_ref[...], kc[:ATTN_SPAN - span]], axis=0)
            vp = jnp.concatenate([vp_ref[...], vc[:ATTN_SPAN - span]], axis=0)
        split = lambda t: t.reshape(nb, blk, HEAD_DIM)
        q3, kp3, kc3, vp3, vc3 = split(q_refs[g][...]), split(kp), split(kc), split(vp), split(vc)
        prev_ok = (kj >= qi) & jnp.logical_not(seq_start & (bi < dil))
        sp = jnp.where(prev_ok, lax.dot_general(q3, kp3, qk, preferred_element_type=F32), NEG_MASK)
        sc = jnp.where(cur_ok, lax.dot_general(q3, kc3, qk, preferred_element_type=F32), NEG_MASK)
        mx = jnp.maximum(jnp.max(sp, axis=-1, keepdims=True), jnp.max(sc, axis=-1, keepdims=True))
        pp, pc = jnp.exp(sp - mx), jnp.exp(sc - mx)
        den = jnp.sum(pp, axis=-1, keepdims=True) + jnp.sum(pc, axis=-1, keepdims=True)
        o = (lax.dot_general((pp / den).astype(BF16), vp3, pv, preferred_element_type=F32)
             + lax.dot_general((pc / den).astype(BF16), vc3, pv, preferred_element_type=F32))
        lse = jnp.broadcast_to(mx + jnp.log(den), (nb, blk, HEAD_DIM))
        for b in range(nb):
            dst = _stream_rows(b // dil, b % dil, dil)
            o_scr[g, dst, :] = o[b]
            l_scr[g, dst, :] = lse[b]
    ls = [l_scr[g] for g in range(n)]
    mx = functools.reduce(jnp.maximum, ls)
    es = [jnp.exp(l - mx) for l in ls]
    tot = functools.reduce(jnp.add, es)
    acc = o_scr[0] * (es[0] / tot)
    for g in range(1, n):
        acc = acc + o_scr[g] * (es[g] / tot)
    o_ref[...] = acc.astype(o_ref.dtype)


def dilated_attention(qs, ks, vs, bsz, dils):
    t, d = qs[0].shape
    kvd = ks[0].shape[1]
    rep = d // kvd
    seq = t // bsz
    assert seq % ATTN_SPAN == 0
    ns = seq // ATTN_SPAN
    cur = lambda b, c, h: (b * ns + c, h // rep)
    in_specs = [pl.BlockSpec((ATTN_SPAN, HEAD_DIM), lambda b, c, h: (b * ns + c, h))] * len(dils)
    args = list(qs)
    for g, dil in enumerate(dils):
        span = ATTN_BLOCK * dil
        per = ATTN_SPAN // span
        prev = lambda b, c, h, per=per: (jnp.maximum((b * ns + c) * per - 1, 0), h // rep)
        in_specs += [pl.BlockSpec((span, HEAD_DIM), prev), pl.BlockSpec((ATTN_SPAN, HEAD_DIM), cur)] * 2
        args += [ks[g], ks[g], vs[g], vs[g]]
    return pl.pallas_call(
        functools.partial(_attn_kernel, dils=dils),
        grid=(bsz, ns, d // HEAD_DIM),
        in_specs=in_specs,
        out_specs=pl.BlockSpec((ATTN_SPAN, HEAD_DIM), lambda b, c, h: (b * ns + c, h)),
        out_shape=jax.ShapeDtypeStruct((t, d), BF16),
        scratch_shapes=[pltpu.VMEM((len(dils), ATTN_SPAN, HEAD_DIM), F32)] * 2,
        compiler_params=_params("parallel", "arbitrary", "arbitrary"), name="dilated_attention",
    )(*args)


def _router_kernel(h_ref, wr_ref, bias_ref, ek_ref, pk_ref, gk_ref, cnt_ref, run_scr, *, n_exp):
    tm = h_ref.shape[0]
    per = n_exp // N_EXPERT_GROUPS
    ng = N_EXPERT_GROUPS

    @pl.when(pl.program_id(0) == 0)
    def _():
        run_scr[...] = jnp.zeros_like(run_scr)

    logits = lax.dot_general(wr_ref[...], h_ref[...], (((1,), (1,)), ((), ())), preferred_element_type=F32)
    scores = jax.nn.sigmoid(logits)
    biased = scores + bias_ref[...]
    sc = [scores[w * ng:(w + 1) * ng] for w in range(per)]
    bi = [biased[w * ng:(w + 1) * ng] for w in range(per)]
    assert per == 4
    hi01, lo01 = jnp.maximum(bi[0], bi[1]), jnp.minimum(bi[0], bi[1])
    hi23, lo23 = jnp.maximum(bi[2], bi[3]), jnp.minimum(bi[2], bi[3])
    group_score = jnp.maximum(hi01, hi23) + jnp.maximum(jnp.minimum(hi01, hi23), jnp.maximum(lo01, lo23))
    gidx = lax.broadcasted_iota(I32, (ng, tm), 0)
    grank = jnp.zeros((ng, tm), I32)
    for g2 in range(ng):
        row = group_score[g2:g2 + 1]
        grank += ((row > group_score) | ((row == group_score) & (g2 < gidx))).astype(I32)
    gsel = grank < TOPK_GROUPS
    cand = [jnp.where(gsel, b, -jnp.inf) for b in bi]
    eid = [gidx * per + w for w in range(per)]
    erank = [jnp.zeros((ng, tm), I32) for _ in range(per)]
    for w2 in range(per):
        for g2 in range(ng):
            row = cand[w2][g2:g2 + 1]
            e2 = g2 * per + w2
            for w in range(per):
                erank[w] += ((row > cand[w]) | ((row == cand[w]) & (e2 < eid[w]))).astype(I32)
    sel = [r < TOP_K for r in erank]
    ssum = functools.reduce(jnp.add, [jnp.sum(jnp.where(m, s, 0.0), axis=0, keepdims=True) for m, s in zip(sel, sc)])
    gate = [jnp.where(m, s / ssum * ROUTED_SCALE, 0.0) for m, s in zip(sel, sc)]
    sel_all = jnp.concatenate([m.astype(F32) for m in sel], axis=0)
    earlier = (lax.broadcasted_iota(I32, (tm, tm), 0) < lax.broadcasted_iota(I32, (tm, tm), 1)).astype(BF16)
    pos = jnp.dot(sel_all.astype(BF16), earlier, preferred_element_type=F32) + run_scr[...]
    run_scr[...] += jnp.sum(sel_all, axis=1, keepdims=True)
    cnt_ref[...] = run_scr[...].astype(I32)
    above = (lax.broadcasted_iota(I32, (n_exp, n_exp), 1) < lax.broadcasted_iota(I32, (n_exp, n_exp), 0)).astype(BF16)
    kidx = jnp.dot(above, sel_all.astype(BF16), preferred_element_type=F32)
    eid_all = jnp.concatenate(eid, axis=0).astype(F32)
    gate_all = jnp.concatenate(gate, axis=0)
    for k in range(TOP_K):
        mk = (sel_all > 0.0) & (kidx == float(k))
        ek_ref[k:k + 1, :] = jnp.sum(jnp.where(mk, eid_all, 0.0), axis=0, keepdims=True).astype(I32)
        pk_ref[k:k + 1, :] = jnp.sum(jnp.where(mk, pos, 0.0), axis=0, keepdims=True).astype(I32)
        gk_ref[k:k + 1, :] = jnp.sum(jnp.where(mk, gate_all, 0.0), axis=0, keepdims=True)


def moe_route(h, router_w, router_bias):
    t, d = h.shape
    n_exp = router_w.shape[1]
    per = n_exp // N_EXPERT_GROUPS
    wr = router_w.T.reshape(N_EXPERT_GROUPS, per, d).transpose(1, 0, 2).reshape(n_exp, d).astype(BF16)
    bias = router_bias.astype(F32).reshape(N_EXPERT_GROUPS, per).T.reshape(n_exp, 1)
    tm = ROUTER_TILE
    kspec = pl.BlockSpec((TOP_K, tm), lambda i: (0, i))
    ek, pk, gk, cnt = pl.pallas_call(
        functools.partial(_router_kernel, n_exp=n_exp), grid=(t // tm,),
        in_specs=[pl.BlockSpec((tm, d), lambda i: (i, 0)), pl.BlockSpec((n_exp, d), lambda i: (0, 0)),
                  pl.BlockSpec((n_exp, 1), lambda i: (0, 0))],
        out_specs=[kspec, kspec, kspec, pl.BlockSpec((n_exp, 1), lambda i: (0, 0))],
        out_shape=[jax.ShapeDtypeStruct((TOP_K, t), I32), jax.ShapeDtypeStruct((TOP_K, t), I32),
                   jax.ShapeDtypeStruct((TOP_K, t), F32), jax.ShapeDtypeStruct((n_exp, 1), I32)],
        scratch_shapes=[pltpu.VMEM((n_exp, 1), F32)],
        compiler_params=_params("arbitrary"), name="moe_router",
    )(h, wr, bias)
    counts = cnt.reshape(per, N_EXPERT_GROUPS).T.reshape(n_exp)
    return ek, pk, gk, counts


def _row_copy(src_ref, src_row, dst_ref, dst_row, sem):
    return pltpu.make_async_copy(src_ref.at[pl.ds(src_row, 1)], dst_ref.at[pl.ds(dst_row, 1)], sem)


def _rows_done(src_ref, dst_ref, sem):
    pltpu.make_async_copy(src_ref.at[pl.ds(0, dst_ref.shape[0])], dst_ref, sem).wait()


def _slot_token_kernel(slot_ref, pad_lo_ref, pad_hi_ref, tok_ref, *, n_tok, steps):
    j = pl.program_id(0)

    @pl.when(j == 0)
    def _():
        for e in range(pad_lo_ref.shape[0]):
            def clear(s, _):
                tok_ref[s] = 0
                return 0
            lax.fori_loop(pad_lo_ref[e], pad_hi_ref[e], clear, 0)

    per = n_tok // steps
    for k in range(TOP_K):
        def put(r, _, k=k):
            tok = j * per + r
            tok_ref[slot_ref[k * n_tok + tok]] = tok
            return 0
        lax.fori_loop(0, per, put, 0, unroll=8)


def moe_slot_tokens(slots, pad_lo, pad_hi, n_slots):
    n_tok = slots.shape[0] // TOP_K
    steps = 64
    assert n_tok % steps == 0
    return pl.pallas_call(
        functools.partial(_slot_token_kernel, n_tok=n_tok, steps=steps),
        grid_spec=pltpu.PrefetchScalarGridSpec(
            num_scalar_prefetch=3, grid=(steps,), in_specs=[],
            out_specs=pl.BlockSpec(memory_space=pltpu.SMEM)),
        out_shape=jax.ShapeDtypeStruct((n_slots,), I32),
        compiler_params=_params("arbitrary"), name="moe_slot_tokens",
    )(slots, pad_lo, pad_hi)


def _expert_kernel(te_ref, nused_ref, tok_ref, h_ref, wg_ref, wu_ref, wd_ref, ys_ref, xbuf, sem, *, te):
    i = pl.program_id(0)
    n_used = nused_ref[0]
    nbuf = GATHER_AHEAD + 1

    @pl.when(i == 0)
    def _():
        for tile in range(GATHER_AHEAD):
            @pl.when(tile < n_used)
            def _(tile=tile):
                def issue(r, _):
                    _row_copy(h_ref, tok_ref[tile * te + r], xbuf.at[tile], r, sem.at[tile]).start()
                    return 0
                lax.fori_loop(0, te, issue, 0, unroll=8)

    def mlp(fetch):
        cur = i % nbuf
        _rows_done(h_ref, xbuf.at[cur], sem.at[cur])
        if fetch:
            nxt = (i + GATHER_AHEAD) % nbuf
            for r in range(te):
                _row_copy(h_ref, tok_ref[(i + GATHER_AHEAD) * te + r], xbuf.at[nxt], r, sem.at[nxt]).start()
        x_lo, x_hi = [t.astype(BF16) for t in _unpack_row_halves(xbuf[cur])]
        half = x_lo.shape[1]
        lo, hi = pl.ds(0, half), pl.ds(half, half)
        g = (jnp.dot(x_lo, wg_ref[lo, :], preferred_element_type=F32)
             + jnp.dot(x_hi, wg_ref[hi, :], preferred_element_type=F32))
        u = (jnp.dot(x_lo, wu_ref[lo, :], preferred_element_type=F32)
             + jnp.dot(x_hi, wu_ref[hi, :], preferred_element_type=F32))
        a = (jax.nn.silu(g) * u).astype(BF16)
        ys_ref[...] = _pack_row_halves(jnp.dot(a, wd_ref[...], preferred_element_type=F32))

    pl.when(i + GATHER_AHEAD < n_used)(functools.partial(mlp, True))
    pl.when((i < n_used) & (i + GATHER_AHEAD >= n_used))(functools.partial(mlp, False))

    @pl.when(i >= n_used)
    def _():
        ys_ref[...] = jnp.zeros_like(ys_ref)


def moe_experts(h_rows, slot_tok, tile_expert, n_used, wg, wu, wd):
    dh = h_rows.shape[1]
    d = 2 * dh
    n_slots = slot_tok.shape[0]
    de = wg.shape[2]
    te = EXPERT_TILE
    return pl.pallas_call(
        functools.partial(_expert_kernel, te=te),
        grid_spec=pltpu.PrefetchScalarGridSpec(
            num_scalar_prefetch=3, grid=(n_slots // te,),
            in_specs=[pl.BlockSpec(memory_space=pl.ANY),
                      pl.BlockSpec((None, d, de), lambda i, e, n, tk: (e[i], 0, 0)),
                      pl.BlockSpec((None, d, de), lambda i, e, n, tk: (e[i], 0, 0)),
                      pl.BlockSpec((None, de, d), lambda i, e, n, tk: (e[i], 0, 0))],
            out_specs=pl.BlockSpec((te, dh), lambda i, e, n, tk: (i, 0)),
            scratch_shapes=[pltpu.VMEM((GATHER_AHEAD + 1, te, dh), jnp.uint32),
                            pltpu.SemaphoreType.DMA((GATHER_AHEAD + 1,))]),
        out_shape=jax.ShapeDtypeStruct((n_slots, dh), jnp.uint32),
        compiler_params=_params("arbitrary"), name="moe_experts",
    )(tile_expert, n_used, slot_tok, h_rows, wg, wu, wd)


def _combine_kernel(slot_ref, gk_ref, sh_ref, x_ref, gate_ref, ng_ref, ys_ref, o_ref, buf, sem, *, tm, n_tok):
    i = pl.program_id(0)
    nbuf = GATHER_AHEAD + 1

    @pl.when(i == 0)
    def _():
        for tile in range(GATHER_AHEAD):
            for k in range(TOP_K):
                def issue(r, _, k=k, tile=tile):
                    _row_copy(ys_ref, slot_ref[k * n_tok + tile * tm + r], buf.at[tile, k], r, sem.at[tile]).start()
                    return 0
                lax.fori_loop(0, tm, issue, 0, unroll=8)

    def tile(fetch):
        b = i % nbuf
        for k in range(TOP_K):
            _rows_done(ys_ref, buf.at[b, k], sem.at[b])
        if fetch:
            nxt = (i + GATHER_AHEAD) % nbuf
            for k in range(TOP_K):
                for r in range(tm):
                    src = slot_ref[k * n_tok + (i + GATHER_AHEAD) * tm + r]
                    _row_copy(ys_ref, src, buf.at[nxt, k], r, sem.at[nxt]).start()
        half = buf.shape[-1]
        lo, hi = pl.ds(0, half), pl.ds(half, half)
        y_lo, y_hi = sh_ref[:, lo].astype(F32), sh_ref[:, hi].astype(F32)
        gk = gk_ref[...]
        for k in range(TOP_K):
            e_lo, e_hi = _unpack_row_halves(buf[b, k])
            y_lo = y_lo + gk[:, k:k + 1] * e_lo
            y_hi = y_hi + gk[:, k:k + 1] * e_hi
        ssq = jnp.sum(y_lo * y_lo, axis=-1, keepdims=True) + jnp.sum(y_hi * y_hi, axis=-1, keepdims=True)
        inv = lax.rsqrt(ssq / (2 * half) + NORM_EPS)
        o_ref[:, lo] = x_ref[:, lo] + gate_ref[:, lo] * (y_lo * inv * ng_ref[:, lo])
        o_ref[:, hi] = x_ref[:, hi] + gate_ref[:, hi] * (y_hi * inv * ng_ref[:, hi])

    pl.when(i + GATHER_AHEAD < pl.num_programs(0))(functools.partial(tile, True))
    pl.when(i + GATHER_AHEAD >= pl.num_programs(0))(functools.partial(tile, False))


def moe_combine(x, ys, slots, gk_t, shared, norm_g, gate):
    bsz, seq, d = x.shape
    tm = COMBINE_TILE
    nt = seq // tm
    return pl.pallas_call(
        functools.partial(_combine_kernel, tm=tm, n_tok=bsz * seq),
        grid_spec=pltpu.PrefetchScalarGridSpec(
            num_scalar_prefetch=1, grid=(bsz * nt,),
            in_specs=[pl.BlockSpec((tm, TOP_K), lambda i, s: (i, 0)),
                      pl.BlockSpec((tm, d), lambda i, s: (i, 0)),
                      pl.BlockSpec((None, tm, d), lambda i, s: (i // nt, i % nt, 0)),
                      pl.BlockSpec((None, 1, d), lambda i, s: (i // nt, 0, 0)),
                      pl.BlockSpec((1, d), lambda i, s: (0, 0)),
                      pl.BlockSpec(memory_space=pl.ANY)],
            out_specs=pl.BlockSpec((None, tm, d), lambda i, s: (i // nt, i % nt, 0)),
            scratch_shapes=[pltpu.VMEM((GATHER_AHEAD + 1, TOP_K, tm, d // 2), jnp.uint32),
                            pltpu.SemaphoreType.DMA((GATHER_AHEAD + 1,))]),
        out_shape=jax.ShapeDtypeStruct(x.shape, F32),
        compiler_params=_params("arbitrary"), name="moe_combine",
    )(slots, gk_t, shared, x, gate.reshape(bsz, 1, d), norm_g.reshape(1, d), ys)


def moe_layer(x, h, h_rows, norm_g, gate, router_w, router_bias, e_gate, e_up, e_down, s_gate, s_up, s_down):
    t, d = h.shape
    n_exp = router_w.shape[1]
    te = EXPERT_TILE
    ek, pk, gk, counts = moe_route(h, router_w, router_bias)
    tiles = (counts + te - 1) // te
    tile_end = jnp.cumsum(tiles)
    offsets = (tile_end - tiles) * te
    experts = jnp.arange(n_exp, dtype=I32)
    slot = pk + jnp.sum(jnp.where(ek[..., None] == experts, offsets, 0), axis=-1)
    slots = slot.reshape(-1).astype(I32)
    n_tiles = (t * TOP_K) // te + n_exp
    n_used = tile_end[-1].astype(I32)
    tile_ids = jnp.arange(n_tiles, dtype=I32)
    tile_expert = jnp.sum((tile_end[None, :] <= jnp.minimum(tile_ids, n_used - 1)[:, None]).astype(I32), axis=1)
    pad_lo = jnp.concatenate([offsets + counts, tile_end[-1:] * te]).astype(I32)
    pad_hi = jnp.concatenate([tile_end * te, jnp.full((1,), n_tiles * te, tile_end.dtype)]).astype(I32)
    slot_tok = moe_slot_tokens(slots, pad_lo, pad_hi, n_tiles * te)
    ys = moe_experts(h_rows, slot_tok, tile_expert, n_used.reshape(1), e_gate, e_up, e_down)
    act = matmul_swiglu(h, s_gate, s_up, tm=1024, tn=512, name="shared_up")
    shared = matmul(act, s_down, BF16, tm=1024, tn=1024, name="shared_down")
    return moe_combine(x, ys, slots, gk.T, shared, norm_g, gate)


def kernel(x, c, positions, ada_w, ada_b, norm_g, ssm_w_in, ssm_a_re, ssm_a_im, ssm_log_step, ssm_b_re, ssm_b_im, ssm_c_re, ssm_c_im, ssm_d, glu_w, glu_b, ssm_w_out, kv_norm_g, w_k, w_v, attn_w_q, attn_w_o, router_w, router_bias, exp_w_gate, exp_w_up, exp_w_down, sh_w_gate, sh_w_up, sh_w_down):
    bsz, seq, d = x.shape
    depth = ada_w.shape[0]
    n_a = ssm_w_in.shape[0]
    t = bsz * seq
    mod = ada_modulation(c, ada_w, ada_b)
    k = v = None
    for layer in range(depth):
        sh1, sc1, g1, sh2, sc2, g2 = jnp.split(mod[layer], N_MOD, axis=-1)
        if layer < n_a:
            a = layer
            (h,) = norm_modulate(x, norm_g[layer, 0], sc1, sh1)
            u_tm = matmul_to_token_major(h, cast_bf16(ssm_w_in, a), bsz, tm=256, tn=512, name="ssm_in")
            bblk, cblk, ab = s5_operators(ssm_a_re[a], ssm_a_im[a], ssm_log_step[a], ssm_b_re[a], ssm_b_im[a],
                                          ssm_c_re[a], ssm_c_im[a])
            z = s5_scan(u_tm, bblk, cblk, ab, ssm_d[a], bsz)
            zz = matmul_glu(z, cast_bf16(glu_w, a), glu_b[a], tm=1024, tn=512, name="ssm_glu")
            y = matmul_from_token_major(zz, cast_bf16(ssm_w_out, a), bsz, BF16, tm=256, tn=512, name="ssm_out")
        else:
            bl = layer - n_a
            n_dil = attn_w_q.shape[2] // d
            assert all(window // dil == ATTN_BLOCK for window, dil in DILATION_GROUPS[:n_dil])
            dils = tuple(dil for _, dil in DILATION_GROUPS[:n_dil])
            if layer == n_a:
                h, hkv = norm_modulate(x, norm_g[layer, 0], sc1, sh1, kv_g=kv_norm_g)
                kvd = w_k.shape[1]
                k_tabs = rope_tables(positions, 1.0)
                k = matmul_streams(hkv, cast_bf16(w_k[None], 0), 0, kvd, dils, k_tabs, tn=min(kvd, 512), name="k_proj")
                v = matmul_streams(hkv, cast_bf16(w_v[None], 0), 0, kvd, dils, None, tn=min(kvd, 512), name="v_proj")
            else:
                (h,) = norm_modulate(x, norm_g[layer, 0], sc1, sh1)
            q_tabs = rope_tables(positions, HEAD_DIM ** -0.5)
            wq = cast_bf16(attn_w_q, bl)
            tnq = min(d, 256)
            qs = [matmul_streams(h, wq, grp * (d // tnq), d, (dil,), q_tabs, tn=tnq, name=f"q_proj_{dil}")[0]
                  for grp, dil in enumerate(dils)]
            o = dilated_attention(qs, k, v, bsz, dils)
            y = matmul(o, cast_bf16(attn_w_o, bl), BF16, tm=1024, tn=512, name="attn_out")
        x, h, h_rows = norm_modulate(x, norm_g[layer, 2], sc2, sh2, with_rows=True,
                                     resid=(y, norm_g[layer, 1], g1))
        x = moe_layer(x, h, h_rows, norm_g[layer, 3], g2, router_w[layer], router_bias[layer],
                      cast_bf16(exp_w_gate, layer), cast_bf16(exp_w_up, layer), cast_bf16(exp_w_down, layer),
                      cast_bf16(sh_w_gate, layer), cast_bf16(sh_w_up, layer), cast_bf16(sh_w_down, layer))
    return x
```

```python
import functools
import math

import jax
import jax.numpy as jnp
from jax import lax
from jax.experimental import pallas as pl
from jax.experimental.pallas import tpu as pltpu

F32 = jnp.float32
BF16 = jnp.bfloat16
I32 = jnp.int32

LANE = 128
SUBLANE = 8
VMEM_LIMIT_BYTES = 56 * 1024 * 1024
MXU_DIM = 256

N_MOD = 6
NORM_EPS = 1e-6
HEAD_DIM = 128
ROT_DIM = HEAD_DIM // 4
ROT_HALF = ROT_DIM // 2
ROPE_THETA = 500000.0
DILATION_GROUPS = ((128, 1), (512, 4), (2048, 16))
ATTN_BLOCK = 128
ATTN_SPAN = ATTN_BLOCK * max(dil for _, dil in DILATION_GROUPS)
N_EXPERT_GROUPS = 8
TOPK_GROUPS = 4
TOP_K = 4
ROUTED_SCALE = 2.5
NEG_MASK = -1e30

SSM_PACK_CH = MXU_DIM
SSM_CHUNK_TOKENS = 256
SSM_LANE_CHUNK = 512
EXPERT_TILE = 256
ROW_TILE = 256
COMBINE_TILE = 128
GATHER_AHEAD = 2
ROUTER_TILE = 512
CAST_BLOCK_BYTES = 4 * 1024 * 1024


def _params(*sem):
    return pltpu.CompilerParams(dimension_semantics=sem, vmem_limit_bytes=VMEM_LIMIT_BYTES)


def _cast_kernel(x_ref, o_ref):
    o_ref[...] = x_ref[...].astype(o_ref.dtype)


def cast_bf16(w, layer):
    lead, c = w.shape[1:-1], w.shape[-1]
    rows = math.prod(lead)
    w3 = w.reshape(w.shape[0], rows, c)
    br = min(rows, 1 << ((CAST_BLOCK_BYTES // (4 * c)).bit_length() - 1))
    assert rows % br == 0 and br % (2 * SUBLANE) == 0
    out = pl.pallas_call(
        _cast_kernel, grid=(rows // br,),
        in_specs=[pl.BlockSpec((None, br, c), lambda i: (layer, i, 0))],
        out_specs=pl.BlockSpec((br, c), lambda i: (i, 0)),
        out_shape=jax.ShapeDtypeStruct((rows, c), BF16),
        compiler_params=_params("parallel"), name="cast_bf16",
    )(w3)
    return out.reshape(*lead, c)


def _ada_kernel(c_ref, w_ref, b_ref, o_ref):
    cond = jax.nn.silu(c_ref[...])
    o_ref[...] = jnp.dot(cond, w_ref[...], precision=lax.Precision.HIGHEST,
                         preferred_element_type=F32) + b_ref[...]


def ada_modulation(c, ada_w, ada_b):
    depth, d, n = ada_w.shape
    bsz = c.shape[0]
    rows = -(-bsz // SUBLANE) * SUBLANE
    cp = jnp.zeros((rows, d), F32).at[:bsz].set(c)
    tn = 512
    out = pl.pallas_call(
        _ada_kernel,
        grid=(depth, n // tn),
        in_specs=[pl.BlockSpec((rows, d), lambda l, j: (0, 0)),
                  pl.BlockSpec((None, d, tn), lambda l, j: (l, 0, j)),
                  pl.BlockSpec((None, 1, tn), lambda l, j: (l, 0, j))],
        out_specs=pl.BlockSpec((None, rows, tn), lambda l, j: (l, 0, j)),
        out_shape=jax.ShapeDtypeStruct((depth, rows, n), F32),
        compiler_params=_params("parallel", "parallel"),
        name="ada_modulation",
    )(cp, ada_w, ada_b.reshape(depth, 1, n))
    return out[:, :bsz]


def _bf16_bits(v):
    return lax.bitcast_convert_type(v.astype(BF16).astype(F32), jnp.uint32)


def _pack_row_halves(v):
    half = v.shape[1] // 2
    return (_bf16_bits(v[:, :half]) >> 16) | _bf16_bits(v[:, half:])


def _unpack_row_halves(p):
    lo = lax.bitcast_convert_type(p << 16, F32)
    hi = lax.bitcast_convert_type(p & jnp.uint32(0xFFFF0000), F32)
    return lo, hi


def _normmod_kernel(x_ref, g_ref, sc_ref, sh_ref, *rest, with_kv, with_rows, with_resid):
    x = x_ref[...]
    rest = list(rest)
    if with_resid:
        y_ref, rg_ref, gate_ref = rest.pop(0), rest.pop(0), rest.pop(0)
        y = y_ref[...].astype(F32)
        x = x + gate_ref[...] * (y * lax.rsqrt(jnp.mean(y * y, axis=-1, keepdims=True) + NORM_EPS) * rg_ref[...])
    xn = x * lax.rsqrt(jnp.mean(x * x, axis=-1, keepdims=True) + NORM_EPS)
    h = xn * g_ref[...] * (1.0 + sc_ref[...]) + sh_ref[...]
    if with_kv:
        kvg_ref = rest.pop(0)
    if with_resid:
        rest.pop(0)[...] = x
    h_ref = rest.pop(0)
    h_ref[...] = h.astype(h_ref.dtype)
    if with_kv:
        rest.pop(0)[...] = (xn * kvg_ref[...]).astype(BF16)
    if with_rows:
        rest.pop(0)[...] = _pack_row_halves(h)


def norm_modulate(x, g, sc, sh, kv_g=None, with_rows=False, resid=None):
    bsz, seq, d = x.shape
    ts = ROW_TILE
    nt = seq // ts
    row = lambda b, i: (b * nt + i, 0)
    vec = pl.BlockSpec((1, d), lambda b, i: (0, 0))
    bvec = pl.BlockSpec((None, 1, d), lambda b, i: (b, 0, 0))
    xspec = pl.BlockSpec((None, ts, d), lambda b, i: (b, i, 0))
    in_specs = [xspec, vec, bvec, bvec]
    args = [x, g.reshape(1, d), sc.reshape(bsz, 1, d), sh.reshape(bsz, 1, d)]
    out_specs = [pl.BlockSpec((ts, d), row)]
    out_shape = [jax.ShapeDtypeStruct((bsz * seq, d), BF16)]
    if resid is not None:
        y, rg, gate = resid
        in_specs += [pl.BlockSpec((ts, d), row), vec, bvec]
        args += [y, rg.reshape(1, d), gate.reshape(bsz, 1, d)]
        out_specs.insert(0, xspec)
        out_shape.insert(0, jax.ShapeDtypeStruct(x.shape, F32))
    if kv_g is not None:
        in_specs.append(vec)
        args.append(kv_g.reshape(1, d))
        out_specs.append(pl.BlockSpec((ts, d), row))
        out_shape.append(jax.ShapeDtypeStruct((bsz * seq, d), BF16))
    if with_rows:
        out_specs.append(pl.BlockSpec((ts, d // 2), row))
        out_shape.append(jax.ShapeDtypeStruct((bsz * seq, d // 2), jnp.uint32))
    return pl.pallas_call(
        functools.partial(_normmod_kernel, with_kv=kv_g is not None, with_rows=with_rows,
                          with_resid=resid is not None),
        grid=(bsz, nt), in_specs=in_specs, out_specs=out_specs, out_shape=out_shape,
        compiler_params=_params("parallel", "parallel"), name="norm_modulate",
    )(*args)


def _mm_kernel(a_ref, w_ref, o_ref):
    o_ref[...] = jnp.dot(a_ref[...], w_ref[...], preferred_element_type=F32).astype(o_ref.dtype)


def _mm_glu_kernel(a_ref, w_ref, z_ref, b_ref, o_ref):
    acc = jnp.dot(a_ref[...], w_ref[...], preferred_element_type=F32) + b_ref[...]
    o_ref[...] = (z_ref[...].astype(F32) * jax.nn.sigmoid(acc)).astype(o_ref.dtype)


def _mm_swiglu_kernel(a_ref, wg_ref, wu_ref, o_ref):
    a = a_ref[...]
    g = jnp.dot(a, wg_ref[...], preferred_element_type=F32)
    u = jnp.dot(a, wu_ref[...], preferred_element_type=F32)
    o_ref[...] = (jax.nn.silu(g) * u).astype(o_ref.dtype)


def matmul(a, w, out_dtype, *, tm, tn, name):
    m, k = a.shape
    n = w.shape[1]
    tm, tn = min(tm, m), min(tn, n)
    return pl.pallas_call(
        _mm_kernel, grid=(m // tm, n // tn),
        in_specs=[pl.BlockSpec((tm, k), lambda i, j: (i, 0)), pl.BlockSpec((k, tn), lambda i, j: (0, j))],
        out_specs=pl.BlockSpec((tm, tn), lambda i, j: (i, j)),
        out_shape=jax.ShapeDtypeStruct((m, n), out_dtype),
        compiler_params=_params("parallel", "arbitrary"), name=name,
    )(a, w)


def matmul_swiglu(a, wg, wu, *, tm, tn, name):
    m, k = a.shape
    n = wg.shape[1]
    tm, tn = min(tm, m), min(tn, n)
    wspec = pl.BlockSpec((k, tn), lambda i, j: (0, j))
    return pl.pallas_call(
        _mm_swiglu_kernel, grid=(m // tm, n // tn),
        in_specs=[pl.BlockSpec((tm, k), lambda i, j: (i, 0)), wspec, wspec],
        out_specs=pl.BlockSpec((tm, tn), lambda i, j: (i, j)),
        out_shape=jax.ShapeDtypeStruct((m, n), BF16),
        compiler_params=_params("parallel", "arbitrary"), name=name,
    )(a, wg, wu)


def matmul_glu(z, w, b, *, tm, tn, name):
    m, k = z.shape
    n = w.shape[1]
    tm, tn = min(tm, m), min(tn, n)
    return pl.pallas_call(
        _mm_glu_kernel, grid=(m // tm, n // tn),
        in_specs=[pl.BlockSpec((tm, k), lambda i, j: (i, 0)), pl.BlockSpec((k, tn), lambda i, j: (0, j)),
                  pl.BlockSpec((tm, tn), lambda i, j: (i, j)), pl.BlockSpec((1, tn), lambda i, j: (0, j))],
        out_specs=pl.BlockSpec((tm, tn), lambda i, j: (i, j)),
        out_shape=jax.ShapeDtypeStruct((m, n), BF16),
        compiler_params=_params("parallel", "arbitrary"), name=name,
    )(z, w, z, b.reshape(1, n))


def _mm_to_token_major_kernel(a_ref, w_ref, o_ref, scr):
    bsz, tm, _ = a_ref.shape
    for b in range(bsz):
        acc = jnp.dot(a_ref[b], w_ref[...], preferred_element_type=F32)
        for j in range(scr.shape[0]):
            scr[j, pl.ds(b, tm, stride=bsz), :] = acc[:, j * LANE:(j + 1) * LANE]
    for j in range(scr.shape[0]):
        o_ref[:, j * LANE:(j + 1) * LANE] = scr[j].astype(o_ref.dtype)


def _mm_from_token_major_kernel(a_ref, w_ref, o_ref, scr):
    bsz, tm, _ = o_ref.shape
    acc = jnp.dot(a_ref[...], w_ref[...], preferred_element_type=F32)
    for j in range(scr.shape[0]):
        scr[j] = acc[:, j * LANE:(j + 1) * LANE]
    for b in range(bsz):
        for j in range(scr.shape[0]):
            o_ref[b, :, j * LANE:(j + 1) * LANE] = scr[j, pl.ds(b, tm, stride=bsz), :].astype(o_ref.dtype)


def matmul_to_token_major(a, w, bsz, *, tm, tn, name):
    m, k = a.shape
    n = w.shape[1]
    seq = m // bsz
    tm, tn = min(tm, seq), min(tn, n)
    return pl.pallas_call(
        _mm_to_token_major_kernel, grid=(seq // tm, n // tn),
        in_specs=[pl.BlockSpec((bsz, tm, k), lambda i, j: (0, i, 0)),
                  pl.BlockSpec((k, tn), lambda i, j: (0, j))],
        out_specs=pl.BlockSpec((bsz * tm, tn), lambda i, j: (i, j)),
        out_shape=jax.ShapeDtypeStruct((m, n), F32),
        scratch_shapes=[pltpu.VMEM((tn // LANE, bsz * tm, LANE), F32)],
        compiler_params=_params("parallel", "arbitrary"), name=name,
    )(a.reshape(bsz, seq, k), w)


def matmul_from_token_major(a, w, bsz, out_dtype, *, tm, tn, name):
    m, k = a.shape
    n = w.shape[1]
    seq = m // bsz
    tm, tn = min(tm, seq), min(tn, n)
    out = pl.pallas_call(
        _mm_from_token_major_kernel, grid=(seq // tm, n // tn),
        in_specs=[pl.BlockSpec((bsz * tm, k), lambda i, j: (i, 0)),
                  pl.BlockSpec((k, tn), lambda i, j: (0, j))],
        out_specs=pl.BlockSpec((bsz, tm, tn), lambda i, j: (0, i, j)),
        out_shape=jax.ShapeDtypeStruct((bsz, seq, n), out_dtype),
        scratch_shapes=[pltpu.VMEM((tn // LANE, bsz * tm, LANE), F32)],
        compiler_params=_params("parallel", "arbitrary"), name=name,
    )(a, w)
    return out.reshape(m, n)


def _s5_discretize_kernel(are_ref, aim_ref, ls_ref, abr_ref, abi_ref, fr_ref, fi_ref):
    lr, li = are_ref[...], aim_ref[...]
    step = jnp.exp(ls_ref[...])
    decay = jnp.exp(lr * step)
    abar_re = decay * jnp.cos(li * step)
    abar_im = decay * jnp.sin(li * step)
    denom = lr * lr + li * li
    nr = abar_re - 1.0
    ni = abar_im
    abr_ref[...] = abar_re
    abi_ref[...] = abar_im
    fr_ref[...] = (nr * lr + ni * li) / denom
    fi_ref[...] = (ni * lr - nr * li) / denom


def s5_discretize(a_re, a_im, log_step):
    g, p = a_re.shape
    shp = jax.ShapeDtypeStruct((g, p), F32)
    return pl.pallas_call(_s5_discretize_kernel, out_shape=[shp] * 4, name="s5_discretize")(
        a_re, a_im, log_step.reshape(g, 1))


def _s5_scan_kernel(un_ref, up_ref, bblk_ref, cblk_ref, ab_ref, d_ref, z_ref, x2_scr, xb2_scr, carry_scr,
                    *, rows, nstate, bsz):
    c = pl.program_id(1)
    n_chunks = pl.num_programs(1) - 1

    @pl.when(c == 0)
    def _():
        carry_scr[...] = jnp.zeros_like(carry_scr)
        xb2_scr[1] = jnp.zeros(xb2_scr.shape[1:], xb2_scr.dtype)
        x2_scr[0] = jnp.dot(up_ref[...].astype(BF16), bblk_ref[...], preferred_element_type=F32)

    u_prev = up_ref[...]
    x2_scr[(c + 1) % 2] = jnp.dot(un_ref[...].astype(BF16), bblk_ref[...], preferred_element_type=F32)
    y = jnp.dot(xb2_scr[(c + 1) % 2], cblk_ref[...], preferred_element_type=F32) + d_ref[...] * u_prev
    z_ref[...] = jax.nn.gelu(y).astype(z_ref.dtype)

    @pl.when(c < n_chunks)
    def _():
        _s5_recurrence(x2_scr.at[c % 2], xb2_scr.at[c % 2], ab_ref, carry_scr, rows=rows, nstate=nstate, bsz=bsz)


def _s5_recurrence(x_scr, xb_scr, ab_ref, carry_scr, *, rows, nstate, bsz):
    lw = SSM_LANE_CHUNK
    second = lax.broadcasted_iota(I32, (SUBLANE, lw), 0) >= bsz
    for lt in range(nstate // lw):
        re = slice(lt * lw, (lt + 1) * lw)
        im = slice(nstate + lt * lw, nstate + (lt + 1) * lw)
        ar, ai = ab_ref[0:1, re], ab_ref[1:2, re]
        a2r, a2i = ar * ar - ai * ai, 2.0 * ar * ai
        amr, ami = jnp.where(second, ar, 0.0), jnp.where(second, ai, 0.0)
        pr, pi = jnp.where(second, a2r, ar), jnp.where(second, a2i, ai)

        def advance(r0, cr, ci, re=re, im=im, amr=amr, ami=ami, pr=pr, pi=pi):
            bur, bui = x_scr[pl.ds(r0, SUBLANE), re], x_scr[pl.ds(r0, SUBLANE), im]
            sr, si = pltpu.roll(bur, bsz, 0), pltpu.roll(bui, bsz, 0)
            wr = bur + amr * sr - ami * si
            wi = bui + amr * si + ami * sr
            cbr = jnp.where(second, cr, pltpu.roll(cr, bsz, 0))
            cbi = jnp.where(second, ci, pltpu.roll(ci, bsz, 0))
            return wr + pr * cbr - pi * cbi, wi + pr * cbi + pi * cbr

        def body(n, carry, re=re, im=im, advance=advance):
            r0 = pl.multiple_of(n * (2 * SUBLANE), 2 * SUBLANE)
            xr0, xi0 = advance(r0, *carry)
            xr1, xi1 = advance(r0 + SUBLANE, xr0, xi0)
            xb_scr[pl.ds(r0, 2 * SUBLANE), re] = jnp.concatenate([xr0, xr1], axis=0).astype(BF16)
            xb_scr[pl.ds(r0, 2 * SUBLANE), im] = jnp.concatenate([xi0, xi1], axis=0).astype(BF16)
            return xr1, xi1

        cr, ci = lax.fori_loop(0, rows // (2 * SUBLANE), body, (carry_scr[:, re], carry_scr[:, im]))
        carry_scr[:, re] = cr
        carry_scr[:, im] = ci


def s5_scan(u_tm, bblk, cblk, ab, d_skip, bsz):
    m, width = u_tm.shape
    assert 2 * bsz == SUBLANE, "the scan tiles two tokens of every batch element into one sublane tile"
    packs, pack_ch, two_n = bblk.shape
    nstate = two_n // 2
    rows = SSM_CHUNK_TOKENS * bsz
    n_chunks = m // rows
    nxt = lambda p, c: (jnp.minimum(c + 1, n_chunks - 1), p)
    prv = lambda p, c: (jnp.clip(c - 1, 0, n_chunks - 1), p)
    return pl.pallas_call(
        functools.partial(_s5_scan_kernel, rows=rows, nstate=nstate, bsz=bsz),
        grid=(packs, n_chunks + 1),
        in_specs=[pl.BlockSpec((rows, pack_ch), nxt), pl.BlockSpec((rows, pack_ch), prv),
                  pl.BlockSpec((None, pack_ch, two_n), lambda p, c: (p, 0, 0)),
                  pl.BlockSpec((None, two_n, pack_ch), lambda p, c: (p, 0, 0)),
                  pl.BlockSpec((None, 2, nstate), lambda p, c: (p, 0, 0)),
                  pl.BlockSpec((None, 1, pack_ch), lambda p, c: (p, 0, 0))],
        out_specs=pl.BlockSpec((rows, pack_ch), prv),
        out_shape=jax.ShapeDtypeStruct((m, width), BF16),
        scratch_shapes=[pltpu.VMEM((2, rows, two_n), F32), pltpu.VMEM((2, rows, two_n), BF16),
                        pltpu.VMEM((SUBLANE, two_n), F32)],
        compiler_params=_params("parallel", "arbitrary"), name="s5_scan",
    )(u_tm, u_tm, bblk, cblk, ab, d_skip.reshape(packs, 1, pack_ch))


def s5_operators(a_re, a_im, log_step, b_re, b_im, c_re, c_im):
    g, p = a_re.shape
    gsz = b_re.shape[-1]
    gpp = SSM_PACK_CH // gsz
    packs = g // gpp
    abr, abi, fr, fi = s5_discretize(a_re, a_im, log_step)
    bbar_re = fr[..., None] * b_re - fi[..., None] * b_im
    bbar_im = fr[..., None] * b_im + fi[..., None] * b_re
    eye = jnp.eye(gpp, dtype=F32)

    def b_block(t):
        t = t.reshape(packs, gpp, p, gsz).transpose(0, 1, 3, 2)
        return (t[:, :, :, None, :] * eye[None, :, None, :, None]).reshape(packs, gpp * gsz, gpp * p)

    def c_block(t):
        t = t.reshape(packs, gpp, gsz, p).transpose(0, 1, 3, 2)
        return (t[:, :, :, None, :] * eye[None, :, None, :, None]).reshape(packs, gpp * p, gpp * gsz)

    bblk = jnp.concatenate([b_block(bbar_re), b_block(bbar_im)], axis=2).astype(BF16)
    cblk = jnp.concatenate([c_block(c_re), -c_block(c_im)], axis=1).astype(BF16)
    ab = jnp.stack([abr.reshape(packs, gpp * p), abi.reshape(packs, gpp * p)], axis=1)
    return bblk, cblk, ab


def _rope_table_kernel(pos_ref, invf_ref, cos_ref, sin_lo_ref, sin_hi_ref, *, scale):
    ang = pos_ref[...].astype(F32) * invf_ref[...]
    lane = lax.broadcasted_iota(I32, ang.shape, 1)
    c, s = jnp.cos(ang), jnp.sin(ang)
    cos_ref[...] = jnp.where(lane < ROT_DIM, c, 1.0) * scale
    sin_lo_ref[...] = jnp.where(lane < ROT_HALF, -s, 0.0) * scale
    sin_hi_ref[...] = jnp.where((lane >= ROT_HALF) & (lane < ROT_DIM), s, 0.0) * scale


def rope_tables(positions, scale):
    t = positions.size
    inv_freq = ROPE_THETA ** (-jnp.arange(ROT_HALF, dtype=F32) * 2.0 / ROT_DIM)
    invf = jnp.zeros((1, HEAD_DIM), F32).at[0, :ROT_DIM].set(jnp.concatenate([inv_freq, inv_freq]))
    ts = 1024
    shp = jax.ShapeDtypeStruct((t, HEAD_DIM), F32)
    spec = pl.BlockSpec((ts, HEAD_DIM), lambda i: (i, 0))
    return pl.pallas_call(
        functools.partial(_rope_table_kernel, scale=scale), grid=(t // ts,),
        in_specs=[pl.BlockSpec((ts, 1), lambda i: (i, 0)), pl.BlockSpec((1, HEAD_DIM), lambda i: (0, 0))],
        out_specs=[spec] * 3, out_shape=[shp] * 3,
        compiler_params=_params("parallel"), name="rope_tables",
    )(positions.reshape(t, 1), invf)


def _stream_rows(chunk, residue, dil):
    start = chunk * (ATTN_BLOCK * dil) + residue
    return pl.ds(start, ATTN_BLOCK) if dil == 1 else pl.ds(start, ATTN_BLOCK, stride=dil)


def _stream_major_store(o_ref, src_ref, dil):
    span = ATTN_BLOCK * dil
    for hd in range(src_ref.shape[0]):
        cols = slice(hd * HEAD_DIM, (hd + 1) * HEAD_DIM)
        for chunk in range(src_ref.shape[1] // span):
            for residue in range(dil):
                dst = pl.ds(chunk * span + residue * ATTN_BLOCK, ATTN_BLOCK)
                o_ref[dst, cols] = src_ref[hd, _stream_rows(chunk, residue, dil), :].astype(o_ref.dtype)


def _mm_streams_kernel(a_ref, w_ref, *rest, dils, rotary):
    rest = list(rest)
    if rotary:
        cos, sin_lo, sin_hi = [rest.pop(0)[...] for _ in range(3)]
    o_refs, scr = rest[:len(dils)], rest[len(dils)]
    acc = jnp.dot(a_ref[...], w_ref[...], preferred_element_type=F32)
    for hd in range(acc.shape[1] // HEAD_DIM):
        t = acc[:, hd * HEAD_DIM:(hd + 1) * HEAD_DIM]
        if rotary:
            t = t * cos + pltpu.roll(t, HEAD_DIM - ROT_HALF, 1) * sin_lo + pltpu.roll(t, ROT_HALF, 1) * sin_hi
        scr[hd] = t
    for o_ref, dil in zip(o_refs, dils):
        _stream_major_store(o_ref, scr, dil)


def matmul_streams(a, w, col_block0, n, dils, tables, *, tn, name):
    m, k = a.shape
    tm = ATTN_SPAN
    rotary = tables is not None
    in_specs = [pl.BlockSpec((tm, k), lambda i, j: (i, 0), pipeline_mode=pl.Buffered(1)),
                pl.BlockSpec((k, tn), lambda i, j: (0, col_block0 + j))]
    args = [a, w]
    if rotary:
        in_specs += [pl.BlockSpec((tm, HEAD_DIM), lambda i, j: (i, 0))] * 3
        args += list(tables)
    return pl.pallas_call(
        functools.partial(_mm_streams_kernel, dils=dils, rotary=rotary), grid=(m // tm, n // tn),
        in_specs=in_specs,
        out_specs=[pl.BlockSpec((tm, tn), lambda i, j: (i, j))] * len(dils),
        out_shape=[jax.ShapeDtypeStruct((m, n), BF16)] * len(dils),
        scratch_shapes=[pltpu.VMEM((tn // HEAD_DIM, tm, HEAD_DIM), F32)],
        compiler_params=_params("parallel", "arbitrary"), name=name,
    )(*args)


def _attn_kernel(*refs, dils):
    n = len(dils)
    q_refs = refs[:n]
    kv_refs = refs[n:5 * n]
    o_ref, o_scr, l_scr = refs[5 * n], refs[5 * n + 1], refs[5 * n + 2]
    blk = ATTN_BLOCK
    nb = ATTN_SPAN // blk
    seq_start = pl.program_id(1) == 0
    qi = lax.broadcasted_iota(I32, (nb, blk, blk), 1)
    kj = lax.broadcasted_iota(I32, (nb, blk, blk), 2)
    bi = lax.broadcasted_iota(I32, (nb, blk, blk), 0)
    cur_ok = kj <= qi
    qk = (((2,), (2,)), ((0,), (0,)))
    pv = (((2,), (1,)), ((0,), (0,)))
    for g, dil in enumerate(dils):
        kp_ref, kc_ref, vp_ref, vc_ref = kv_refs[4 * g:4 * g + 4]
        span = blk * dil
        kc, vc = kc_ref[...], vc_ref[...]
        if span == ATTN_SPAN:
            kp, vp = kp_ref[...], vp_ref[...]
        else:
            kp = jnp.concatenate([kp_ref[...], kc[:ATTN_SPAN - span]], axis=0)
            vp = jnp.concatenate([vp_ref[...], vc[:ATTN_SPAN - span]], axis=0)
        split = lambda t: t.reshape(nb, blk, HEAD_DIM)
        q3, kp3, kc3, vp3, vc3 = split(q_refs[g][...]), split(kp), split(kc), split(vp), split(vc)
        prev_ok = (kj >= qi) & jnp.logical_not(seq_start & (bi < dil))
        sp = jnp.where(prev_ok, lax.dot_general(q3, kp3, qk, preferred_element_type=F32), NEG_MASK)
        sc = jnp.where(cur_ok, lax.dot_general(q3, kc3, qk, preferred_element_type=F32), NEG_MASK)
        mx = jnp.maximum(jnp.max(sp, axis=-1, keepdims=True), jnp.max(sc, axis=-1, keepdims=True))
        pp, pc = jnp.exp(sp - mx), jnp.exp(sc - mx)
        den = jnp.sum(pp, axis=-1, keepdims=True) + jnp.sum(pc, axis=-1, keepdims=True)
        o = (lax.dot_general((pp / den).astype(BF16), vp3, pv, preferred_element_type=F32)
             + lax.dot_general((pc / den).astype(BF16), vc3, pv, preferred_element_type=F32))
        lse = jnp.broadcast_to(mx + jnp.log(den), (nb, blk, HEAD_DIM))
        for b in range(nb):
            dst = _stream_rows(b // dil, b % dil, dil)
            o_scr[g, dst, :] = o[b]
            l_scr[g, dst, :] = lse[b]
    ls = [l_scr[g] for g in range(n)]
    mx = functools.reduce(jnp.maximum, ls)
    es = [jnp.exp(l - mx) for l in ls]
    tot = functools.reduce(jnp.add, es)
    acc = o_scr[0] * (es[0] / tot)
    for g in range(1, n):
        acc = acc + o_scr[g] * (es[g] / tot)
    o_ref[...] = acc.astype(o_ref.dtype)


def dilated_attention(qs, ks, vs, bsz, dils):
    t, d = qs[0].shape
    kvd = ks[0].shape[1]
    rep = d // kvd
    seq = t // bsz
    assert seq % ATTN_SPAN == 0
    ns = seq // ATTN_SPAN
    cur = lambda b, c, h: (b * ns + c, h // rep)
    in_specs = [pl.BlockSpec((ATTN_SPAN, HEAD_DIM), lambda b, c, h: (b * ns + c, h))] * len(dils)
    args = list(qs)
    for g, dil in enumerate(dils):
        span = ATTN_BLOCK * dil
        per = ATTN_SPAN // span
        prev = lambda b, c, h, per=per: (jnp.maximum((b * ns + c) * per - 1, 0), h // rep)
        in_specs += [pl.BlockSpec((span, HEAD_DIM), prev), pl.BlockSpec((ATTN_SPAN, HEAD_DIM), cur)] * 2
        args += [ks[g], ks[g], vs[g], vs[g]]
    return pl.pallas_call(
        functools.partial(_attn_kernel, dils=dils),
        grid=(bsz, ns, d // HEAD_DIM),
        in_specs=in_specs,
        out_specs=pl.BlockSpec((ATTN_SPAN, HEAD_DIM), lambda b, c, h: (b * ns + c, h)),
        out_shape=jax.ShapeDtypeStruct((t, d), BF16),
        scratch_shapes=[pltpu.VMEM((len(dils), ATTN_SPAN, HEAD_DIM), F32)] * 2,
        compiler_params=_params("parallel", "arbitrary", "arbitrary"), name="dilated_attention",
    )(*args)


def _router_kernel(h_ref, wr_ref, bias_ref, ek_ref, pk_ref, gk_ref, cnt_ref, run_scr, *, n_exp):
    tm = h_ref.shape[0]
    per = n_exp // N_EXPERT_GROUPS
    ng = N_EXPERT_GROUPS

    @pl.when(pl.program_id(0) == 0)
    def _():
        run_scr[...] = jnp.zeros_like(run_scr)

    logits = lax.dot_general(wr_ref[...], h_ref[...], (((1,), (1,)), ((), ())), preferred_element_type=F32)
    scores = jax.nn.sigmoid(logits)
    biased = scores + bias_ref[...]
    sc = [scores[w * ng:(w + 1) * ng] for w in range(per)]
    bi = [biased[w * ng:(w + 1) * ng] for w in range(per)]
    assert per == 4
    hi01, lo01 = jnp.maximum(bi[0], bi[1]), jnp.minimum(bi[0], bi[1])
    hi23, lo23 = jnp.maximum(bi[2], bi[3]), jnp.minimum(bi[2], bi[3])
    group_score = jnp.maximum(hi01, hi23) + jnp.maximum(jnp.minimum(hi01, hi23), jnp.maximum(lo01, lo23))
    gidx = lax.broadcasted_iota(I32, (ng, tm), 0)
    grank = jnp.zeros((ng, tm), I32)
    for g2 in range(ng):
        row = group_score[g2:g2 + 1]
        grank += ((row > group_score) | ((row == group_score) & (g2 < gidx))).astype(I32)
    gsel = grank < TOPK_GROUPS
    cand = [jnp.where(gsel, b, -jnp.inf) for b in bi]
    eid = [gidx * per + w for w in range(per)]
    erank = [jnp.zeros((ng, tm), I32) for _ in range(per)]
    for w2 in range(per):
        for g2 in range(ng):
            row = cand[w2][g2:g2 + 1]
            e2 = g2 * per + w2
            for w in range(per):
                erank[w] += ((row > cand[w]) | ((row == cand[w]) & (e2 < eid[w]))).astype(I32)
    sel = [r < TOP_K for r in erank]
    ssum = functools.reduce(jnp.add, [jnp.sum(jnp.where(m, s, 0.0), axis=0, keepdims=True) for m, s in zip(sel, sc)])
    gate = [jnp.where(m, s / ssum * ROUTED_SCALE, 0.0) for m, s in zip(sel, sc)]
    sel_all = jnp.concatenate([m.astype(F32) for m in sel], axis=0)
    earlier = (lax.broadcasted_iota(I32, (tm, tm), 0) < lax.broadcasted_iota(I32, (tm, tm), 1)).astype(BF16)
    pos = jnp.dot(sel_all.astype(BF16), earlier, preferred_element_type=F32) + run_scr[...]
    run_scr[...] += jnp.sum(sel_all, axis=1, keepdims=True)
    cnt_ref[...] = run_scr[...].astype(I32)
    above = (lax.broadcasted_iota(I32, (n_exp, n_exp), 1) < lax.broadcasted_iota(I32, (n_exp, n_exp), 0)).astype(BF16)
    kidx = jnp.dot(above, sel_all.astype(BF16), preferred_element_type=F32)
    eid_all = jnp.concatenate(eid, axis=0).astype(F32)
    gate_all = jnp.concatenate(gate, axis=0)
    for k in range(TOP_K):
        mk = (sel_all > 0.0) & (kidx == float(k))
        ek_ref[k:k + 1, :] = jnp.sum(jnp.where(mk, eid_all, 0.0), axis=0, keepdims=True).astype(I32)
        pk_ref[k:k + 1, :] = jnp.sum(jnp.where(mk, pos, 0.0), axis=0, keepdims=True).astype(I32)
        gk_ref[k:k + 1, :] = jnp.sum(jnp.where(mk, gate_all, 0.0), axis=0, keepdims=True)


def moe_route(h, router_w, router_bias):
    t, d = h.shape
    n_exp = router_w.shape[1]
    per = n_exp // N_EXPERT_GROUPS
    wr = router_w.T.reshape(N_EXPERT_GROUPS, per, d).transpose(1, 0, 2).reshape(n_exp, d).astype(BF16)
    bias = router_bias.astype(F32).reshape(N_EXPERT_GROUPS, per).T.reshape(n_exp, 1)
    tm = ROUTER_TILE
    kspec = pl.BlockSpec((TOP_K, tm), lambda i: (0, i))
    ek, pk, gk, cnt = pl.pallas_call(
        functools.partial(_router_kernel, n_exp=n_exp), grid=(t // tm,),
        in_specs=[pl.BlockSpec((tm, d), lambda i: (i, 0)), pl.BlockSpec((n_exp, d), lambda i: (0, 0)),
                  pl.BlockSpec((n_exp, 1), lambda i: (0, 0))],
        out_specs=[kspec, kspec, kspec, pl.BlockSpec((n_exp, 1), lambda i: (0, 0))],
        out_shape=[jax.ShapeDtypeStruct((TOP_K, t), I32), jax.ShapeDtypeStruct((TOP_K, t), I32),
                   jax.ShapeDtypeStruct((TOP_K, t), F32), jax.ShapeDtypeStruct((n_exp, 1), I32)],
        scratch_shapes=[pltpu.VMEM((n_exp, 1), F32)],
        compiler_params=_params("arbitrary"), name="moe_router",
    )(h, wr, bias)
    counts = cnt.reshape(per, N_EXPERT_GROUPS).T.reshape(n_exp)
    return ek, pk, gk, counts


def _row_copy(src_ref, src_row, dst_ref, dst_row, sem):
    return pltpu.make_async_copy(src_ref.at[pl.ds(src_row, 1)], dst_ref.at[pl.ds(dst_row, 1)], sem)


def _rows_done(src_ref, dst_ref, sem):
    pltpu.make_async_copy(src_ref.at[pl.ds(0, dst_ref.shape[0])], dst_ref, sem).wait()


def _slot_token_kernel(slot_ref, pad_lo_ref, pad_hi_ref, tok_ref, *, n_tok, steps):
    j = pl.program_id(0)

    @pl.when(j == 0)
    def _():
        for e in range(pad_lo_ref.shape[0]):
            def clear(s, _):
                tok_ref[s] = 0
                return 0
            lax.fori_loop(pad_lo_ref[e], pad_hi_ref[e], clear, 0)

    per = n_tok // steps
    for k in range(TOP_K):
        def put(r, _, k=k):
            tok = j * per + r
            tok_ref[slot_ref[k * n_tok + tok]] = tok
            return 0
        lax.fori_loop(0, per, put, 0, unroll=8)


def moe_slot_tokens(slots, pad_lo, pad_hi, n_slots):
    n_tok = slots.shape[0] // TOP_K
    steps = 64
    assert n_tok % steps == 0
    return pl.pallas_call(
        functools.partial(_slot_token_kernel, n_tok=n_tok, steps=steps),
        grid_spec=pltpu.PrefetchScalarGridSpec(
            num_scalar_prefetch=3, grid=(steps,), in_specs=[],
            out_specs=pl.BlockSpec(memory_space=pltpu.SMEM)),
        out_shape=jax.ShapeDtypeStruct((n_slots,), I32),
        compiler_params=_params("arbitrary"), name="moe_slot_tokens",
    )(slots, pad_lo, pad_hi)


def _expert_kernel(te_ref, nused_ref, tok_ref, h_ref, wg_ref, wu_ref, wd_ref, ys_ref, *scratch, te):
    i = pl.program_id(0)
    n_used = nused_ref[0]
    nbuf = GATHER_AHEAD + 1
    xbufs, sem = scratch[:nbuf], scratch[nbuf]

    @pl.when(i == 0)
    def _():
        for tile in range(GATHER_AHEAD):
            @pl.when(tile < n_used)
            def _(tile=tile):
                def issue(r, _):
                    _row_copy(h_ref, tok_ref[tile * te + r], xbufs[tile], r, sem.at[tile]).start()
                    return 0
                lax.fori_loop(0, te, issue, 0, unroll=8)

    def mlp(cur, fetch):
        _rows_done(h_ref, xbufs[cur], sem.at[cur])
        if fetch:
            nxt = (cur + GATHER_AHEAD) % nbuf
            for r in range(te):
                _row_copy(h_ref, tok_ref[(i + GATHER_AHEAD) * te + r], xbufs[nxt], r, sem.at[nxt]).start()
        x_lo, x_hi = [t.astype(BF16) for t in _unpack_row_halves(xbufs[cur][...])]
        half = x_lo.shape[1]
        lo, hi = pl.ds(0, half), pl.ds(half, half)
        g = (jnp.dot(x_lo, wg_ref[lo, :], preferred_element_type=F32)
             + jnp.dot(x_hi, wg_ref[hi, :], preferred_element_type=F32))
        u = (jnp.dot(x_lo, wu_ref[lo, :], preferred_element_type=F32)
             + jnp.dot(x_hi, wu_ref[hi, :], preferred_element_type=F32))
        a = (jax.nn.silu(g) * u).astype(BF16)
        ys_ref[...] = _pack_row_halves(jnp.dot(a, wd_ref[...], preferred_element_type=F32))

    for cur in range(nbuf):
        mine = (i % nbuf == cur) & (i < n_used)
        pl.when(mine & (i + GATHER_AHEAD < n_used))(functools.partial(mlp, cur, True))
        pl.when(mine & (i + GATHER_AHEAD >= n_used))(functools.partial(mlp, cur, False))

    @pl.when(i >= n_used)
    def _():
        ys_ref[...] = jnp.zeros_like(ys_ref)


def moe_experts(h_rows, slot_tok, tile_expert, n_used, wg, wu, wd):
    dh = h_rows.shape[1]
    d = 2 * dh
    n_slots = slot_tok.shape[0]
    de = wg.shape[2]
    te = EXPERT_TILE
    return pl.pallas_call(
        functools.partial(_expert_kernel, te=te),
        grid_spec=pltpu.PrefetchScalarGridSpec(
            num_scalar_prefetch=3, grid=(n_slots // te,),
            in_specs=[pl.BlockSpec(memory_space=pl.ANY),
                      pl.BlockSpec((None, d, de), lambda i, e, n, tk: (e[i], 0, 0)),
                      pl.BlockSpec((None, d, de), lambda i, e, n, tk: (e[i], 0, 0)),
                      pl.BlockSpec((None, de, d), lambda i, e, n, tk: (e[i], 0, 0))],
            out_specs=pl.BlockSpec((te, dh), lambda i, e, n, tk: (i, 0)),
            scratch_shapes=[pltpu.VMEM((te, dh), jnp.uint32)] * (GATHER_AHEAD + 1)
                           + [pltpu.SemaphoreType.DMA((GATHER_AHEAD + 1,))]),
        out_shape=jax.ShapeDtypeStruct((n_slots, dh), jnp.uint32),
        compiler_params=_params("arbitrary"), name="moe_experts",
    )(tile_expert, n_used, slot_tok, h_rows, wg, wu, wd)


def _combine_kernel(slot_ref, gk_ref, sh_ref, x_ref, gate_ref, ng_ref, ys_ref, o_ref, *scratch, tm, n_tok):
    i = pl.program_id(0)
    nbuf = GATHER_AHEAD + 1
    bufs, sem = scratch[:nbuf], scratch[nbuf]

    @pl.when(i == 0)
    def _():
        for tile in range(GATHER_AHEAD):
            for k in range(TOP_K):
                def issue(r, _, k=k, tile=tile):
                    src = slot_ref[k * n_tok + tile * tm + r]
                    _row_copy(ys_ref, src, bufs[tile].at[k], r, sem.at[tile]).start()
                    return 0
                lax.fori_loop(0, tm, issue, 0, unroll=8)

    def tile(b, fetch):
        buf = bufs[b]
        for k in range(TOP_K):
            _rows_done(ys_ref, buf.at[k], sem.at[b])
        if fetch:
            nxt = (b + GATHER_AHEAD) % nbuf
            for k in range(TOP_K):
                for r in range(tm):
                    src = slot_ref[k * n_tok + (i + GATHER_AHEAD) * tm + r]
                    _row_copy(ys_ref, src, bufs[nxt].at[k], r, sem.at[nxt]).start()
        half = buf.shape[-1]
        lo, hi = pl.ds(0, half), pl.ds(half, half)
        y_lo, y_hi = sh_ref[:, lo].astype(F32), sh_ref[:, hi].astype(F32)
        gk = gk_ref[...]
        for k in range(TOP_K):
            e_lo, e_hi = _unpack_row_halves(buf[k])
            y_lo = y_lo + gk[:, k:k + 1] * e_lo
            y_hi = y_hi + gk[:, k:k + 1] * e_hi
        ssq = jnp.sum(y_lo * y_lo, axis=-1, keepdims=True) + jnp.sum(y_hi * y_hi, axis=-1, keepdims=True)
        inv = lax.rsqrt(ssq / (2 * half) + NORM_EPS)
        o_ref[:, lo] = x_ref[:, lo] + gate_ref[:, lo] * (y_lo * inv * ng_ref[:, lo])
        o_ref[:, hi] = x_ref[:, hi] + gate_ref[:, hi] * (y_hi * inv * ng_ref[:, hi])

    for b in range(nbuf):
        mine = i % nbuf == b
        pl.when(mine & (i + GATHER_AHEAD < pl.num_programs(0)))(functools.partial(tile, b, True))
        pl.when(mine & (i + GATHER_AHEAD >= pl.num_programs(0)))(functools.partial(tile, b, False))


def moe_combine(x, ys, slots, gk_t, shared, norm_g, gate):
    bsz, seq, d = x.shape
    tm = COMBINE_TILE
    nt = seq // tm
    return pl.pallas_call(
        functools.partial(_combine_kernel, tm=tm, n_tok=bsz * seq),
        grid_spec=pltpu.PrefetchScalarGridSpec(
            num_scalar_prefetch=1, grid=(bsz * nt,),
            in_specs=[pl.BlockSpec((tm, TOP_K), lambda i, s: (i, 0)),
                      pl.BlockSpec((tm, d), lambda i, s: (i, 0)),
                      pl.BlockSpec((None, tm, d), lambda i, s: (i // nt, i % nt, 0)),
                      pl.BlockSpec((None, 1, d), lambda i, s: (i // nt, 0, 0)),
                      pl.BlockSpec((1, d), lambda i, s: (0, 0)),
                      pl.BlockSpec(memory_space=pl.ANY)],
            out_specs=pl.BlockSpec((None, tm, d), lambda i, s: (i // nt, i % nt, 0)),
            scratch_shapes=[pltpu.VMEM((TOP_K, tm, d // 2), jnp.uint32)] * (GATHER_AHEAD + 1)
                           + [pltpu.SemaphoreType.DMA((GATHER_AHEAD + 1,))]),
        out_shape=jax.ShapeDtypeStruct(x.shape, F32),
        compiler_params=_params("arbitrary"), name="moe_combine",
    )(slots, gk_t, shared, x, gate.reshape(bsz, 1, d), norm_g.reshape(1, d), ys)


def moe_layer(x, h, h_rows, norm_g, gate, router_w, router_bias, e_gate, e_up, e_down, s_gate, s_up, s_down):
    t, d = h.shape
    n_exp = router_w.shape[1]
    te = EXPERT_TILE
    ek, pk, gk, counts = moe_route(h, router_w, router_bias)
    tiles = (counts + te - 1) // te
    tile_end = jnp.cumsum(tiles)
    offsets = (tile_end - tiles) * te
    experts = jnp.arange(n_exp, dtype=I32)
    slot = pk + jnp.sum(jnp.where(ek[..., None] == experts, offsets, 0), axis=-1)
    slots = slot.reshape(-1).astype(I32)
    n_tiles = (t * TOP_K) // te + n_exp
    n_used = tile_end[-1].astype(I32)
    tile_ids = jnp.arange(n_tiles, dtype=I32)
    tile_expert = jnp.sum((tile_end[None, :] <= jnp.minimum(tile_ids, n_used - 1)[:, None]).astype(I32), axis=1)
    pad_lo = jnp.concatenate([offsets + counts, tile_end[-1:] * te]).astype(I32)
    pad_hi = jnp.concatenate([tile_end * te, jnp.full((1,), n_tiles * te, tile_end.dtype)]).astype(I32)
    slot_tok = moe_slot_tokens(slots, pad_lo, pad_hi, n_tiles * te)
    ys = moe_experts(h_rows, slot_tok, tile_expert, n_used.reshape(1), e_gate, e_up, e_down)
    act = matmul_swiglu(h, s_gate, s_up, tm=1024, tn=512, name="shared_up")
    shared = matmul(act, s_down, BF16, tm=1024, tn=1024, name="shared_down")
    return moe_combine(x, ys, slots, gk.T, shared, norm_g, gate)


def kernel(x, c, positions, ada_w, ada_b, norm_g, ssm_w_in, ssm_a_re, ssm_a_im, ssm_log_step, ssm_b_re, ssm_b_im, ssm_c_re, ssm_c_im, ssm_d, glu_w, glu_b, ssm_w_out, kv_norm_g, w_k, w_v, attn_w_q, attn_w_o, router_w, router_bias, exp_w_gate, exp_w_up, exp_w_down, sh_w_gate, sh_w_up, sh_w_down):
    bsz, seq, d = x.shape
    depth = ada_w.shape[0]
    n_a = ssm_w_in.shape[0]
    t = bsz * seq
    mod = ada_modulation(c, ada_w, ada_b)
    k = v = None
    for layer in range(depth):
        sh1, sc1, g1, sh2, sc2, g2 = jnp.split(mod[layer], N_MOD, axis=-1)
        if layer < n_a:
            a = layer
            (h,) = norm_modulate(x, norm_g[layer, 0], sc1, sh1)
            u_tm = matmul_to_token_major(h, cast_bf16(ssm_w_in, a), bsz, tm=256, tn=512, name="ssm_in")
            bblk, cblk, ab = s5_operators(ssm_a_re[a], ssm_a_im[a], ssm_log_step[a], ssm_b_re[a], ssm_b_im[a],
                                          ssm_c_re[a], ssm_c_im[a])
            z = s5_scan(u_tm, bblk, cblk, ab, ssm_d[a], bsz)
            zz = matmul_glu(z, cast_bf16(glu_w, a), glu_b[a], tm=1024, tn=512, name="ssm_glu")
            y = matmul_from_token_major(zz, cast_bf16(ssm_w_out, a), bsz, BF16, tm=256, tn=512, name="ssm_out")
        else:
            bl = layer - n_a
            n_dil = attn_w_q.shape[2] // d
            assert all(window // dil == ATTN_BLOCK for window, dil in DILATION_GROUPS[:n_dil])
            dils = tuple(dil for _, dil in DILATION_GROUPS[:n_dil])
            if layer == n_a:
                h, hkv = norm_modulate(x, norm_g[layer, 0], sc1, sh1, kv_g=kv_norm_g)
                kvd = w_k.shape[1]
                k_tabs = rope_tables(positions, 1.0)
                k = matmul_streams(hkv, cast_bf16(w_k[None], 0), 0, kvd, dils, k_tabs, tn=min(kvd, 512), name="k_proj")
                v = matmul_streams(hkv, cast_bf16(w_v[None], 0), 0, kvd, dils, None, tn=min(kvd, 512), name="v_proj")
            else:
                (h,) = norm_modulate(x, norm_g[layer, 0], sc1, sh1)
            q_tabs = rope_tables(positions, HEAD_DIM ** -0.5)
            wq = cast_bf16(attn_w_q, bl)
            tnq = min(d, 256)
            qs = [matmul_streams(h, wq, grp * (d // tnq), d, (dil,), q_tabs, tn=tnq, name=f"q_proj_{dil}")[0]
                  for grp, dil in enumerate(dils)]
            o = dilated_attention(qs, k, v, bsz, dils)
            y = matmul(o, cast_bf16(attn_w_o, bl), BF16, tm=1024, tn=512, name="attn_out")
        x, h, h_rows = norm_modulate(x, norm_g[layer, 2], sc2, sh2, with_rows=True,
                                     resid=(y, norm_g[layer, 1], g1))
        x = moe_layer(x, h, h_rows, norm_g[layer, 3], g2, router_w[layer], router_bias[layer],
                      cast_bf16(exp_w_gate, layer), cast_bf16(exp_w_up, layer), cast_bf16(exp_w_down, layer),
                      cast_bf16(sh_w_gate, layer), cast_bf16(sh_w_up, layer), cast_bf16(sh_w_down, layer))
    return x
```

```python
import functools
import math

import jax
import jax.numpy as jnp
from jax import lax
from jax.experimental import pallas as pl
from jax.experimental.pallas import tpu as pltpu

F32 = jnp.float32
BF16 = jnp.bfloat16
I32 = jnp.int32

LANE = 128
SUBLANE = 8
VMEM_LIMIT_BYTES = 56 * 1024 * 1024
MXU_DIM = 256

N_MOD = 6
NORM_EPS = 1e-6
HEAD_DIM = 128
ROT_DIM = HEAD_DIM // 4
ROT_HALF = ROT_DIM // 2
ROPE_THETA = 500000.0
DILATION_GROUPS = ((128, 1), (512, 4), (2048, 16))
ATTN_BLOCK = 128
ATTN_SPAN = ATTN_BLOCK * max(dil for _, dil in DILATION_GROUPS)
N_EXPERT_GROUPS = 8
TOPK_GROUPS = 4
TOP_K = 4
ROUTED_SCALE = 2.5
NEG_MASK = -1e30

SSM_PACK_CH = MXU_DIM
SSM_CHUNK_TOKENS = 256
SSM_LANE_CHUNK = 512
EXPERT_TILE = 256
ROW_TILE = 256
COMBINE_TILE = 128
GATHER_AHEAD = 2
ROUTER_TILE = 512
CAST_BLOCK_BYTES = 4 * 1024 * 1024


def _params(*sem):
    return pltpu.CompilerParams(dimension_semantics=sem, vmem_limit_bytes=VMEM_LIMIT_BYTES)


def _cast_kernel(x_ref, o_ref):
    o_ref[...] = x_ref[...].astype(o_ref.dtype)


def cast_bf16(w, layer):
    lead, c = w.shape[1:-1], w.shape[-1]
    rows = math.prod(lead)
    w3 = w.reshape(w.shape[0], rows, c)
    br = min(rows, 1 << ((CAST_BLOCK_BYTES // (4 * c)).bit_length() - 1))
    assert rows % br == 0 and br % (2 * SUBLANE) == 0
    out = pl.pallas_call(
        _cast_kernel, grid=(rows // br,),
        in_specs=[pl.BlockSpec((None, br, c), lambda i: (layer, i, 0))],
        out_specs=pl.BlockSpec((br, c), lambda i: (i, 0)),
        out_shape=jax.ShapeDtypeStruct((rows, c), BF16),
        compiler_params=_params("parallel"), name="cast_bf16",
    )(w3)
    return out.reshape(*lead, c)


def _ada_kernel(c_ref, w_ref, b_ref, o_ref):
    cond = jax.nn.silu(c_ref[...])
    o_ref[...] = jnp.dot(cond, w_ref[...], precision=lax.Precision.HIGHEST,
                         preferred_element_type=F32) + b_ref[...]


def ada_modulation(c, ada_w, ada_b):
    depth, d, n = ada_w.shape
    bsz = c.shape[0]
    rows = -(-bsz // SUBLANE) * SUBLANE
    cp = jnp.zeros((rows, d), F32).at[:bsz].set(c)
    tn = 512
    out = pl.pallas_call(
        _ada_kernel,
        grid=(depth, n // tn),
        in_specs=[pl.BlockSpec((rows, d), lambda l, j: (0, 0)),
                  pl.BlockSpec((None, d, tn), lambda l, j: (l, 0, j)),
                  pl.BlockSpec((None, 1, tn), lambda l, j: (l, 0, j))],
        out_specs=pl.BlockSpec((None, rows, tn), lambda l, j: (l, 0, j)),
        out_shape=jax.ShapeDtypeStruct((depth, rows, n), F32),
        compiler_params=_params("parallel", "parallel"),
        name="ada_modulation",
    )(cp, ada_w, ada_b.reshape(depth, 1, n))
    return out[:, :bsz]


def _bf16_bits(v):
    return lax.bitcast_convert_type(v.astype(BF16).astype(F32), jnp.uint32)


def _pack_row_halves(v):
    half = v.shape[1] // 2
    return (_bf16_bits(v[:, :half]) >> 16) | _bf16_bits(v[:, half:])


def _unpack_row_halves(p):
    lo = lax.bitcast_convert_type(p << 16, F32)
    hi = lax.bitcast_convert_type(p & jnp.uint32(0xFFFF0000), F32)
    return lo, hi


def _normmod_kernel(x_ref, g_ref, sc_ref, sh_ref, *rest, with_kv, with_rows, with_resid):
    x = x_ref[...]
    rest = list(rest)
    if with_resid:
        y_ref, rg_ref, gate_ref = rest.pop(0), rest.pop(0), rest.pop(0)
        y = y_ref[...].astype(F32)
        x = x + gate_ref[...] * (y * lax.rsqrt(jnp.mean(y * y, axis=-1, keepdims=True) + NORM_EPS) * rg_ref[...])
    xn = x * lax.rsqrt(jnp.mean(x * x, axis=-1, keepdims=True) + NORM_EPS)
    h = xn * g_ref[...] * (1.0 + sc_ref[...]) + sh_ref[...]
    if with_kv:
        kvg_ref = rest.pop(0)
    if with_resid:
        rest.pop(0)[...] = x
    h_ref = rest.pop(0)
    h_ref[...] = h.astype(h_ref.dtype)
    if with_kv:
        rest.pop(0)[...] = (xn * kvg_ref[...]).astype(BF16)
    if with_rows:
        rest.pop(0)[...] = _pack_row_halves(h)


def norm_modulate(x, g, sc, sh, kv_g=None, with_rows=False, resid=None):
    bsz, seq, d = x.shape
    ts = ROW_TILE
    nt = seq // ts
    row = lambda b, i: (b * nt + i, 0)
    vec = pl.BlockSpec((1, d), lambda b, i: (0, 0))
    bvec = pl.BlockSpec((None, 1, d), lambda b, i: (b, 0, 0))
    xspec = pl.BlockSpec((None, ts, d), lambda b, i: (b, i, 0))
    in_specs = [xspec, vec, bvec, bvec]
    args = [x, g.reshape(1, d), sc.reshape(bsz, 1, d), sh.reshape(bsz, 1, d)]
    out_specs = [pl.BlockSpec((ts, d), row)]
    out_shape = [jax.ShapeDtypeStruct((bsz * seq, d), BF16)]
    if resid is not None:
        y, rg, gate = resid
        in_specs += [pl.BlockSpec((ts, d), row), vec, bvec]
        args += [y, rg.reshape(1, d), gate.reshape(bsz, 1, d)]
        out_specs.insert(0, xspec)
        out_shape.insert(0, jax.ShapeDtypeStruct(x.shape, F32))
    if kv_g is not None:
        in_specs.append(vec)
        args.append(kv_g.reshape(1, d))
        out_specs.append(pl.BlockSpec((ts, d), row))
        out_shape.append(jax.ShapeDtypeStruct((bsz * seq, d), BF16))
    if with_rows:
        out_specs.append(pl.BlockSpec((ts, d // 2), row))
        out_shape.append(jax.ShapeDtypeStruct((bsz * seq, d // 2), jnp.uint32))
    return pl.pallas_call(
        functools.partial(_normmod_kernel, with_kv=kv_g is not None, with_rows=with_rows,
                          with_resid=resid is not None),
        grid=(bsz, nt), in_specs=in_specs, out_specs=out_specs, out_shape=out_shape,
        compiler_params=_params("parallel", "parallel"), name="norm_modulate",
    )(*args)


def _mm_kernel(a_ref, w_ref, o_ref):
    o_ref[...] = jnp.dot(a_ref[...], w_ref[...], preferred_element_type=F32).astype(o_ref.dtype)


def _mm_glu_kernel(a_ref, w_ref, z_ref, b_ref, o_ref):
    acc = jnp.dot(a_ref[...], w_ref[...], preferred_element_type=F32) + b_ref[...]
    o_ref[...] = (z_ref[...].astype(F32) * jax.nn.sigmoid(acc)).astype(o_ref.dtype)


def _mm_swiglu_kernel(a_ref, wg_ref, wu_ref, o_ref):
    a = a_ref[...]
    g = jnp.dot(a, wg_ref[...], preferred_element_type=F32)
    u = jnp.dot(a, wu_ref[...], preferred_element_type=F32)
    o_ref[...] = (jax.nn.silu(g) * u).astype(o_ref.dtype)


def matmul(a, w, out_dtype, *, tm, tn, name):
    m, k = a.shape
    n = w.shape[1]
    tm, tn = min(tm, m), min(tn, n)
    return pl.pallas_call(
        _mm_kernel, grid=(m // tm, n // tn),
        in_specs=[pl.BlockSpec((tm, k), lambda i, j: (i, 0)), pl.BlockSpec((k, tn), lambda i, j: (0, j))],
        out_specs=pl.BlockSpec((tm, tn), lambda i, j: (i, j)),
        out_shape=jax.ShapeDtypeStruct((m, n), out_dtype),
        compiler_params=_params("parallel", "arbitrary"), name=name,
    )(a, w)


def matmul_swiglu(a, wg, wu, *, tm, tn, name):
    m, k = a.shape
    n = wg.shape[1]
    tm, tn = min(tm, m), min(tn, n)
    wspec = pl.BlockSpec((k, tn), lambda i, j: (0, j))
    return pl.pallas_call(
        _mm_swiglu_kernel, grid=(m // tm, n // tn),
        in_specs=[pl.BlockSpec((tm, k), lambda i, j: (i, 0)), wspec, wspec],
        out_specs=pl.BlockSpec((tm, tn), lambda i, j: (i, j)),
        out_shape=jax.ShapeDtypeStruct((m, n), BF16),
        compiler_params=_params("parallel", "arbitrary"), name=name,
    )(a, wg, wu)


def matmul_glu(z, w, b, *, tm, tn, name):
    m, k = z.shape
    n = w.shape[1]
    tm, tn = min(tm, m), min(tn, n)
    return pl.pallas_call(
        _mm_glu_kernel, grid=(m // tm, n // tn),
        in_specs=[pl.BlockSpec((tm, k), lambda i, j: (i, 0)), pl.BlockSpec((k, tn), lambda i, j: (0, j)),
                  pl.BlockSpec((tm, tn), lambda i, j: (i, j)), pl.BlockSpec((1, tn), lambda i, j: (0, j))],
        out_specs=pl.BlockSpec((tm, tn), lambda i, j: (i, j)),
        out_shape=jax.ShapeDtypeStruct((m, n), BF16),
        compiler_params=_params("parallel", "arbitrary"), name=name,
    )(z, w, z, b.reshape(1, n))


def _mm_to_token_major_kernel(a_ref, w_ref, o_ref, scr):
    bsz, tm, _ = a_ref.shape
    for b in range(bsz):
        acc = jnp.dot(a_ref[b], w_ref[...], preferred_element_type=F32)
        for j in range(scr.shape[0]):
            scr[j, pl.ds(b, tm, stride=bsz), :] = acc[:, j * LANE:(j + 1) * LANE]
    for j in range(scr.shape[0]):
        o_ref[:, j * LANE:(j + 1) * LANE] = scr[j].astype(o_ref.dtype)


def _mm_from_token_major_kernel(a_ref, w_ref, o_ref, scr):
    bsz, tm, _ = o_ref.shape
    acc = jnp.dot(a_ref[...], w_ref[...], preferred_element_type=F32)
    for j in range(scr.shape[0]):
        scr[j] = acc[:, j * LANE:(j + 1) * LANE]
    for b in range(bsz):
        for j in range(scr.shape[0]):
            o_ref[b, :, j * LANE:(j + 1) * LANE] = scr[j, pl.ds(b, tm, stride=bsz), :].astype(o_ref.dtype)


def matmul_to_token_major(a, w, bsz, *, tm, tn, name):
    m, k = a.shape
    n = w.shape[1]
    seq = m // bsz
    tm, tn = min(tm, seq), min(tn, n)
    return pl.pallas_call(
        _mm_to_token_major_kernel, grid=(seq // tm, n // tn),
        in_specs=[pl.BlockSpec((bsz, tm, k), lambda i, j: (0, i, 0)),
                  pl.BlockSpec((k, tn), lambda i, j: (0, j))],
        out_specs=pl.BlockSpec((bsz * tm, tn), lambda i, j: (i, j)),
        out_shape=jax.ShapeDtypeStruct((m, n), F32),
        scratch_shapes=[pltpu.VMEM((tn // LANE, bsz * tm, LANE), F32)],
        compiler_params=_params("parallel", "arbitrary"), name=name,
    )(a.reshape(bsz, seq, k), w)


def matmul_from_token_major(a, w, bsz, out_dtype, *, tm, tn, name):
    m, k = a.shape
    n = w.shape[1]
    seq = m // bsz
    tm, tn = min(tm, seq), min(tn, n)
    out = pl.pallas_call(
        _mm_from_token_major_kernel, grid=(seq // tm, n // tn),
        in_specs=[pl.BlockSpec((bsz * tm, k), lambda i, j: (i, 0)),
                  pl.BlockSpec((k, tn), lambda i, j: (0, j))],
        out_specs=pl.BlockSpec((bsz, tm, tn), lambda i, j: (0, i, j)),
        out_shape=jax.ShapeDtypeStruct((bsz, seq, n), out_dtype),
        scratch_shapes=[pltpu.VMEM((tn // LANE, bsz * tm, LANE), F32)],
        compiler_params=_params("parallel", "arbitrary"), name=name,
    )(a, w)
    return out.reshape(m, n)


def _s5_discretize_kernel(are_ref, aim_ref, ls_ref, abr_ref, abi_ref, fr_ref, fi_ref):
    lr, li = are_ref[...], aim_ref[...]
    step = jnp.exp(ls_ref[...])
    decay = jnp.exp(lr * step)
    abar_re = decay * jnp.cos(li * step)
    abar_im = decay * jnp.sin(li * step)
    denom = lr * lr + li * li
    nr = abar_re - 1.0
    ni = abar_im
    abr_ref[...] = abar_re
    abi_ref[...] = abar_im
    fr_ref[...] = (nr * lr + ni * li) / denom
    fi_ref[...] = (ni * lr - nr * li) / denom


def s5_discretize(a_re, a_im, log_step):
    g, p = a_re.shape
    shp = jax.ShapeDtypeStruct((g, p), F32)
    return pl.pallas_call(_s5_discretize_kernel, out_shape=[shp] * 4, name="s5_discretize")(
        a_re, a_im, log_step.reshape(g, 1))


def _s5_scan_kernel(un_ref, up_ref, bblk_ref, cblk_ref, ab_ref, d_ref, z_ref, x2_scr, xb2_scr, carry_scr,
                    *, rows, nstate, bsz):
    c = pl.program_id(1)
    n_chunks = pl.num_programs(1) - 1

    @pl.when(c == 0)
    def _():
        carry_scr[...] = jnp.zeros_like(carry_scr)
        xb2_scr[1] = jnp.zeros(xb2_scr.shape[1:], xb2_scr.dtype)
        x2_scr[0] = jnp.dot(up_ref[...].astype(BF16), bblk_ref[...], preferred_element_type=F32)

    u_prev = up_ref[...]
    x2_scr[(c + 1) % 2] = jnp.dot(un_ref[...].astype(BF16), bblk_ref[...], preferred_element_type=F32)
    y = jnp.dot(xb2_scr[(c + 1) % 2], cblk_ref[...], preferred_element_type=F32) + d_ref[...] * u_prev
    z_ref[...] = jax.nn.gelu(y).astype(z_ref.dtype)

    @pl.when(c < n_chunks)
    def _():
        _s5_recurrence(x2_scr.at[c % 2], xb2_scr.at[c % 2], ab_ref, carry_scr, rows=rows, nstate=nstate, bsz=bsz)


def _s5_recurrence(x_scr, xb_scr, ab_ref, carry_scr, *, rows, nstate, bsz):
    lw = SSM_LANE_CHUNK
    second = lax.broadcasted_iota(I32, (SUBLANE, lw), 0) >= bsz
    for lt in range(nstate // lw):
        re = slice(lt * lw, (lt + 1) * lw)
        im = slice(nstate + lt * lw, nstate + (lt + 1) * lw)
        ar, ai = ab_ref[0:1, re], ab_ref[1:2, re]
        a2r, a2i = ar * ar - ai * ai, 2.0 * ar * ai
        amr, ami = jnp.where(second, ar, 0.0), jnp.where(second, ai, 0.0)
        pr, pi = jnp.where(second, a2r, ar), jnp.where(second, a2i, ai)

        def advance(r0, cr, ci, re=re, im=im, amr=amr, ami=ami, pr=pr, pi=pi):
            bur, bui = x_scr[pl.ds(r0, SUBLANE), re], x_scr[pl.ds(r0, SUBLANE), im]
            sr, si = pltpu.roll(bur, bsz, 0), pltpu.roll(bui, bsz, 0)
            wr = bur + amr * sr - ami * si
            wi = bui + amr * si + ami * sr
            cbr = jnp.where(second, cr, pltpu.roll(cr, bsz, 0))
            cbi = jnp.where(second, ci, pltpu.roll(ci, bsz, 0))
            return wr + pr * cbr - pi * cbi, wi + pr * cbi + pi * cbr

        def body(n, carry, re=re, im=im, advance=advance):
            r0 = pl.multiple_of(n * (2 * SUBLANE), 2 * SUBLANE)
            xr0, xi0 = advance(r0, *carry)
            xr1, xi1 = advance(r0 + SUBLANE, xr0, xi0)
            xb_scr[pl.ds(r0, 2 * SUBLANE), re] = jnp.concatenate([xr0, xr1], axis=0).astype(BF16)
            xb_scr[pl.ds(r0, 2 * SUBLANE), im] = jnp.concatenate([xi0, xi1], axis=0).astype(BF16)
            return xr1, xi1

        cr, ci = lax.fori_loop(0, rows // (2 * SUBLANE), body, (carry_scr[:, re], carry_scr[:, im]))
        carry_scr[:, re] = cr
        carry_scr[:, im] = ci


def s5_scan(u_tm, bblk, cblk, ab, d_skip, bsz):
    m, width = u_tm.shape
    assert 2 * bsz == SUBLANE, "the scan tiles two tokens of every batch element into one sublane tile"
    packs, pack_ch, two_n = bblk.shape
    nstate = two_n // 2
    rows = SSM_CHUNK_TOKENS * bsz
    n_chunks = m // rows
    nxt = lambda p, c: (jnp.minimum(c + 1, n_chunks - 1), p)
    prv = lambda p, c: (jnp.clip(c - 1, 0, n_chunks - 1), p)
    return pl.pallas_call(
        functools.partial(_s5_scan_kernel, rows=rows, nstate=nstate, bsz=bsz),
        grid=(packs, n_chunks + 1),
        in_specs=[pl.BlockSpec((rows, pack_ch), nxt), pl.BlockSpec((rows, pack_ch), prv),
                  pl.BlockSpec((None, pack_ch, two_n), lambda p, c: (p, 0, 0)),
                  pl.BlockSpec((None, two_n, pack_ch), lambda p, c: (p, 0, 0)),
                  pl.BlockSpec((None, 2, nstate), lambda p, c: (p, 0, 0)),
                  pl.BlockSpec((None, 1, pack_ch), lambda p, c: (p, 0, 0))],
        out_specs=pl.BlockSpec((rows, pack_ch), prv),
        out_shape=jax.ShapeDtypeStruct((m, width), BF16),
        scratch_shapes=[pltpu.VMEM((2, rows, two_n), F32), pltpu.VMEM((2, rows, two_n), BF16),
                        pltpu.VMEM((SUBLANE, two_n), F32)],
        compiler_params=_params("parallel", "arbitrary"), name="s5_scan",
    )(u_tm, u_tm, bblk, cblk, ab, d_skip.reshape(packs, 1, pack_ch))


def s5_operators(a_re, a_im, log_step, b_re, b_im, c_re, c_im):
    g, p = a_re.shape
    gsz = b_re.shape[-1]
    gpp = SSM_PACK_CH // gsz
    packs = g // gpp
    abr, abi, fr, fi = s5_discretize(a_re, a_im, log_step)
    bbar_re = fr[..., None] * b_re - fi[..., None] * b_im
    bbar_im = fr[..., None] * b_im + fi[..., None] * b_re
    eye = jnp.eye(gpp, dtype=F32)

    def b_block(t):
        t = t.reshape(packs, gpp, p, gsz).transpose(0, 1, 3, 2)
        return (t[:, :, :, None, :] * eye[None, :, None, :, None]).reshape(packs, gpp * gsz, gpp * p)

    def c_block(t):
        t = t.reshape(packs, gpp, gsz, p).transpose(0, 1, 3, 2)
        return (t[:, :, :, None, :] * eye[None, :, None, :, None]).reshape(packs, gpp * p, gpp * gsz)

    bblk = jnp.concatenate([b_block(bbar_re), b_block(bbar_im)], axis=2).astype(BF16)
    cblk = jnp.concatenate([c_block(c_re), -c_block(c_im)], axis=1).astype(BF16)
    ab = jnp.stack([abr.reshape(packs, gpp * p), abi.reshape(packs, gpp * p)], axis=1)
    return bblk, cblk, ab


def _rope_table_kernel(pos_ref, invf_ref, cos_ref, sin_lo_ref, sin_hi_ref, *, scale):
    ang = pos_ref[...].astype(F32) * invf_ref[...]
    lane = lax.broadcasted_iota(I32, ang.shape, 1)
    c, s = jnp.cos(ang), jnp.sin(ang)
    cos_ref[...] = jnp.where(lane < ROT_DIM, c, 1.0) * scale
    sin_lo_ref[...] = jnp.where(lane < ROT_HALF, -s, 0.0) * scale
    sin_hi_ref[...] = jnp.where((lane >= ROT_HALF) & (lane < ROT_DIM), s, 0.0) * scale


def rope_tables(positions, scale):
    t = positions.size
    inv_freq = ROPE_THETA ** (-jnp.arange(ROT_HALF, dtype=F32) * 2.0 / ROT_DIM)
    invf = jnp.zeros((1, HEAD_DIM), F32).at[0, :ROT_DIM].set(jnp.concatenate([inv_freq, inv_freq]))
    ts = 1024
    shp = jax.ShapeDtypeStruct((t, HEAD_DIM), F32)
    spec = pl.BlockSpec((ts, HEAD_DIM), lambda i: (i, 0))
    return pl.pallas_call(
        functools.partial(_rope_table_kernel, scale=scale), grid=(t // ts,),
        in_specs=[pl.BlockSpec((ts, 1), lambda i: (i, 0)), pl.BlockSpec((1, HEAD_DIM), lambda i: (0, 0))],
        out_specs=[spec] * 3, out_shape=[shp] * 3,
        compiler_params=_params("parallel"), name="rope_tables",
    )(positions.reshape(t, 1), invf)


def _stream_rows(chunk, residue, dil):
    start = chunk * (ATTN_BLOCK * dil) + residue
    return pl.ds(start, ATTN_BLOCK) if dil == 1 else pl.ds(start, ATTN_BLOCK, stride=dil)


def _stream_major_store(o_ref, src_ref, dil):
    span = ATTN_BLOCK * dil
    for hd in range(src_ref.shape[0]):
        cols = slice(hd * HEAD_DIM, (hd + 1) * HEAD_DIM)
        for chunk in range(src_ref.shape[1] // span):
            for residue in range(dil):
                dst = pl.ds(chunk * span + residue * ATTN_BLOCK, ATTN_BLOCK)
                o_ref[dst, cols] = src_ref[hd, _stream_rows(chunk, residue, dil), :].astype(o_ref.dtype)


def _mm_streams_kernel(a_ref, w_ref, *rest, dils, rotary):
    rest = list(rest)
    if rotary:
        cos, sin_lo, sin_hi = [rest.pop(0)[...] for _ in range(3)]
    o_refs, scr = rest[:len(dils)], rest[len(dils)]
    acc = jnp.dot(a_ref[...], w_ref[...], preferred_element_type=F32)
    for hd in range(acc.shape[1] // HEAD_DIM):
        t = acc[:, hd * HEAD_DIM:(hd + 1) * HEAD_DIM]
        if rotary:
            t = t * cos + pltpu.roll(t, HEAD_DIM - ROT_HALF, 1) * sin_lo + pltpu.roll(t, ROT_HALF, 1) * sin_hi
        scr[hd] = t
    for o_ref, dil in zip(o_refs, dils):
        _stream_major_store(o_ref, scr, dil)


def matmul_streams(a, w, col_block0, n, dils, tables, *, tn, name):
    m, k = a.shape
    tm = ATTN_SPAN
    rotary = tables is not None
    in_specs = [pl.BlockSpec((tm, k), lambda i, j: (i, 0), pipeline_mode=pl.Buffered(1)),
                pl.BlockSpec((k, tn), lambda i, j: (0, col_block0 + j))]
    args = [a, w]
    if rotary:
        in_specs += [pl.BlockSpec((tm, HEAD_DIM), lambda i, j: (i, 0))] * 3
        args += list(tables)
    return pl.pallas_call(
        functools.partial(_mm_streams_kernel, dils=dils, rotary=rotary), grid=(m // tm, n // tn),
        in_specs=in_specs,
        out_specs=[pl.BlockSpec((tm, tn), lambda i, j: (i, j))] * len(dils),
        out_shape=[jax.ShapeDtypeStruct((m, n), BF16)] * len(dils),
        scratch_shapes=[pltpu.VMEM((tn // HEAD_DIM, tm, HEAD_DIM), F32)],
        compiler_params=_params("parallel", "arbitrary"), name=name,
    )(*args)


def _attn_kernel(*refs, dils):
    n = len(dils)
    q_refs = refs[:n]
    kv_refs = refs[n:5 * n]
    o_ref, o_scr, l_scr = refs[5 * n], refs[5 * n + 1], refs[5 * n + 2]
    blk = ATTN_BLOCK
    nb = ATTN_SPAN // blk
    seq_start = pl.program_id(1) == 0
    qi = lax.broadcasted_iota(I32, (nb, blk, blk), 1)
    kj = lax.broadcasted_iota(I32, (nb, blk, blk), 2)
    bi = lax.broadcasted_iota(I32, (nb, blk, blk), 0)
    cur_ok = kj <= qi
    qk = (((2,), (2,)), ((0,), (0,)))
    pv = (((2,), (1,)), ((0,), (0,)))
    for g, dil in enumerate(dils):
        kp_ref, kc_ref, vp_ref, vc_ref = kv_refs[4 * g:4 * g + 4]
        span = blk * dil
        kc, vc = kc_ref[...], vc_ref[...]
        if span == ATTN_SPAN:
            kp, vp = kp_ref[...], vp_ref[...]
        else:
            kp = jnp.concatenate([kp_ref[...], kc[:ATTN_SPAN - span]], axis=0)
            vp = jnp.concatenate([vp_ref[...], vc[:ATTN_SPAN - span]], axis=0)
        split = lambda t: t.reshape(nb, blk, HEAD_DIM)
        q3, kp3, kc3, vp3, vc3 = split(q_refs[g][...]), split(kp), split(kc), split(vp), split(vc)
        prev_ok = (kj >= qi) & jnp.logical_not(seq_start & (bi < dil))
        sp = jnp.where(prev_ok, lax.dot_general(q3, kp3, qk, preferred_element_type=F32), NEG_MASK)
        sc = jnp.where(cur_ok, lax.dot_general(q3, kc3, qk, preferred_element_type=F32), NEG_MASK)
        mx = jnp.maximum(jnp.max(sp, axis=-1, keepdims=True), jnp.max(sc, axis=-1, keepdims=True))
        pp, pc = jnp.exp(sp - mx), jnp.exp(sc - mx)
        den = jnp.sum(pp, axis=-1, keepdims=True) + jnp.sum(pc, axis=-1, keepdims=True)
        o = (lax.dot_general((pp / den).astype(BF16), vp3, pv, preferred_element_type=F32)
             + lax.dot_general((pc / den).astype(BF16), vc3, pv, preferred_element_type=F32))
        lse = jnp.broadcast_to(mx + jnp.log(den), (nb, blk, HEAD_DIM))
        for b in range(nb):
            dst = _stream_rows(b // dil, b % dil, dil)
            o_scr[g, dst, :] = o[b]
            l_scr[g, dst, :] = lse[b]
    ls = [l_scr[g] for g in range(n)]
    mx = functools.reduce(jnp.maximum, ls)
    es = [jnp.exp(l - mx) for l in ls]
    tot = functools.reduce(jnp.add, es)
    acc = o_scr[0] * (es[0] / tot)
    for g in range(1, n):
        acc = acc + o_scr[g] * (es[g] / tot)
    o_ref[...] = acc.astype(o_ref.dtype)


def dilated_attention(qs, ks, vs, bsz, dils):
    t, d = qs[0].shape
    kvd = ks[0].shape[1]
    rep = d // kvd
    seq = t // bsz
    assert seq % ATTN_SPAN == 0
    ns = seq // ATTN_SPAN
    cur = lambda b, c, h: (b * ns + c, h // rep)
    in_specs = [pl.BlockSpec((ATTN_SPAN, HEAD_DIM), lambda b, c, h: (b * ns + c, h))] * len(dils)
    args = list(qs)
    for g, dil in enumerate(dils):
        span = ATTN_BLOCK * dil
        per = ATTN_SPAN // span
        prev = lambda b, c, h, per=per: (jnp.maximum((b * ns + c) * per - 1, 0), h // rep)
        in_specs += [pl.BlockSpec((span, HEAD_DIM), prev), pl.BlockSpec((ATTN_SPAN, HEAD_DIM), cur)] * 2
        args += [ks[g], ks[g], vs[g], vs[g]]
    return pl.pallas_call(
        functools.partial(_attn_kernel, dils=dils),
        grid=(bsz, ns, d // HEAD_DIM),
        in_specs=in_specs,
        out_specs=pl.BlockSpec((ATTN_SPAN, HEAD_DIM), lambda b, c, h: (b * ns + c, h)),
        out_shape=jax.ShapeDtypeStruct((t, d), BF16),
        scratch_shapes=[pltpu.VMEM((len(dils), ATTN_SPAN, HEAD_DIM), F32)] * 2,
        compiler_params=_params("parallel", "arbitrary", "arbitrary"), name="dilated_attention",
    )(*args)


def _router_kernel(h_ref, wr_ref, bias_ref, ek_ref, pk_ref, gk_ref, cnt_ref, run_scr, *, n_exp):
    tm = h_ref.shape[0]
    per = n_exp // N_EXPERT_GROUPS
    ng = N_EXPERT_GROUPS

    @pl.when(pl.program_id(0) == 0)
    def _():
        run_scr[...] = jnp.zeros_like(run_scr)

    logits = lax.dot_general(wr_ref[...], h_ref[...], (((1,), (1,)), ((), ())), preferred_element_type=F32)
    scores = jax.nn.sigmoid(logits)
    biased = scores + bias_ref[...]
    sc = [scores[w * ng:(w + 1) * ng] for w in range(per)]
    bi = [biased[w * ng:(w + 1) * ng] for w in range(per)]
    assert per == 4
    hi01, lo01 = jnp.maximum(bi[0], bi[1]), jnp.minimum(bi[0], bi[1])
    hi23, lo23 = jnp.maximum(bi[2], bi[3]), jnp.minimum(bi[2], bi[3])
    group_score = jnp.maximum(hi01, hi23) + jnp.maximum(jnp.minimum(hi01, hi23), jnp.maximum(lo01, lo23))
    gidx = lax.broadcasted_iota(I32, (ng, tm), 0)
    grank = jnp.zeros((ng, tm), I32)
    for g2 in range(ng):
        row = group_score[g2:g2 + 1]
        grank += ((row > group_score) | ((row == group_score) & (g2 < gidx))).astype(I32)
    gsel = grank < TOPK_GROUPS
    cand = [jnp.where(gsel, b, -jnp.inf) for b in bi]
    eid = [gidx * per + w for w in range(per)]
    erank = [jnp.zeros((ng, tm), I32) for _ in range(per)]
    for w2 in range(per):
        for g2 in range(ng):
            row = cand[w2][g2:g2 + 1]
            e2 = g2 * per + w2
            for w in range(per):
                erank[w] += ((row > cand[w]) | ((row == cand[w]) & (e2 < eid[w]))).astype(I32)
    sel = [r < TOP_K for r in erank]
    ssum = functools.reduce(jnp.add, [jnp.sum(jnp.where(m, s, 0.0), axis=0, keepdims=True) for m, s in zip(sel, sc)])
    gate = [jnp.where(m, s / ssum * ROUTED_SCALE, 0.0) for m, s in zip(sel, sc)]
    sel_all = jnp.concatenate([m.astype(F32) for m in sel], axis=0)
    earlier = (lax.broadcasted_iota(I32, (tm, tm), 0) < lax.broadcasted_iota(I32, (tm, tm), 1)).astype(BF16)
    pos = jnp.dot(sel_all.astype(BF16), earlier, preferred_element_type=F32) + run_scr[...]
    run_scr[...] += jnp.sum(sel_all, axis=1, keepdims=True)
    cnt_ref[...] = run_scr[...].astype(I32)
    above = (lax.broadcasted_iota(I32, (n_exp, n_exp), 1) < lax.broadcasted_iota(I32, (n_exp, n_exp), 0)).astype(BF16)
    kidx = jnp.dot(above, sel_all.astype(BF16), preferred_element_type=F32)
    eid_all = jnp.concatenate(eid, axis=0).astype(F32)
    gate_all = jnp.concatenate(gate, axis=0)
    for k in range(TOP_K):
        mk = (sel_all > 0.0) & (kidx == float(k))
        ek_ref[k:k + 1, :] = jnp.sum(jnp.where(mk, eid_all, 0.0), axis=0, keepdims=True).astype(I32)
        pk_ref[k:k + 1, :] = jnp.sum(jnp.where(mk, pos, 0.0), axis=0, keepdims=True).astype(I32)
        gk_ref[k:k + 1, :] = jnp.sum(jnp.where(mk, gate_all, 0.0), axis=0, keepdims=True)


def moe_route(h, router_w, router_bias):
    t, d = h.shape
    n_exp = router_w.shape[1]
    per = n_exp // N_EXPERT_GROUPS
    wr = router_w.T.reshape(N_EXPERT_GROUPS, per, d).transpose(1, 0, 2).reshape(n_exp, d).astype(BF16)
    bias = router_bias.astype(F32).reshape(N_EXPERT_GROUPS, per).T.reshape(n_exp, 1)
    tm = ROUTER_TILE
    kspec = pl.BlockSpec((TOP_K, tm), lambda i: (0, i))
    ek, pk, gk, cnt = pl.pallas_call(
        functools.partial(_router_kernel, n_exp=n_exp), grid=(t // tm,),
        in_specs=[pl.BlockSpec((tm, d), lambda i: (i, 0)), pl.BlockSpec((n_exp, d), lambda i: (0, 0)),
                  pl.BlockSpec((n_exp, 1), lambda i: (0, 0))],
        out_specs=[kspec, kspec, kspec, pl.BlockSpec((n_exp, 1), lambda i: (0, 0))],
        out_shape=[jax.ShapeDtypeStruct((TOP_K, t), I32), jax.ShapeDtypeStruct((TOP_K, t), I32),
                   jax.ShapeDtypeStruct((TOP_K, t), F32), jax.ShapeDtypeStruct((n_exp, 1), I32)],
        scratch_shapes=[pltpu.VMEM((n_exp, 1), F32)],
        compiler_params=_params("arbitrary"), name="moe_router",
    )(h, wr, bias)
    counts = cnt.reshape(per, N_EXPERT_GROUPS).T.reshape(n_exp)
    return ek, pk, gk, counts


def _row_copy(src_ref, src_row, dst_ref, dst_row, sem):
    return pltpu.make_async_copy(src_ref.at[pl.ds(src_row, 1)], dst_ref.at[pl.ds(dst_row, 1)], sem)


def _rows_done(src_ref, dst_ref, sem):
    pltpu.make_async_copy(src_ref.at[pl.ds(0, dst_ref.shape[0])], dst_ref, sem).wait()


def _slot_token_kernel(slot_ref, pad_lo_ref, pad_hi_ref, tok_ref, *, n_tok, steps):
    j = pl.program_id(0)

    @pl.when(j == 0)
    def _():
        for e in range(pad_lo_ref.shape[0]):
            def clear(s, _):
                tok_ref[s] = 0
                return 0
            lax.fori_loop(pad_lo_ref[e], pad_hi_ref[e], clear, 0)

    per = n_tok // steps
    for k in range(TOP_K):
        def put(r, _, k=k):
            tok = j * per + r
            tok_ref[slot_ref[k * n_tok + tok]] = tok
            return 0
        lax.fori_loop(0, per, put, 0, unroll=8)


def moe_slot_tokens(slots, pad_lo, pad_hi, n_slots):
    n_tok = slots.shape[0] // TOP_K
    steps = 64
    assert n_tok % steps == 0
    return pl.pallas_call(
        functools.partial(_slot_token_kernel, n_tok=n_tok, steps=steps),
        grid_spec=pltpu.PrefetchScalarGridSpec(
            num_scalar_prefetch=3, grid=(steps,), in_specs=[],
            out_specs=pl.BlockSpec(memory_space=pltpu.SMEM)),
        out_shape=jax.ShapeDtypeStruct((n_slots,), I32),
        compiler_params=_params("arbitrary"), name="moe_slot_tokens",
    )(slots, pad_lo, pad_hi)


def _expert_kernel(te_ref, nused_ref, tok_ref, h_ref, wg_ref, wu_ref, wd_ref, ys_ref, *scratch, te):
    i = pl.program_id(0)
    n_used = nused_ref[0]
    nbuf = GATHER_AHEAD + 1
    xbufs, sem = scratch[:nbuf], scratch[nbuf]

    @pl.when(i == 0)
    def _():
        for tile in range(GATHER_AHEAD):
            @pl.when(tile < n_used)
            def _(tile=tile):
                def issue(r, _):
                    _row_copy(h_ref, tok_ref[tile * te + r], xbufs[tile], r, sem.at[tile]).start()
                    return 0
                lax.fori_loop(0, te, issue, 0, unroll=8)

    def mlp(cur, fetch):
        _rows_done(h_ref, xbufs[cur], sem.at[cur])
        if fetch:
            nxt = (cur + GATHER_AHEAD) % nbuf
            for r in range(te):
                _row_copy(h_ref, tok_ref[(i + GATHER_AHEAD) * te + r], xbufs[nxt], r, sem.at[nxt]).start()
        x_lo, x_hi = [t.astype(BF16) for t in _unpack_row_halves(xbufs[cur][...])]
        half = x_lo.shape[1]
        lo, hi = pl.ds(0, half), pl.ds(half, half)
        g = (jnp.dot(x_lo, wg_ref[lo, :], preferred_element_type=F32)
             + jnp.dot(x_hi, wg_ref[hi, :], preferred_element_type=F32))
        u = (jnp.dot(x_lo, wu_ref[lo, :], preferred_element_type=F32)
             + jnp.dot(x_hi, wu_ref[hi, :], preferred_element_type=F32))
        a = (jax.nn.silu(g) * u).astype(BF16)
        ys_ref[...] = _pack_row_halves(jnp.dot(a, wd_ref[...], preferred_element_type=F32))

    for cur in range(nbuf):
        mine = (i % nbuf == cur) & (i < n_used)
        pl.when(mine & (i + GATHER_AHEAD < n_used))(functools.partial(mlp, cur, True))
        pl.when(mine & (i + GATHER_AHEAD >= n_used))(functools.partial(mlp, cur, False))

    @pl.when(i >= n_used)
    def _():
        ys_ref[...] = jnp.zeros_like(ys_ref)


def moe_experts(h_rows, slot_tok, tile_expert, n_used, wg, wu, wd):
    dh = h_rows.shape[1]
    d = 2 * dh
    n_slots = slot_tok.shape[0]
    de = wg.shape[2]
    te = EXPERT_TILE
    return pl.pallas_call(
        functools.partial(_expert_kernel, te=te),
        grid_spec=pltpu.PrefetchScalarGridSpec(
            num_scalar_prefetch=3, grid=(n_slots // te,),
            in_specs=[pl.BlockSpec(memory_space=pl.ANY),
                      pl.BlockSpec((None, d, de), lambda i, e, n, tk: (e[i], 0, 0)),
                      pl.BlockSpec((None, d, de), lambda i, e, n, tk: (e[i], 0, 0)),
                      pl.BlockSpec((None, de, d), lambda i, e, n, tk: (e[i], 0, 0))],
            out_specs=pl.BlockSpec((te, dh), lambda i, e, n, tk: (i, 0)),
            scratch_shapes=[pltpu.VMEM((te, dh), jnp.uint32)] * (GATHER_AHEAD + 1)
                           + [pltpu.SemaphoreType.DMA((GATHER_AHEAD + 1,))]),
        out_shape=jax.ShapeDtypeStruct((n_slots, dh), jnp.uint32),
        compiler_params=_params("arbitrary"), name="moe_experts",
    )(tile_expert, n_used, slot_tok, h_rows, wg, wu, wd)


def _combine_kernel(slot_ref, gk_ref, sh_ref, x_ref, gate_ref, ng_ref, ys_ref, o_ref, *scratch, tm, n_tok):
    i = pl.program_id(0)
    nbuf = GATHER_AHEAD + 1
    bufs, sem = scratch[:nbuf], scratch[nbuf]

    @pl.when(i == 0)
    def _():
        for tile in range(GATHER_AHEAD):
            for k in range(TOP_K):
                def issue(r, _, k=k, tile=tile):
                    src = slot_ref[k * n_tok + tile * tm + r]
                    _row_copy(ys_ref, src, bufs[tile].at[k], r, sem.at[tile]).start()
                    return 0
                lax.fori_loop(0, tm, issue, 0, unroll=8)

    def tile(b, fetch):
        buf = bufs[b]
        for k in range(TOP_K):
            _rows_done(ys_ref, buf.at[k], sem.at[b])
        if fetch:
            nxt = (b + GATHER_AHEAD) % nbuf
            for k in range(TOP_K):
                for r in range(tm):
                    src = slot_ref[k * n_tok + (i + GATHER_AHEAD) * tm + r]
                    _row_copy(ys_ref, src, bufs[nxt].at[k], r, sem.at[nxt]).start()
        half = buf.shape[-1]
        lo, hi = pl.ds(0, half), pl.ds(half, half)
        y_lo, y_hi = sh_ref[:, lo].astype(F32), sh_ref[:, hi].astype(F32)
        gk = gk_ref[...]
        for k in range(TOP_K):
            e_lo, e_hi = _unpack_row_halves(buf[k])
            y_lo = y_lo + gk[:, k:k + 1] * e_lo
            y_hi = y_hi + gk[:, k:k + 1] * e_hi
        ssq = jnp.sum(y_lo * y_lo, axis=-1, keepdims=True) + jnp.sum(y_hi * y_hi, axis=-1, keepdims=True)
        inv = lax.rsqrt(ssq / (2 * half) + NORM_EPS)
        o_ref[:, lo] = x_ref[:, lo] + gate_ref[:, lo] * (y_lo * inv * ng_ref[:, lo])
        o_ref[:, hi] = x_ref[:, hi] + gate_ref[:, hi] * (y_hi * inv * ng_ref[:, hi])

    for b in range(nbuf):
        mine = i % nbuf == b
        pl.when(mine & (i + GATHER_AHEAD < pl.num_programs(0)))(functools.partial(tile, b, True))
        pl.when(mine & (i + GATHER_AHEAD >= pl.num_programs(0)))(functools.partial(tile, b, False))


def moe_combine(x, ys, slots, gk_t, shared, norm_g, gate):
    bsz, seq, d = x.shape
    tm = COMBINE_TILE
    nt = seq // tm
    return pl.pallas_call(
        functools.partial(_combine_kernel, tm=tm, n_tok=bsz * seq),
        grid_spec=pltpu.PrefetchScalarGridSpec(
            num_scalar_prefetch=1, grid=(bsz * nt,),
            in_specs=[pl.BlockSpec((tm, TOP_K), lambda i, s: (i, 0)),
                      pl.BlockSpec((tm, d), lambda i, s: (i, 0)),
                      pl.BlockSpec((None, tm, d), lambda i, s: (i // nt, i % nt, 0)),
                      pl.BlockSpec((None, 1, d), lambda i, s: (i // nt, 0, 0)),
                      pl.BlockSpec((1, d), lambda i, s: (0, 0)),
                      pl.BlockSpec(memory_space=pl.ANY)],
            out_specs=pl.BlockSpec((None, tm, d), lambda i, s: (i // nt, i % nt, 0)),
            scratch_shapes=[pltpu.VMEM((TOP_K, tm, d // 2), jnp.uint32)] * (GATHER_AHEAD + 1)
                           + [pltpu.SemaphoreType.DMA((GATHER_AHEAD + 1,))]),
        out_shape=jax.ShapeDtypeStruct(x.shape, F32),
        compiler_params=_params("arbitrary"), name="moe_combine",
    )(slots, gk_t, shared, x, gate.reshape(bsz, 1, d), norm_g.reshape(1, d), ys)


def moe_layer(x, h, h_rows, norm_g, gate, router_w, router_bias, e_gate, e_up, e_down, s_gate, s_up, s_down):
    t, d = h.shape
    n_exp = router_w.shape[1]
    te = EXPERT_TILE
    ek, pk, gk, counts = moe_route(h, router_w, router_bias)
    tiles = (counts + te - 1) // te
    tile_end = jnp.cumsum(tiles)
    offsets = (tile_end - tiles) * te
    experts = jnp.arange(n_exp, dtype=I32)
    slot = pk + jnp.sum(jnp.where(ek[..., None] == experts, offsets, 0), axis=-1)
    slots = slot.reshape(-1).astype(I32)
    n_tiles = (t * TOP_K) // te + n_exp
    n_used = tile_end[-1].astype(I32)
    tile_ids = jnp.arange(n_tiles, dtype=I32)
    tile_expert = jnp.sum((tile_end[None, :] <= jnp.minimum(tile_ids, n_used - 1)[:, None]).astype(I32), axis=1)
    pad_lo = jnp.concatenate([offsets + counts, tile_end[-1:] * te]).astype(I32)
    pad_hi = jnp.concatenate([tile_end * te, jnp.full((1,), n_tiles * te, tile_end.dtype)]).astype(I32)
    slot_tok = moe_slot_tokens(slots, pad_lo, pad_hi, n_tiles * te)
    ys = moe_experts(h_rows, slot_tok, tile_expert, n_used.reshape(1), e_gate, e_up, e_down)
    act = matmul_swiglu(h, s_gate, s_up, tm=1024, tn=512, name="shared_up")
    shared = matmul(act, s_down, BF16, tm=1024, tn=1024, name="shared_down")
    return moe_combine(x, ys, slots, gk.T, shared, norm_g, gate)


def kernel(x, c, positions, ada_w, ada_b, norm_g, ssm_w_in, ssm_a_re, ssm_a_im, ssm_log_step, ssm_b_re, ssm_b_im, ssm_c_re, ssm_c_im, ssm_d, glu_w, glu_b, ssm_w_out, kv_norm_g, w_k, w_v, attn_w_q, attn_w_o, router_w, router_bias, exp_w_gate, exp_w_up, exp_w_down, sh_w_gate, sh_w_up, sh_w_down):
    bsz, seq, d = x.shape
    depth = ada_w.shape[0]
    n_a = ssm_w_in.shape[0]
    t = bsz * seq
    mod = ada_modulation(c, ada_w, ada_b)
    k = v = None
    for layer in range(depth):
        sh1, sc1, g1, sh2, sc2, g2 = jnp.split(mod[layer], N_MOD, axis=-1)
        if layer < n_a:
            a = layer
            (h,) = norm_modulate(x, norm_g[layer, 0], sc1, sh1)
            u_tm = matmul_to_token_major(h, cast_bf16(ssm_w_in, a), bsz, tm=256, tn=512, name="ssm_in")
            bblk, cblk, ab = s5_operators(ssm_a_re[a], ssm_a_im[a], ssm_log_step[a], ssm_b_re[a], ssm_b_im[a],
                                          ssm_c_re[a], ssm_c_im[a])
            z = s5_scan(u_tm, bblk, cblk, ab, ssm_d[a], bsz)
            zz = matmul_glu(z, cast_bf16(glu_w, a), glu_b[a], tm=1024, tn=1024, name="ssm_glu")
            y = matmul_from_token_major(zz, cast_bf16(ssm_w_out, a), bsz, BF16, tm=256, tn=512, name="ssm_out")
        else:
            bl = layer - n_a
            n_dil = attn_w_q.shape[2] // d
            assert all(window // dil == ATTN_BLOCK for window, dil in DILATION_GROUPS[:n_dil])
            dils = tuple(dil for _, dil in DILATION_GROUPS[:n_dil])
            if layer == n_a:
                h, hkv = norm_modulate(x, norm_g[layer, 0], sc1, sh1, kv_g=kv_norm_g)
                kvd = w_k.shape[1]
                k_tabs = rope_tables(positions, 1.0)
                k = matmul_streams(hkv, cast_bf16(w_k[None], 0), 0, kvd, dils, k_tabs, tn=min(kvd, 512), name="k_proj")
                v = matmul_streams(hkv, cast_bf16(w_v[None], 0), 0, kvd, dils, None, tn=min(kvd, 512), name="v_proj")
            else:
                (h,) = norm_modulate(x, norm_g[layer, 0], sc1, sh1)
            q_tabs = rope_tables(positions, HEAD_DIM ** -0.5)
            wq = cast_bf16(attn_w_q, bl)
            tnq = min(d, 512)
            qs = [matmul_streams(h, wq, grp * (d // tnq), d, (dil,), q_tabs, tn=tnq, name=f"q_proj_{dil}")[0]
                  for grp, dil in enumerate(dils)]
            o = dilated_attention(qs, k, v, bsz, dils)
            y = matmul(o, cast_bf16(attn_w_o, bl), BF16, tm=1024, tn=1024, name="attn_out")
        x, h, h_rows = norm_modulate(x, norm_g[layer, 2], sc2, sh2, with_rows=True,
                                     resid=(y, norm_g[layer, 1], g1))
        x = moe_layer(x, h, h_rows, norm_g[layer, 3], g2, router_w[layer], router_bias[layer],
                      cast_bf16(exp_w_gate, layer), cast_bf16(exp_w_up, layer), cast_bf16(exp_w_down, layer),
                      cast_bf16(sh_w_gate, layer), cast_bf16(sh_w_up, layer), cast_bf16(sh_w_down, layer))
    return x
```

```python
import functools
import math

import jax
import jax.numpy as jnp
from jax import lax
from jax.experimental import pallas as pl
from jax.experimental.pallas import tpu as pltpu

F32 = jnp.float32
BF16 = jnp.bfloat16
I32 = jnp.int32

LANE = 128
SUBLANE = 8
VMEM_LIMIT_BYTES = 56 * 1024 * 1024
MXU_DIM = 256

N_MOD = 6
NORM_EPS = 1e-6
HEAD_DIM = 128
ROT_DIM = HEAD_DIM // 4
ROT_HALF = ROT_DIM // 2
ROPE_THETA = 500000.0
DILATION_GROUPS = ((128, 1), (512, 4), (2048, 16))
ATTN_BLOCK = 128
ATTN_SPAN = ATTN_BLOCK * max(dil for _, dil in DILATION_GROUPS)
N_EXPERT_GROUPS = 8
TOPK_GROUPS = 4
TOP_K = 4
ROUTED_SCALE = 2.5
NEG_MASK = -1e30

SSM_PACK_CH = MXU_DIM
SSM_CHUNK_TOKENS = 256
SSM_LANE_CHUNK = 512
EXPERT_TILE = 256
ROW_TILE = 256
COMBINE_TILE = 128
GATHER_AHEAD = 2
ROUTER_TILE = 512
CAST_BLOCK_BYTES = 4 * 1024 * 1024


def _params(*sem):
    return pltpu.CompilerParams(dimension_semantics=sem, vmem_limit_bytes=VMEM_LIMIT_BYTES)


def _cast_kernel(x_ref, o_ref):
    o_ref[...] = x_ref[...].astype(o_ref.dtype)


def cast_bf16(w, layer):
    lead, c = w.shape[1:-1], w.shape[-1]
    rows = math.prod(lead)
    w3 = w.reshape(w.shape[0], rows, c)
    br = min(rows, 1 << ((CAST_BLOCK_BYTES // (4 * c)).bit_length() - 1))
    assert rows % br == 0 and br % (2 * SUBLANE) == 0
    out = pl.pallas_call(
        _cast_kernel, grid=(rows // br,),
        in_specs=[pl.BlockSpec((None, br, c), lambda i: (layer, i, 0))],
        out_specs=pl.BlockSpec((br, c), lambda i: (i, 0)),
        out_shape=jax.ShapeDtypeStruct((rows, c), BF16),
        compiler_params=_params("parallel"), name="cast_bf16",
    )(w3)
    return out.reshape(*lead, c)


def _ada_kernel(c_ref, w_ref, b_ref, o_ref):
    cond = jax.nn.silu(c_ref[...])
    o_ref[...] = jnp.dot(cond, w_ref[...], precision=lax.Precision.HIGHEST,
                         preferred_element_type=F32) + b_ref[...]


def ada_modulation(c, ada_w, ada_b):
    depth, d, n = ada_w.shape
    bsz = c.shape[0]
    rows = -(-bsz // SUBLANE) * SUBLANE
    cp = jnp.zeros((rows, d), F32).at[:bsz].set(c)
    tn = 512
    out = pl.pallas_call(
        _ada_kernel,
        grid=(depth, n // tn),
        in_specs=[pl.BlockSpec((rows, d), lambda l, j: (0, 0)),
                  pl.BlockSpec((None, d, tn), lambda l, j: (l, 0, j)),
                  pl.BlockSpec((None, 1, tn), lambda l, j: (l, 0, j))],
        out_specs=pl.BlockSpec((None, rows, tn), lambda l, j: (l, 0, j)),
        out_shape=jax.ShapeDtypeStruct((depth, rows, n), F32),
        compiler_params=_params("parallel", "parallel"),
        name="ada_modulation",
    )(cp, ada_w, ada_b.reshape(depth, 1, n))
    return out[:, :bsz]


def _bf16_bits(v):
    return lax.bitcast_convert_type(v.astype(BF16).astype(F32), jnp.uint32)


def _pack_row_halves(v):
    half = v.shape[1] // 2
    return (_bf16_bits(v[:, :half]) >> 16) | _bf16_bits(v[:, half:])


def _unpack_row_halves(p):
    lo = lax.bitcast_convert_type(p << 16, F32)
    hi = lax.bitcast_convert_type(p & jnp.uint32(0xFFFF0000), F32)
    return lo, hi


def _normmod_kernel(x_ref, g_ref, sc_ref, sh_ref, *rest, with_kv, with_rows, with_resid):
    x = x_ref[...]
    rest = list(rest)
    if with_resid:
        y_ref, rg_ref, gate_ref = rest.pop(0), rest.pop(0), rest.pop(0)
        y = y_ref[...].astype(F32)
        x = x + gate_ref[...] * (y * lax.rsqrt(jnp.mean(y * y, axis=-1, keepdims=True) + NORM_EPS) * rg_ref[...])
    xn = x * lax.rsqrt(jnp.mean(x * x, axis=-1, keepdims=True) + NORM_EPS)
    h = xn * g_ref[...] * (1.0 + sc_ref[...]) + sh_ref[...]
    if with_kv:
        kvg_ref = rest.pop(0)
    if with_resid:
        rest.pop(0)[...] = x
    h_ref = rest.pop(0)
    h_ref[...] = h.astype(h_ref.dtype)
    if with_kv:
        rest.pop(0)[...] = (xn * kvg_ref[...]).astype(BF16)
    if with_rows:
        rest.pop(0)[...] = _pack_row_halves(h)


def norm_modulate(x, g, sc, sh, kv_g=None, with_rows=False, resid=None):
    bsz, seq, d = x.shape
    ts = ROW_TILE
    nt = seq // ts
    row = lambda b, i: (b * nt + i, 0)
    vec = pl.BlockSpec((1, d), lambda b, i: (0, 0))
    bvec = pl.BlockSpec((None, 1, d), lambda b, i: (b, 0, 0))
    xspec = pl.BlockSpec((None, ts, d), lambda b, i: (b, i, 0))
    in_specs = [xspec, vec, bvec, bvec]
    args = [x, g.reshape(1, d), sc.reshape(bsz, 1, d), sh.reshape(bsz, 1, d)]
    out_specs = [pl.BlockSpec((ts, d), row)]
    out_shape = [jax.ShapeDtypeStruct((bsz * seq, d), BF16)]
    if resid is not None:
        y, rg, gate = resid
        in_specs += [pl.BlockSpec((ts, d), row), vec, bvec]
        args += [y, rg.reshape(1, d), gate.reshape(bsz, 1, d)]
        out_specs.insert(0, xspec)
        out_shape.insert(0, jax.ShapeDtypeStruct(x.shape, F32))
    if kv_g is not None:
        in_specs.append(vec)
        args.append(kv_g.reshape(1, d))
        out_specs.append(pl.BlockSpec((ts, d), row))
        out_shape.append(jax.ShapeDtypeStruct((bsz * seq, d), BF16))
    if with_rows:
        out_specs.append(pl.BlockSpec((ts, d // 2), row))
        out_shape.append(jax.ShapeDtypeStruct((bsz * seq, d // 2), jnp.uint32))
    return pl.pallas_call(
        functools.partial(_normmod_kernel, with_kv=kv_g is not None, with_rows=with_rows,
                          with_resid=resid is not None),
        grid=(bsz, nt), in_specs=in_specs, out_specs=out_specs, out_shape=out_shape,
        compiler_params=_params("parallel", "parallel"), name="norm_modulate",
    )(*args)


def _mm_kernel(a_ref, w_ref, o_ref):
    o_ref[...] = jnp.dot(a_ref[...], w_ref[...], preferred_element_type=F32).astype(o_ref.dtype)


def _mm_glu_kernel(a_ref, w_ref, z_ref, b_ref, o_ref):
    acc = jnp.dot(a_ref[...], w_ref[...], preferred_element_type=F32) + b_ref[...]
    o_ref[...] = (z_ref[...].astype(F32) * jax.nn.sigmoid(acc)).astype(o_ref.dtype)


def _mm_swiglu_kernel(a_ref, wg_ref, wu_ref, o_ref):
    a = a_ref[...]
    g = jnp.dot(a, wg_ref[...], preferred_element_type=F32)
    u = jnp.dot(a, wu_ref[...], preferred_element_type=F32)
    o_ref[...] = (jax.nn.silu(g) * u).astype(o_ref.dtype)


def matmul(a, w, out_dtype, *, tm, tn, name):
    m, k = a.shape
    n = w.shape[1]
    tm, tn = min(tm, m), min(tn, n)
    return pl.pallas_call(
        _mm_kernel, grid=(m // tm, n // tn),
        in_specs=[pl.BlockSpec((tm, k), lambda i, j: (i, 0)), pl.BlockSpec((k, tn), lambda i, j: (0, j))],
        out_specs=pl.BlockSpec((tm, tn), lambda i, j: (i, j)),
        out_shape=jax.ShapeDtypeStruct((m, n), out_dtype),
        compiler_params=_params("parallel", "arbitrary"), name=name,
    )(a, w)


def matmul_swiglu(a, wg, wu, *, tm, tn, name):
    m, k = a.shape
    n = wg.shape[1]
    tm, tn = min(tm, m), min(tn, n)
    wspec = pl.BlockSpec((k, tn), lambda i, j: (0, j))
    return pl.pallas_call(
        _mm_swiglu_kernel, grid=(m // tm, n // tn),
        in_specs=[pl.BlockSpec((tm, k), lambda i, j: (i, 0)), wspec, wspec],
        out_specs=pl.BlockSpec((tm, tn), lambda i, j: (i, j)),
        out_shape=jax.ShapeDtypeStruct((m, n), BF16),
        compiler_params=_params("parallel", "arbitrary"), name=name,
    )(a, wg, wu)


def matmul_glu(z, w, b, *, tm, tn, name):
    m, k = z.shape
    n = w.shape[1]
    tm, tn = min(tm, m), min(tn, n)
    return pl.pallas_call(
        _mm_glu_kernel, grid=(m // tm, n // tn),
        in_specs=[pl.BlockSpec((tm, k), lambda i, j: (i, 0)), pl.BlockSpec((k, tn), lambda i, j: (0, j)),
                  pl.BlockSpec((tm, tn), lambda i, j: (i, j)), pl.BlockSpec((1, tn), lambda i, j: (0, j))],
        out_specs=pl.BlockSpec((tm, tn), lambda i, j: (i, j)),
        out_shape=jax.ShapeDtypeStruct((m, n), BF16),
        compiler_params=_params("parallel", "arbitrary"), name=name,
    )(z, w, z, b.reshape(1, n))


def _mm_to_token_major_kernel(a_ref, w_ref, o_ref, scr):
    bsz, tm, _ = a_ref.shape
    for b in range(bsz):
        acc = jnp.dot(a_ref[b], w_ref[...], preferred_element_type=F32)
        for j in range(scr.shape[0]):
            scr[j, pl.ds(b, tm, stride=bsz), :] = acc[:, j * LANE:(j + 1) * LANE]
    for j in range(scr.shape[0]):
        o_ref[:, j * LANE:(j + 1) * LANE] = scr[j].astype(o_ref.dtype)


def _mm_from_token_major_kernel(a_ref, w_ref, o_ref, scr):
    bsz, tm, _ = o_ref.shape
    acc = jnp.dot(a_ref[...], w_ref[...], preferred_element_type=F32)
    for j in range(scr.shape[0]):
        scr[j] = acc[:, j * LANE:(j + 1) * LANE]
    for b in range(bsz):
        for j in range(scr.shape[0]):
            o_ref[b, :, j * LANE:(j + 1) * LANE] = scr[j, pl.ds(b, tm, stride=bsz), :].astype(o_ref.dtype)


def matmul_to_token_major(a, w, bsz, *, tm, tn, name):
    m, k = a.shape
    n = w.shape[1]
    seq = m // bsz
    tm, tn = min(tm, seq), min(tn, n)
    return pl.pallas_call(
        _mm_to_token_major_kernel, grid=(seq // tm, n // tn),
        in_specs=[pl.BlockSpec((bsz, tm, k), lambda i, j: (0, i, 0)),
                  pl.BlockSpec((k, tn), lambda i, j: (0, j))],
        out_specs=pl.BlockSpec((bsz * tm, tn), lambda i, j: (i, j)),
        out_shape=jax.ShapeDtypeStruct((m, n), F32),
        scratch_shapes=[pltpu.VMEM((tn // LANE, bsz * tm, LANE), F32)],
        compiler_params=_params("parallel", "arbitrary"), name=name,
    )(a.reshape(bsz, seq, k), w)


def matmul_from_token_major(a, w, bsz, out_dtype, *, tm, tn, name):
    m, k = a.shape
    n = w.shape[1]
    seq = m // bsz
    tm, tn = min(tm, seq), min(tn, n)
    out = pl.pallas_call(
        _mm_from_token_major_kernel, grid=(seq // tm, n // tn),
        in_specs=[pl.BlockSpec((bsz * tm, k), lambda i, j: (i, 0)),
                  pl.BlockSpec((k, tn), lambda i, j: (0, j))],
        out_specs=pl.BlockSpec((bsz, tm, tn), lambda i, j: (0, i, j)),
        out_shape=jax.ShapeDtypeStruct((bsz, seq, n), out_dtype),
        scratch_shapes=[pltpu.VMEM((tn // LANE, bsz * tm, LANE), F32)],
        compiler_params=_params("parallel", "arbitrary"), name=name,
    )(a, w)
    return out.reshape(m, n)


def _s5_discretize_kernel(are_ref, aim_ref, ls_ref, abr_ref, abi_ref, fr_ref, fi_ref):
    lr, li = are_ref[...], aim_ref[...]
    step = jnp.exp(ls_ref[...])
    decay = jnp.exp(lr * step)
    abar_re = decay * jnp.cos(li * step)
    abar_im = decay * jnp.sin(li * step)
    denom = lr * lr + li * li
    nr = abar_re - 1.0
    ni = abar_im
    abr_ref[...] = abar_re
    abi_ref[...] = abar_im
    fr_ref[...] = (nr * lr + ni * li) / denom
    fi_ref[...] = (ni * lr - nr * li) / denom


def s5_discretize(a_re, a_im, log_step):
    g, p = a_re.shape
    shp = jax.ShapeDtypeStruct((g, p), F32)
    return pl.pallas_call(_s5_discretize_kernel, out_shape=[shp] * 4, name="s5_discretize")(
        a_re, a_im, log_step.reshape(g, 1))


def _s5_scan_kernel(un_ref, up_ref, bblk_ref, cblk_ref, ab_ref, d_ref, z_ref, x2_scr, xb2_scr, carry_scr,
                    *, rows, nstate, bsz):
    c = pl.program_id(1)
    n_chunks = pl.num_programs(1) - 1

    @pl.when(c == 0)
    def _():
        carry_scr[...] = jnp.zeros_like(carry_scr)
        xb2_scr[1] = jnp.zeros(xb2_scr.shape[1:], xb2_scr.dtype)
        x2_scr[0] = jnp.dot(up_ref[...].astype(BF16), bblk_ref[...], preferred_element_type=F32)

    u_prev = up_ref[...]
    x2_scr[(c + 1) % 2] = jnp.dot(un_ref[...].astype(BF16), bblk_ref[...], preferred_element_type=F32)
    y = jnp.dot(xb2_scr[(c + 1) % 2], cblk_ref[...], preferred_element_type=F32) + d_ref[...] * u_prev
    z_ref[...] = jax.nn.gelu(y).astype(z_ref.dtype)

    @pl.when(c < n_chunks)
    def _():
        _s5_recurrence(x2_scr.at[c % 2], xb2_scr.at[c % 2], ab_ref, carry_scr, rows=rows, nstate=nstate, bsz=bsz)


def _s5_recurrence(x_scr, xb_scr, ab_ref, carry_scr, *, rows, nstate, bsz):
    lw = SSM_LANE_CHUNK
    second = lax.broadcasted_iota(I32, (SUBLANE, lw), 0) >= bsz
    for lt in range(nstate // lw):
        re = slice(lt * lw, (lt + 1) * lw)
        im = slice(nstate + lt * lw, nstate + (lt + 1) * lw)
        ar, ai = ab_ref[0:1, re], ab_ref[1:2, re]
        a2r, a2i = ar * ar - ai * ai, 2.0 * ar * ai
        amr, ami = jnp.where(second, ar, 0.0), jnp.where(second, ai, 0.0)
        pr, pi = jnp.where(second, a2r, ar), jnp.where(second, a2i, ai)

        def advance(r0, cr, ci, re=re, im=im, amr=amr, ami=ami, pr=pr, pi=pi):
            bur, bui = x_scr[pl.ds(r0, SUBLANE), re], x_scr[pl.ds(r0, SUBLANE), im]
            sr, si = pltpu.roll(bur, bsz, 0), pltpu.roll(bui, bsz, 0)
            wr = bur + amr * sr - ami * si
            wi = bui + amr * si + ami * sr
            cbr = jnp.where(second, cr, pltpu.roll(cr, bsz, 0))
            cbi = jnp.where(second, ci, pltpu.roll(ci, bsz, 0))
            return wr + pr * cbr - pi * cbi, wi + pr * cbi + pi * cbr

        def body(n, carry, re=re, im=im, advance=advance):
            r0 = pl.multiple_of(n * (2 * SUBLANE), 2 * SUBLANE)
            xr0, xi0 = advance(r0, *carry)
            xr1, xi1 = advance(r0 + SUBLANE, xr0, xi0)
            xb_scr[pl.ds(r0, 2 * SUBLANE), re] = jnp.concatenate([xr0, xr1], axis=0).astype(BF16)
            xb_scr[pl.ds(r0, 2 * SUBLANE), im] = jnp.concatenate([xi0, xi1], axis=0).astype(BF16)
            return xr1, xi1

        cr, ci = lax.fori_loop(0, rows // (2 * SUBLANE), body, (carry_scr[:, re], carry_scr[:, im]))
        carry_scr[:, re] = cr
        carry_scr[:, im] = ci


def s5_scan(u_tm, bblk, cblk, ab, d_skip, bsz):
    m, width = u_tm.shape
    assert 2 * bsz == SUBLANE, "the scan tiles two tokens of every batch element into one sublane tile"
    packs, pack_ch, two_n = bblk.shape
    nstate = two_n // 2
    rows = SSM_CHUNK_TOKENS * bsz
    n_chunks = m // rows
    nxt = lambda p, c: (jnp.minimum(c + 1, n_chunks - 1), p)
    prv = lambda p, c: (jnp.clip(c - 1, 0, n_chunks - 1), p)
    return pl.pallas_call(
        functools.partial(_s5_scan_kernel, rows=rows, nstate=nstate, bsz=bsz),
        grid=(packs, n_chunks + 1),
        in_specs=[pl.BlockSpec((rows, pack_ch), nxt), pl.BlockSpec((rows, pack_ch), prv),
                  pl.BlockSpec((None, pack_ch, two_n), lambda p, c: (p, 0, 0)),
                  pl.BlockSpec((None, two_n, pack_ch), lambda p, c: (p, 0, 0)),
                  pl.BlockSpec((None, 2, nstate), lambda p, c: (p, 0, 0)),
                  pl.BlockSpec((None, 1, pack_ch), lambda p, c: (p, 0, 0))],
        out_specs=pl.BlockSpec((rows, pack_ch), prv),
        out_shape=jax.ShapeDtypeStruct((m, width), BF16),
        scratch_shapes=[pltpu.VMEM((2, rows, two_n), F32), pltpu.VMEM((2, rows, two_n), BF16),
                        pltpu.VMEM((SUBLANE, two_n), F32)],
        compiler_params=_params("parallel", "arbitrary"), name="s5_scan",
    )(u_tm, u_tm, bblk, cblk, ab, d_skip.reshape(packs, 1, pack_ch))


def s5_operators(a_re, a_im, log_step, b_re, b_im, c_re, c_im):
    g, p = a_re.shape
    gsz = b_re.shape[-1]
    gpp = SSM_PACK_CH // gsz
    packs = g // gpp
    abr, abi, fr, fi = s5_discretize(a_re, a_im, log_step)
    bbar_re = fr[..., None] * b_re - fi[..., None] * b_im
    bbar_im = fr[..., None] * b_im + fi[..., None] * b_re
    eye = jnp.eye(gpp, dtype=F32)

    def b_block(t):
        t = t.reshape(packs, gpp, p, gsz).transpose(0, 1, 3, 2)
        return (t[:, :, :, None, :] * eye[None, :, None, :, None]).reshape(packs, gpp * gsz, gpp * p)

    def c_block(t):
        t = t.reshape(packs, gpp, gsz, p).transpose(0, 1, 3, 2)
        return (t[:, :, :, None, :] * eye[None, :, None, :, None]).reshape(packs, gpp * p, gpp * gsz)

    bblk = jnp.concatenate([b_block(bbar_re), b_block(bbar_im)], axis=2).astype(BF16)
    cblk = jnp.concatenate([c_block(c_re), -c_block(c_im)], axis=1).astype(BF16)
    ab = jnp.stack([abr.reshape(packs, gpp * p), abi.reshape(packs, gpp * p)], axis=1)
    return bblk, cblk, ab


def _rope_table_kernel(pos_ref, invf_ref, cos_ref, sin_lo_ref, sin_hi_ref, *, scale):
    ang = pos_ref[...].astype(F32) * invf_ref[...]
    lane = lax.broadcasted_iota(I32, ang.shape, 1)
    c, s = jnp.cos(ang), jnp.sin(ang)
    cos_ref[...] = jnp.where(lane < ROT_DIM, c, 1.0) * scale
    sin_lo_ref[...] = jnp.where(lane < ROT_HALF, -s, 0.0) * scale
    sin_hi_ref[...] = jnp.where((lane >= ROT_HALF) & (lane < ROT_DIM), s, 0.0) * scale


def rope_tables(positions, scale):
    t = positions.size
    inv_freq = ROPE_THETA ** (-jnp.arange(ROT_HALF, dtype=F32) * 2.0 / ROT_DIM)
    invf = jnp.zeros((1, HEAD_DIM), F32).at[0, :ROT_DIM].set(jnp.concatenate([inv_freq, inv_freq]))
    ts = 1024
    shp = jax.ShapeDtypeStruct((t, HEAD_DIM), F32)
    spec = pl.BlockSpec((ts, HEAD_DIM), lambda i: (i, 0))
    return pl.pallas_call(
        functools.partial(_rope_table_kernel, scale=scale), grid=(t // ts,),
        in_specs=[pl.BlockSpec((ts, 1), lambda i: (i, 0)), pl.BlockSpec((1, HEAD_DIM), lambda i: (0, 0))],
        out_specs=[spec] * 3, out_shape=[shp] * 3,
        compiler_params=_params("parallel"), name="rope_tables",
    )(positions.reshape(t, 1), invf)


def _stream_rows(chunk, residue, dil):
    start = chunk * (ATTN_BLOCK * dil) + residue
    return pl.ds(start, ATTN_BLOCK) if dil == 1 else pl.ds(start, ATTN_BLOCK, stride=dil)


def _stream_major_store(o_ref, src_ref, dil):
    span = ATTN_BLOCK * dil
    for hd in range(src_ref.shape[0]):
        cols = slice(hd * HEAD_DIM, (hd + 1) * HEAD_DIM)
        for chunk in range(src_ref.shape[1] // span):
            for residue in range(dil):
                dst = pl.ds(chunk * span + residue * ATTN_BLOCK, ATTN_BLOCK)
                o_ref[dst, cols] = src_ref[hd, _stream_rows(chunk, residue, dil), :].astype(o_ref.dtype)


def _mm_streams_kernel(a_ref, w_ref, *rest, dils, rotary):
    rest = list(rest)
    if rotary:
        cos, sin_lo, sin_hi = [rest.pop(0)[...] for _ in range(3)]
    o_refs, scr = rest[:len(dils)], rest[len(dils)]
    acc = jnp.dot(a_ref[...], w_ref[...], preferred_element_type=F32)
    for hd in range(acc.shape[1] // HEAD_DIM):
        t = acc[:, hd * HEAD_DIM:(hd + 1) * HEAD_DIM]
        if rotary:
            t = t * cos + pltpu.roll(t, HEAD_DIM - ROT_HALF, 1) * sin_lo + pltpu.roll(t, ROT_HALF, 1) * sin_hi
        scr[hd] = t
    for o_ref, dil in zip(o_refs, dils):
        _stream_major_store(o_ref, scr, dil)


def matmul_streams(a, w, col_block0, n, dils, tables, *, tn, name):
    m, k = a.shape
    tm = ATTN_SPAN
    rotary = tables is not None
    in_specs = [pl.BlockSpec((tm, k), lambda i, j: (i, 0), pipeline_mode=pl.Buffered(1)),
                pl.BlockSpec((k, tn), lambda i, j: (0, col_block0 + j))]
    args = [a, w]
    if rotary:
        in_specs += [pl.BlockSpec((tm, HEAD_DIM), lambda i, j: (i, 0))] * 3
        args += list(tables)
    return pl.pallas_call(
        functools.partial(_mm_streams_kernel, dils=dils, rotary=rotary), grid=(m // tm, n // tn),
        in_specs=in_specs,
        out_specs=[pl.BlockSpec((tm, tn), lambda i, j: (i, j))] * len(dils),
        out_shape=[jax.ShapeDtypeStruct((m, n), BF16)] * len(dils),
        scratch_shapes=[pltpu.VMEM((tn // HEAD_DIM, tm, HEAD_DIM), F32)],
        compiler_params=_params("parallel", "arbitrary"), name=name,
    )(*args)


def _attn_kernel(*refs, dils):
    n = len(dils)
    q_refs = refs[:n]
    kv_refs = refs[n:5 * n]
    o_ref, o_scr, l_scr = refs[5 * n], refs[5 * n + 1], refs[5 * n + 2]
    blk = ATTN_BLOCK
    nb = ATTN_SPAN // blk
    seq_start = pl.program_id(1) == 0
    qi = lax.broadcasted_iota(I32, (nb, blk, blk), 1)
    kj = lax.broadcasted_iota(I32, (nb, blk, blk), 2)
    bi = lax.broadcasted_iota(I32, (nb, blk, blk), 0)
    cur_ok = kj <= qi
    qk = (((2,), (2,)), ((0,), (0,)))
    pv = (((2,), (1,)), ((0,), (0,)))
    for g, dil in enumerate(dils):
        kp_ref, kc_ref, vp_ref, vc_ref = kv_refs[4 * g:4 * g + 4]
        span = blk * dil
        kc, vc = kc_ref[...], vc_ref[...]
        if span == ATTN_SPAN:
            kp, vp = kp_ref[...], vp_ref[...]
        else:
            kp = jnp.concatenate([kp_ref[...], kc[:ATTN_SPAN - span]], axis=0)
            vp = jnp.concatenate([vp_ref[...], vc[:ATTN_SPAN - span]], axis=0)
        split = lambda t: t.reshape(nb, blk, HEAD_DIM)
        q3, kp3, kc3, vp3, vc3 = split(q_refs[g][...]), split(kp), split(kc), split(vp), split(vc)
        prev_ok = (kj >= qi) & jnp.logical_not(seq_start & (bi < dil))
        sp = jnp.where(prev_ok, lax.dot_general(q3, kp3, qk, preferred_element_type=F32), NEG_MASK)
        sc = jnp.where(cur_ok, lax.dot_general(q3, kc3, qk, preferred_element_type=F32), NEG_MASK)
        mx = jnp.maximum(jnp.max(sp, axis=-1, keepdims=True), jnp.max(sc, axis=-1, keepdims=True))
        pp, pc = jnp.exp(sp - mx), jnp.exp(sc - mx)
        den = jnp.sum(pp, axis=-1, keepdims=True) + jnp.sum(pc, axis=-1, keepdims=True)
        o = (lax.dot_general((pp / den).astype(BF16), vp3, pv, preferred_element_type=F32)
             + lax.dot_general((pc / den).astype(BF16), vc3, pv, preferred_element_type=F32))
        lse = jnp.broadcast_to(mx + jnp.log(den), (nb, blk, HEAD_DIM))
        for b in range(nb):
            dst = _stream_rows(b // dil, b % dil, dil)
            o_scr[g, dst, :] = o[b]
            l_scr[g, dst, :] = lse[b]
    ls = [l_scr[g] for g in range(n)]
    mx = functools.reduce(jnp.maximum, ls)
    es = [jnp.exp(l - mx) for l in ls]
    tot = functools.reduce(jnp.add, es)
    acc = o_scr[0] * (es[0] / tot)
    for g in range(1, n):
        acc = acc + o_scr[g] * (es[g] / tot)
    o_ref[...] = acc.astype(o_ref.dtype)


def dilated_attention(qs, ks, vs, bsz, dils):
    t, d = qs[0].shape
    kvd = ks[0].shape[1]
    rep = d // kvd
    seq = t // bsz
    assert seq % ATTN_SPAN == 0
    ns = seq // ATTN_SPAN
    cur = lambda b, c, h: (b * ns + c, h // rep)
    in_specs = [pl.BlockSpec((ATTN_SPAN, HEAD_DIM), lambda b, c, h: (b * ns + c, h))] * len(dils)
    args = list(qs)
    for g, dil in enumerate(dils):
        span = ATTN_BLOCK * dil
        per = ATTN_SPAN // span
        prev = lambda b, c, h, per=per: (jnp.maximum((b * ns + c) * per - 1, 0), h // rep)
        in_specs += [pl.BlockSpec((span, HEAD_DIM), prev), pl.BlockSpec((ATTN_SPAN, HEAD_DIM), cur)] * 2
        args += [ks[g], ks[g], vs[g], vs[g]]
    return pl.pallas_call(
        functools.partial(_attn_kernel, dils=dils),
        grid=(bsz, ns, d // HEAD_DIM),
        in_specs=in_specs,
        out_specs=pl.BlockSpec((ATTN_SPAN, HEAD_DIM), lambda b, c, h: (b * ns + c, h)),
        out_shape=jax.ShapeDtypeStruct((t, d), BF16),
        scratch_shapes=[pltpu.VMEM((len(dils), ATTN_SPAN, HEAD_DIM), F32)] * 2,
        compiler_params=_params("parallel", "arbitrary", "arbitrary"), name="dilated_attention",
    )(*args)


def _router_kernel(h_ref, wr_ref, bias_ref, ek_ref, pk_ref, gk_ref, cnt_ref, run_scr, *, n_exp):
    tm = h_ref.shape[0]
    per = n_exp // N_EXPERT_GROUPS
    ng = N_EXPERT_GROUPS

    @pl.when(pl.program_id(0) == 0)
    def _():
        run_scr[...] = jnp.zeros_like(run_scr)

    logits = lax.dot_general(wr_ref[...], h_ref[...], (((1,), (1,)), ((), ())), preferred_element_type=F32)
    scores = jax.nn.sigmoid(logits)
    biased = scores + bias_ref[...]
    sc = [scores[w * ng:(w + 1) * ng] for w in range(per)]
    bi = [biased[w * ng:(w + 1) * ng] for w in range(per)]
    assert per == 4
    hi01, lo01 = jnp.maximum(bi[0], bi[1]), jnp.minimum(bi[0], bi[1])
    hi23, lo23 = jnp.maximum(bi[2], bi[3]), jnp.minimum(bi[2], bi[3])
    group_score = jnp.maximum(hi01, hi23) + jnp.maximum(jnp.minimum(hi01, hi23), jnp.maximum(lo01, lo23))
    gidx = lax.broadcasted_iota(I32, (ng, tm), 0)
    grank = jnp.zeros((ng, tm), I32)
    for g2 in range(ng):
        row = group_score[g2:g2 + 1]
        grank += ((row > group_score) | ((row == group_score) & (g2 < gidx))).astype(I32)
    gsel = grank < TOPK_GROUPS
    cand = [jnp.where(gsel, b, -jnp.inf) for b in bi]
    eid = [gidx * per + w for w in range(per)]
    erank = [jnp.zeros((ng, tm), I32) for _ in range(per)]
    for w2 in range(per):
        for g2 in range(ng):
            row = cand[w2][g2:g2 + 1]
            e2 = g2 * per + w2
            for w in range(per):
                erank[w] += ((row > cand[w]) | ((row == cand[w]) & (e2 < eid[w]))).astype(I32)
    sel = [r < TOP_K for r in erank]
    ssum = functools.reduce(jnp.add, [jnp.sum(jnp.where(m, s, 0.0), axis=0, keepdims=True) for m, s in zip(sel, sc)])
    gate = [jnp.where(m, s / ssum * ROUTED_SCALE, 0.0) for m, s in zip(sel, sc)]
    sel_all = jnp.concatenate([m.astype(F32) for m in sel], axis=0)
    earlier = (lax.broadcasted_iota(I32, (tm, tm), 0) < lax.broadcasted_iota(I32, (tm, tm), 1)).astype(BF16)
    pos = jnp.dot(sel_all.astype(BF16), earlier, preferred_element_type=F32) + run_scr[...]
    run_scr[...] += jnp.sum(sel_all, axis=1, keepdims=True)
    cnt_ref[...] = run_scr[...].astype(I32)
    above = (lax.broadcasted_iota(I32, (n_exp, n_exp), 1) < lax.broadcasted_iota(I32, (n_exp, n_exp), 0)).astype(BF16)
    kidx = jnp.dot(above, sel_all.astype(BF16), preferred_element_type=F32)
    eid_all = jnp.concatenate(eid, axis=0).astype(F32)
    gate_all = jnp.concatenate(gate, axis=0)
    for k in range(TOP_K):
        mk = (sel_all > 0.0) & (kidx == float(k))
        ek_ref[k:k + 1, :] = jnp.sum(jnp.where(mk, eid_all, 0.0), axis=0, keepdims=True).astype(I32)
        pk_ref[k:k + 1, :] = jnp.sum(jnp.where(mk, pos, 0.0), axis=0, keepdims=True).astype(I32)
        gk_ref[k:k + 1, :] = jnp.sum(jnp.where(mk, gate_all, 0.0), axis=0, keepdims=True)


def moe_route(h, router_w, router_bias):
    t, d = h.shape
    n_exp = router_w.shape[1]
    per = n_exp // N_EXPERT_GROUPS
    wr = router_w.T.reshape(N_EXPERT_GROUPS, per, d).transpose(1, 0, 2).reshape(n_exp, d).astype(BF16)
    bias = router_bias.astype(F32).reshape(N_EXPERT_GROUPS, per).T.reshape(n_exp, 1)
    tm = ROUTER_TILE
    kspec = pl.BlockSpec((TOP_K, tm), lambda i: (0, i))
    ek, pk, gk, cnt = pl.pallas_call(
        functools.partial(_router_kernel, n_exp=n_exp), grid=(t // tm,),
        in_specs=[pl.BlockSpec((tm, d), lambda i: (i, 0)), pl.BlockSpec((n_exp, d), lambda i: (0, 0)),
                  pl.BlockSpec((n_exp, 1), lambda i: (0, 0))],
        out_specs=[kspec, kspec, kspec, pl.BlockSpec((n_exp, 1), lambda i: (0, 0))],
        out_shape=[jax.ShapeDtypeStruct((TOP_K, t), I32), jax.ShapeDtypeStruct((TOP_K, t), I32),
                   jax.ShapeDtypeStruct((TOP_K, t), F32), jax.ShapeDtypeStruct((n_exp, 1), I32)],
        scratch_shapes=[pltpu.VMEM((n_exp, 1), F32)],
        compiler_params=_params("arbitrary"), name="moe_router",
    )(h, wr, bias)
    counts = cnt.reshape(per, N_EXPERT_GROUPS).T.reshape(n_exp)
    return ek, pk, gk, counts


def _row_copy(src_ref, src_row, dst_ref, dst_row, sem):
    return pltpu.make_async_copy(src_ref.at[pl.ds(src_row, 1)], dst_ref.at[pl.ds(dst_row, 1)], sem)


def _rows_done(src_ref, dst_ref, sem):
    pltpu.make_async_copy(src_ref.at[pl.ds(0, dst_ref.shape[0])], dst_ref, sem).wait()


def _slot_token_kernel(slot_ref, pad_lo_ref, pad_hi_ref, tok_ref, *, n_tok, steps):
    j = pl.program_id(0)

    @pl.when(j == 0)
    def _():
        for e in range(pad_lo_ref.shape[0]):
            def clear(s, _):
                tok_ref[s] = 0
                return 0
            lax.fori_loop(pad_lo_ref[e], pad_hi_ref[e], clear, 0)

    per = n_tok // steps
    for k in range(TOP_K):
        def put(r, _, k=k):
            tok = j * per + r
            tok_ref[slot_ref[k * n_tok + tok]] = tok
            return 0
        lax.fori_loop(0, per, put, 0, unroll=8)


def moe_slot_tokens(slots, pad_lo, pad_hi, n_slots):
    n_tok = slots.shape[0] // TOP_K
    steps = 64
    assert n_tok % steps == 0
    return pl.pallas_call(
        functools.partial(_slot_token_kernel, n_tok=n_tok, steps=steps),
        grid_spec=pltpu.PrefetchScalarGridSpec(
            num_scalar_prefetch=3, grid=(steps,), in_specs=[],
            out_specs=pl.BlockSpec(memory_space=pltpu.SMEM)),
        out_shape=jax.ShapeDtypeStruct((n_slots,), I32),
        compiler_params=_params("arbitrary"), name="moe_slot_tokens",
    )(slots, pad_lo, pad_hi)


def _expert_kernel(te_ref, nused_ref, tok_ref, h_ref, wg_ref, wu_ref, wd_ref, ys_ref, *scratch, te):
    i = pl.program_id(0)
    n_used = nused_ref[0]
    nbuf = GATHER_AHEAD + 1
    xbufs, sem = scratch[:nbuf], scratch[nbuf]

    @pl.when(i == 0)
    def _():
        for tile in range(GATHER_AHEAD):
            @pl.when(tile < n_used)
            def _(tile=tile):
                def issue(r, _):
                    _row_copy(h_ref, tok_ref[tile * te + r], xbufs[tile], r, sem.at[tile]).start()
                    return 0
                lax.fori_loop(0, te, issue, 0, unroll=8)

    def mlp(cur, fetch):
        _rows_done(h_ref, xbufs[cur], sem.at[cur])
        if fetch:
            nxt = (cur + GATHER_AHEAD) % nbuf
            for r in range(te):
                _row_copy(h_ref, tok_ref[(i + GATHER_AHEAD) * te + r], xbufs[nxt], r, sem.at[nxt]).start(priority=r % 2)
        x_lo, x_hi = [t.astype(BF16) for t in _unpack_row_halves(xbufs[cur][...])]
        half = x_lo.shape[1]
        lo, hi = pl.ds(0, half), pl.ds(half, half)
        g = (jnp.dot(x_lo, wg_ref[lo, :], preferred_element_type=F32)
             + jnp.dot(x_hi, wg_ref[hi, :], preferred_element_type=F32))
        u = (jnp.dot(x_lo, wu_ref[lo, :], preferred_element_type=F32)
             + jnp.dot(x_hi, wu_ref[hi, :], preferred_element_type=F32))
        a = (jax.nn.silu(g) * u).astype(BF16)
        ys_ref[...] = _pack_row_halves(jnp.dot(a, wd_ref[...], preferred_element_type=F32))

    for cur in range(nbuf):
        mine = (i % nbuf == cur) & (i < n_used)
        pl.when(mine & (i + GATHER_AHEAD < n_used))(functools.partial(mlp, cur, True))
        pl.when(mine & (i + GATHER_AHEAD >= n_used))(functools.partial(mlp, cur, False))

    @pl.when(i >= n_used)
    def _():
        ys_ref[...] = jnp.zeros_like(ys_ref)


def moe_experts(h_rows, slot_tok, tile_expert, n_used, wg, wu, wd):
    dh = h_rows.shape[1]
    d = 2 * dh
    n_slots = slot_tok.shape[0]
    de = wg.shape[2]
    te = EXPERT_TILE
    return pl.pallas_call(
        functools.partial(_expert_kernel, te=te),
        grid_spec=pltpu.PrefetchScalarGridSpec(
            num_scalar_prefetch=3, grid=(n_slots // te,),
            in_specs=[pl.BlockSpec(memory_space=pl.ANY),
                      pl.BlockSpec((None, d, de), lambda i, e, n, tk: (e[i], 0, 0)),
                      pl.BlockSpec((None, d, de), lambda i, e, n, tk: (e[i], 0, 0)),
                      pl.BlockSpec((None, de, d), lambda i, e, n, tk: (e[i], 0, 0))],
            out_specs=pl.BlockSpec((te, dh), lambda i, e, n, tk: (i, 0)),
            scratch_shapes=[pltpu.VMEM((te, dh), jnp.uint32)] * (GATHER_AHEAD + 1)
                           + [pltpu.SemaphoreType.DMA((GATHER_AHEAD + 1,))]),
        out_shape=jax.ShapeDtypeStruct((n_slots, dh), jnp.uint32),
        compiler_params=_params("arbitrary"), name="moe_experts",
    )(tile_expert, n_used, slot_tok, h_rows, wg, wu, wd)


def _combine_kernel(slot_ref, gk_ref, sh_ref, x_ref, gate_ref, ng_ref, ys_ref, o_ref, *scratch, tm, n_tok):
    i = pl.program_id(0)
    nbuf = GATHER_AHEAD + 1
    bufs, sem = scratch[:nbuf], scratch[nbuf]

    @pl.when(i == 0)
    def _():
        for tile in range(GATHER_AHEAD):
            for k in range(TOP_K):
                def issue(r, _, k=k, tile=tile):
                    src = slot_ref[k * n_tok + tile * tm + r]
                    _row_copy(ys_ref, src, bufs[tile].at[k], r, sem.at[tile]).start()
                    return 0
                lax.fori_loop(0, tm, issue, 0, unroll=8)

    def tile(b, fetch):
        buf = bufs[b]
        for k in range(TOP_K):
            _rows_done(ys_ref, buf.at[k], sem.at[b])
        if fetch:
            nxt = (b + GATHER_AHEAD) % nbuf
            for k in range(TOP_K):
                for r in range(tm):
                    src = slot_ref[k * n_tok + (i + GATHER_AHEAD) * tm + r]
                    _row_copy(ys_ref, src, bufs[nxt].at[k], r, sem.at[nxt]).start(priority=r % 2)
        half = buf.shape[-1]
        lo, hi = pl.ds(0, half), pl.ds(half, half)
        y_lo, y_hi = sh_ref[:, lo].astype(F32), sh_ref[:, hi].astype(F32)
        gk = gk_ref[...]
        for k in range(TOP_K):
            e_lo, e_hi = _unpack_row_halves(buf[k])
            y_lo = y_lo + gk[:, k:k + 1] * e_lo
            y_hi = y_hi + gk[:, k:k + 1] * e_hi
        ssq = jnp.sum(y_lo * y_lo, axis=-1, keepdims=True) + jnp.sum(y_hi * y_hi, axis=-1, keepdims=True)
        inv = lax.rsqrt(ssq / (2 * half) + NORM_EPS)
        o_ref[:, lo] = x_ref[:, lo] + gate_ref[:, lo] * (y_lo * inv * ng_ref[:, lo])
        o_ref[:, hi] = x_ref[:, hi] + gate_ref[:, hi] * (y_hi * inv * ng_ref[:, hi])

    for b in range(nbuf):
        mine = i % nbuf == b
        pl.when(mine & (i + GATHER_AHEAD < pl.num_programs(0)))(functools.partial(tile, b, True))
        pl.when(mine & (i + GATHER_AHEAD >= pl.num_programs(0)))(functools.partial(tile, b, False))


def moe_combine(x, ys, slots, gk_t, shared, norm_g, gate):
    bsz, seq, d = x.shape
    tm = COMBINE_TILE
    nt = seq // tm
    return pl.pallas_call(
        functools.partial(_combine_kernel, tm=tm, n_tok=bsz * seq),
        grid_spec=pltpu.PrefetchScalarGridSpec(
            num_scalar_prefetch=1, grid=(bsz * nt,),
            in_specs=[pl.BlockSpec((tm, TOP_K), lambda i, s: (i, 0)),
                      pl.BlockSpec((tm, d), lambda i, s: (i, 0)),
                      pl.BlockSpec((None, tm, d), lambda i, s: (i // nt, i % nt, 0)),
                      pl.BlockSpec((None, 1, d), lambda i, s: (i // nt, 0, 0)),
                      pl.BlockSpec((1, d), lambda i, s: (0, 0)),
                      pl.BlockSpec(memory_space=pl.ANY)],
            out_specs=pl.BlockSpec((None, tm, d), lambda i, s: (i // nt, i % nt, 0)),
            scratch_shapes=[pltpu.VMEM((TOP_K, tm, d // 2), jnp.uint32)] * (GATHER_AHEAD + 1)
                           + [pltpu.SemaphoreType.DMA((GATHER_AHEAD + 1,))]),
        out_shape=jax.ShapeDtypeStruct(x.shape, F32),
        compiler_params=_params("arbitrary"), name="moe_combine",
    )(slots, gk_t, shared, x, gate.reshape(bsz, 1, d), norm_g.reshape(1, d), ys)


def moe_layer(x, h, h_rows, norm_g, gate, router_w, router_bias, e_gate, e_up, e_down, s_gate, s_up, s_down):
    t, d = h.shape
    n_exp = router_w.shape[1]
    te = EXPERT_TILE
    ek, pk, gk, counts = moe_route(h, router_w, router_bias)
    tiles = (counts + te - 1) // te
    tile_end = jnp.cumsum(tiles)
    offsets = (tile_end - tiles) * te
    experts = jnp.arange(n_exp, dtype=I32)
    slot = pk + jnp.sum(jnp.where(ek[..., None] == experts, offsets, 0), axis=-1)
    slots = slot.reshape(-1).astype(I32)
    n_tiles = (t * TOP_K) // te + n_exp
    n_used = tile_end[-1].astype(I32)
    tile_ids = jnp.arange(n_tiles, dtype=I32)
    tile_expert = jnp.sum((tile_end[None, :] <= jnp.minimum(tile_ids, n_used - 1)[:, None]).astype(I32), axis=1)
    pad_lo = jnp.concatenate([offsets + counts, tile_end[-1:] * te]).astype(I32)
    pad_hi = jnp.concatenate([tile_end * te, jnp.full((1,), n_tiles * te, tile_end.dtype)]).astype(I32)
    slot_tok = moe_slot_tokens(slots, pad_lo, pad_hi, n_tiles * te)
    ys = moe_experts(h_rows, slot_tok, tile_expert, n_used.reshape(1), e_gate, e_up, e_down)
    act = matmul_swiglu(h, s_gate, s_up, tm=1024, tn=512, name="shared_up")
    shared = matmul(act, s_down, BF16, tm=1024, tn=1024, name="shared_down")
    return moe_combine(x, ys, slots, gk.T, shared, norm_g, gate)


def kernel(x, c, positions, ada_w, ada_b, norm_g, ssm_w_in, ssm_a_re, ssm_a_im, ssm_log_step, ssm_b_re, ssm_b_im, ssm_c_re, ssm_c_im, ssm_d, glu_w, glu_b, ssm_w_out, kv_norm_g, w_k, w_v, attn_w_q, attn_w_o, router_w, router_bias, exp_w_gate, exp_w_up, exp_w_down, sh_w_gate, sh_w_up, sh_w_down):
    bsz, seq, d = x.shape
    depth = ada_w.shape[0]
    n_a = ssm_w_in.shape[0]
    t = bsz * seq
    mod = ada_modulation(c, ada_w, ada_b)
    k = v = None
    for layer in range(depth):
        sh1, sc1, g1, sh2, sc2, g2 = jnp.split(mod[layer], N_MOD, axis=-1)
        if layer < n_a:
            a = layer
            (h,) = norm_modulate(x, norm_g[layer, 0], sc1, sh1)
            u_tm = matmul_to_token_major(h, cast_bf16(ssm_w_in, a), bsz, tm=256, tn=512, name="ssm_in")
            bblk, cblk, ab = s5_operators(ssm_a_re[a], ssm_a_im[a], ssm_log_step[a], ssm_b_re[a], ssm_b_im[a],
                                          ssm_c_re[a], ssm_c_im[a])
            z = s5_scan(u_tm, bblk, cblk, ab, ssm_d[a], bsz)
            zz = matmul_glu(z, cast_bf16(glu_w, a), glu_b[a], tm=1024, tn=1024, name="ssm_glu")
            y = matmul_from_token_major(zz, cast_bf16(ssm_w_out, a), bsz, BF16, tm=256, tn=512, name="ssm_out")
        else:
            bl = layer - n_a
            n_dil = attn_w_q.shape[2] // d
            assert all(window // dil == ATTN_BLOCK for window, dil in DILATION_GROUPS[:n_dil])
            dils = tuple(dil for _, dil in DILATION_GROUPS[:n_dil])
            if layer == n_a:
                h, hkv = norm_modulate(x, norm_g[layer, 0], sc1, sh1, kv_g=kv_norm_g)
                kvd = w_k.shape[1]
                k_tabs = rope_tables(positions, 1.0)
                k = matmul_streams(hkv, cast_bf16(w_k[None], 0), 0, kvd, dils, k_tabs, tn=min(kvd, 512), name="k_proj")
                v = matmul_streams(hkv, cast_bf16(w_v[None], 0), 0, kvd, dils, None, tn=min(kvd, 512), name="v_proj")
            else:
                (h,) = norm_modulate(x, norm_g[layer, 0], sc1, sh1)
            q_tabs = rope_tables(positions, HEAD_DIM ** -0.5)
            wq = cast_bf16(attn_w_q, bl)
            tnq = min(d, 512)
            qs = [matmul_streams(h, wq, grp * (d // tnq), d, (dil,), q_tabs, tn=tnq, name=f"q_proj_{dil}")[0]
                  for grp, dil in enumerate(dils)]
            o = dilated_attention(qs, k, v, bsz, dils)
            y = matmul(o, cast_bf16(attn_w_o, bl), BF16, tm=1024, tn=1024, name="attn_out")
        x, h, h_rows = norm_modulate(x, norm_g[layer, 2], sc2, sh2, with_rows=True,
                                     resid=(y, norm_g[layer, 1], g1))
        x = moe_layer(x, h, h_rows, norm_g[layer, 3], g2, router_w[layer], router_bias[layer],
                      cast_bf16(exp_w_gate, layer), cast_bf16(exp_w_up, layer), cast_bf16(exp_w_down, layer),
                      cast_bf16(sh_w_gate, layer), cast_bf16(sh_w_up, layer), cast_bf16(sh_w_down, layer))
    return x
```
